```python
import jax, jax.numpy as jnp
from jax import lax
import numpy as np


D_MODEL = 1024
BATCH = 8
SEQ = 4096
DEPTH = 4

GRID_W = 64
CTX_LEN = 256
N_MIXERS = 2
N_ATT_LAYERS = (DEPTH + 1) // 2
N_RET_LAYERS = DEPTH // 2

ATT_HEADS = 8
ATT_KV_HEADS = 2
ATT_HEAD_DIM = D_MODEL // ATT_HEADS
ATT_GROUP = ATT_HEADS // ATT_KV_HEADS
ATT_WIDTH = ATT_HEADS * ATT_HEAD_DIM
ATT_KV_WIDTH = ATT_KV_HEADS * ATT_HEAD_DIM
ATT_IN = 2 * ATT_WIDTH + 2 * ATT_KV_WIDTH
Q_BLOCK = 128
ROPE_THETA = 10000.0

RET_HEADS = 4
RET_QK_DIM = D_MODEL // RET_HEADS
RET_V_DIM = 2 * RET_QK_DIM
RET_QK_WIDTH = RET_HEADS * RET_QK_DIM
RET_V_WIDTH = RET_HEADS * RET_V_DIM
RET_IN = 2 * RET_QK_WIDTH + 2 * RET_V_WIDTH
RET_CHUNK = 128

DEEPNORM_ALPHA = (2.0 * DEPTH) ** 0.25
DEEPNORM_BETA = (8.0 * DEPTH) ** -0.25
LN_EPS = 1e-5
QK_EPS = 1e-6
GN_EPS = 1e-5

kernel_name = 'hybrid_gqa_retention_prefix_flow_block'


def layer_norm(x, g, b):
    xf = x.astype(jnp.float32)
    mu = xf.mean(-1, keepdims=True)
    var = jnp.square(xf - mu).mean(-1, keepdims=True)
    return ((xf - mu) * lax.rsqrt(var + LN_EPS) * g.astype(jnp.float32) + b.astype(jnp.float32)).astype(x.dtype)


def rms_norm(x, g):
    xf = x.astype(jnp.float32)
    ms = jnp.square(xf).mean(-1, keepdims=True)
    return (xf * lax.rsqrt(ms + QK_EPS) * g.astype(jnp.float32)).astype(x.dtype)


def rope_1d(x, pos):
    half = x.shape[-1] // 2
    freqs = ROPE_THETA ** (-jnp.arange(half, dtype=jnp.float32) / half)
    ang = pos.astype(jnp.float32)[:, None] * freqs[None, :]
    cos = jnp.cos(ang)[:, None, :]
    sin = jnp.sin(ang)[:, None, :]
    xf = x.astype(jnp.float32)
    x1, x2 = xf[..., :half], xf[..., half:]
    return jnp.concatenate([x1 * cos - x2 * sin, x1 * sin + x2 * cos], axis=-1).astype(x.dtype)


def axial_rope(x, row, col):
    half = x.shape[-1] // 2
    return jnp.concatenate([rope_1d(x[..., :half], row), rope_1d(x[..., half:], col)], axis=-1)


def attend(q, k, v):
    s = jnp.einsum('bqkgd,bskd->bkgqs', q, k).astype(jnp.float32) * (ATT_HEAD_DIM ** -0.5)
    p = jax.nn.softmax(s, axis=-1).astype(v.dtype)
    return jnp.einsum('bkgqs,bskd->bqkgd', p, v)


def attention_mixer(h_lat, h_ctx, w_in, w_out, q_scale, k_scale, row, col, need_ctx):
    B, S, _ = h_lat.shape
    L = h_ctx.shape[1]

    def project(h):
        n = h.shape[1]
        p = h @ w_in
        q, g, k, v = jnp.split(p, [ATT_WIDTH, 2 * ATT_WIDTH, 2 * ATT_WIDTH + ATT_KV_WIDTH], axis=-1)
        q = rms_norm(q.reshape(B, n, ATT_HEADS, ATT_HEAD_DIM), q_scale)
        k = rms_norm(k.reshape(B, n, ATT_KV_HEADS, ATT_HEAD_DIM), k_scale)
        v = v.reshape(B, n, ATT_KV_HEADS, ATT_HEAD_DIM)
        return q, k, v, g

    q_l, k_l, v_l, g_l = project(h_lat)
    q_l = axial_rope(q_l, row, col)
    k_l = axial_rope(k_l, row, col)
    q_c, k_c, v_c, g_c = project(h_ctx)
    k_all = jnp.concatenate([k_l, k_c], axis=1)
    v_all = jnp.concatenate([v_l, v_c], axis=1)

    nb = S // Q_BLOCK
    qb = q_l.reshape(B, nb, Q_BLOCK, ATT_KV_HEADS, ATT_GROUP, ATT_HEAD_DIM).transpose(1, 0, 2, 3, 4, 5)
    o_l = lax.map(lambda qblk: attend(qblk, k_all, v_all), qb)
    o_l = o_l.transpose(1, 0, 2, 3, 4, 5).reshape(B, S, ATT_WIDTH)
    y_l = (o_l * jax.nn.silu(g_l)) @ w_out
    y_c = None
    if need_ctx:
        o_c = attend(q_c.reshape(B, L, ATT_KV_HEADS, ATT_GROUP, ATT_HEAD_DIM), k_c, v_c).reshape(B, L, ATT_WIDTH)
        y_c = (o_c * jax.nn.silu(g_c)) @ w_out
    return y_l, y_c


def decayed_state(k, v, log_g):
    n = k.shape[2]
    w = jnp.exp(log_g.astype(jnp.float32)[:, None] * (n - 1 - jnp.arange(n, dtype=jnp.float32))[None, :])
    return jnp.einsum('bhld,hl,bhle->bhde', k, w.astype(k.dtype), v)


def retention_chunkwise(q, k, v, log_g, state0, strict):
    B, H, N, _ = q.shape
    dv = v.shape[-1]
    C = RET_CHUNK
    nc = N // C
    idx = jnp.arange(C, dtype=jnp.float32)
    lg = log_g.astype(jnp.float32)[:, None]
    diff = idx[:, None] - idx[None, :]
    mask = (diff > 0) if strict else (diff >= 0)
    decay = jnp.where(mask, jnp.exp(lg[:, :, None] * jnp.maximum(diff, 0.0)), 0.0).astype(q.dtype)
    xi = jnp.exp(lg * (idx + 1.0)).astype(q.dtype)
    zeta = jnp.exp(lg * (C - 1.0 - idx)).astype(q.dtype)
    g_chunk = jnp.exp(lg[:, 0] * C).astype(q.dtype)

    def chunks(t):
        return t.reshape(B, H, nc, C, t.shape[-1]).transpose(2, 0, 1, 3, 4)

    def step(state, qkv):
        qc, kc, vc = qkv
        s = jnp.einsum('bhnd,bhmd->bhnm', qc, kc) * decay
        o = jnp.einsum('bhnm,bhme->bhne', s, vc) + jnp.einsum('bhnd,bhde->bhne', qc, state) * xi[..., None]
        state = state * g_chunk[:, None, None] + jnp.einsum('bhmd,bhme->bhde', kc * zeta[..., None], vc)
        return state, o

    _, o = lax.scan(step, state0.astype(q.dtype), (chunks(q), chunks(k), chunks(v)))
    return o.transpose(1, 2, 0, 3, 4).reshape(B, H, N, dv)


def retention_mixer(h_lat, h_ctx, w_in, w_out, gn_g, lg_f, lg_b, need_ctx):
    B, S, _ = h_lat.shape
    L = h_ctx.shape[1]

    def project(h, pos):
        n = h.shape[1]
        p = h @ w_in
        q, k, v, g = jnp.split(p, [RET_QK_WIDTH, 2 * RET_QK_WIDTH, 2 * RET_QK_WIDTH + RET_V_WIDTH], axis=-1)
        q = rope_1d(q.reshape(B, n, RET_HEADS, RET_QK_DIM), pos).transpose(0, 2, 1, 3)
        k = (rope_1d(k.reshape(B, n, RET_HEADS, RET_QK_DIM), pos) * (RET_QK_DIM ** -0.5)).transpose(0, 2, 1, 3)
        v = v.reshape(B, n, RET_HEADS, RET_V_DIM).transpose(0, 2, 1, 3)
        return q, k, v, g

    def flip(t):
        return jnp.flip(t, axis=2)

    def output(o, g):
        of = o.astype(jnp.float32)
        mu = of.mean(-1, keepdims=True)
        var = jnp.square(of - mu).mean(-1, keepdims=True)
        on = ((of - mu) * lax.rsqrt(var + GN_EPS) * gn_g.astype(jnp.float32).reshape(RET_HEADS, 1, RET_V_DIM)).astype(o.dtype)
        n = o.shape[2]
        on = on.transpose(0, 2, 1, 3).reshape(B, n, RET_V_WIDTH)
        return (on * jax.nn.silu(g)) @ w_out

    q_c, k_c, v_c, g_c = project(h_ctx, jnp.arange(L))
    q_l, k_l, v_l, g_l = project(h_lat, L + jnp.arange(S))
    s_f = decayed_state(k_c, v_c, lg_f)
    s_b = decayed_state(flip(k_c), flip(v_c), lg_b)
    o_l = (retention_chunkwise(q_l, k_l, v_l, lg_f, s_f, False)
           + flip(retention_chunkwise(flip(q_l), flip(k_l), flip(v_l), lg_b, s_b, True)))
    y_l = output(o_l, g_l)
    y_c = None
    if need_ctx:
        zero = jnp.zeros_like(s_f)
        o_c = (retention_chunkwise(q_c, k_c, v_c, lg_f, zero, False)
               + flip(retention_chunkwise(flip(q_c), flip(k_c), flip(v_c), lg_b, zero, True)))
        y_c = output(o_c, g_c)
    return y_l, y_c


def setup_inputs(seed: int = 0) -> dict:
    key = jax.random.key(seed)
    ks = jax.random.split(key, 20)
    f32 = jnp.float32
    nrm = lambda k, shape: jax.random.normal(k, shape, dtype=f32)
    base_decay = jnp.log(1.0 - 2.0 ** (-5.0 - jnp.arange(RET_HEADS, dtype=f32)))
    return {
        'x': nrm(ks[0], (BATCH, SEQ, D_MODEL)),
        'c': nrm(ks[1], (BATCH, D_MODEL)),
        'ctx': nrm(ks[2], (BATCH, CTX_LEN, D_MODEL)),
        'c_ctx': nrm(ks[3], (D_MODEL,)),
        'mod_w': nrm(ks[4], (DEPTH, D_MODEL, 3 * D_MODEL)) * (0.5 * D_MODEL ** -0.5),
        'mod_b': nrm(ks[5], (DEPTH, 3 * D_MODEL)) * 0.01,
        'ln_g': 1.0 + 0.02 * nrm(ks[6], (DEPTH, D_MODEL)),
        'ln_b': 0.02 * nrm(ks[7], (DEPTH, D_MODEL)),
        'attn_w_in': nrm(ks[8], (N_ATT_LAYERS, D_MODEL, ATT_IN)) * (D_MODEL ** -0.5),
        'attn_w_out': nrm(ks[9], (N_ATT_LAYERS, ATT_WIDTH, D_MODEL)) * (ATT_WIDTH ** -0.5) * DEEPNORM_BETA,
        'attn_q_scale': 1.0 + 0.02 * nrm(ks[10], (N_ATT_LAYERS, ATT_HEAD_DIM)),
        'attn_k_scale': 1.0 + 0.02 * nrm(ks[11], (N_ATT_LAYERS, ATT_HEAD_DIM)),
        'ret_w_in': nrm(ks[12], (N_RET_LAYERS, D_MODEL, RET_IN)) * (D_MODEL ** -0.5),
        'ret_w_out': nrm(ks[13], (N_RET_LAYERS, RET_V_WIDTH, D_MODEL)) * (RET_V_WIDTH ** -0.5) * DEEPNORM_BETA,
        'ret_gn_g': 1.0 + 0.02 * nrm(ks[14], (N_RET_LAYERS, RET_V_WIDTH)),
        'ret_log_decay_fwd': base_decay[None, :] * jnp.exp(0.1 * nrm(ks[15], (N_RET_LAYERS, RET_HEADS))),
        'ret_log_decay_bwd': base_decay[None, :] * jnp.exp(0.1 * nrm(ks[16], (N_RET_LAYERS, RET_HEADS))),
    }


def reference(x, c, ctx, c_ctx, mod_w, mod_b, ln_g, ln_b, attn_w_in, attn_w_out, attn_q_scale, attn_k_scale,
              ret_w_in, ret_w_out, ret_gn_g, ret_log_decay_fwd, ret_log_decay_bwd):
    S = x.shape[1]
    ROWS = S // GRID_W
    row = jnp.repeat(jnp.arange(ROWS), GRID_W)
    col = jnp.tile(jnp.arange(GRID_W), ROWS)
    sc = jax.nn.silu(c)
    scc = jax.nn.silu(c_ctx)
    for i in range(DEPTH):
        need_ctx = i < DEPTH - 1
        shift, scale, gate = jnp.split(sc @ mod_w[i] + mod_b[i], 3, axis=-1)
        shift_c, scale_c, gate_c = jnp.split(scc @ mod_w[i] + mod_b[i], 3, axis=-1)
        h_lat = x * (1.0 + scale[:, None, :]) + shift[:, None, :]
        h_ctx = ctx * (1.0 + scale_c) + shift_c
        j = i // N_MIXERS
        if i % N_MIXERS == 0:
            y_l, y_c = attention_mixer(h_lat, h_ctx, attn_w_in[j], attn_w_out[j], attn_q_scale[j], attn_k_scale[j],
                                       row, col, need_ctx)
        else:
            y_l, y_c = retention_mixer(h_lat, h_ctx, ret_w_in[j], ret_w_out[j], ret_gn_g[j],
                                       ret_log_decay_fwd[j], ret_log_decay_bwd[j], need_ctx)
        x = layer_norm(DEEPNORM_ALPHA * x + gate[:, None, :] * y_l, ln_g[i], ln_b[i])
        if need_ctx:
            ctx = layer_norm(DEEPNORM_ALPHA * ctx + gate_c * y_c, ln_g[i], ln_b[i])
    return x
```

```python
import functools
import math

import jax
import jax.numpy as jnp
from jax import lax
from jax.experimental import pallas as pl
from jax.experimental.pallas import tpu as pltpu

GRID_W = 64
ROPE_THETA = 10000.0

ATT_HEADS = 8
ATT_KV_HEADS = 2
ATT_GROUP = ATT_HEADS // ATT_KV_HEADS
ATT_HEAD_DIM = 128

RET_HEADS = 4
RET_QK_DIM = 256
RET_V_DIM = 512
RET_CHUNK = 256

LN_EPS = 1e-5
QK_EPS = 1e-6
GN_EPS = 1e-5

LANES = 128
VMEM_LIMIT = 56 * 1024 * 1024

_BF16 = jnp.bfloat16
_F32 = jnp.float32


def _params(n_grid):
    return pltpu.CompilerParams(dimension_semantics=("arbitrary",) * n_grid,
                                vmem_limit_bytes=VMEM_LIMIT)


def _silu(g):
    return g * jax.nn.sigmoid(g)


def _mod_kernel(c_ref, w_ref, b_ref, o_ref):
    sc = _silu(c_ref[...])
    o_ref[0] = jnp.dot(sc, w_ref[0], preferred_element_type=_F32) + b_ref[0]


def _modulation(cvec, mod_w, mod_b):
    depth, d, d3 = mod_w.shape
    r = cvec.shape[0]
    tn = 768
    return pl.pallas_call(
        _mod_kernel,
        grid=(depth, d3 // tn),
        in_specs=[pl.BlockSpec((r, d), lambda i, j: (0, 0)),
                  pl.BlockSpec((1, d, tn), lambda i, j: (i, 0, j)),
                  pl.BlockSpec((1, 1, tn), lambda i, j: (i, 0, j))],
        out_specs=pl.BlockSpec((1, r, tn), lambda i, j: (i, 0, j)),
        out_shape=jax.ShapeDtypeStruct((depth, r, d3), _F32),
        compiler_params=_params(2),
        name="modulation",
    )(cvec, mod_w, mod_b.reshape(depth, 1, d3))


def _modulated(x_ref, shift_ref, scale_ref):
    return (x_ref[0] * (1.0 + scale_ref[0]) + shift_ref[0]).astype(_BF16)


def _attn_inproj_kernel(x_ref, shift_ref, scale_ref, w_ref, qs_ref, ks_ref, cos_ref, sin_ref,
                        q_ref, k_ref, v_ref, g_ref, *, q_premul):
    h = _modulated(x_ref, shift_ref, scale_ref)
    dh = ATT_HEAD_DIM
    qw = ATT_HEADS * dh
    kw = ATT_KV_HEADS * dh
    cos = cos_ref[...]
    sin = sin_ref[...]
    lane = lax.broadcasted_iota(jnp.int32, cos.shape, 1)
    first_half = (lane % (dh // 2)) < (dh // 4)

    def norm_rope(t, scale_row, mul):
        ms = jnp.mean(t * t, axis=-1, keepdims=True)
        t = t * lax.rsqrt(ms + QK_EPS) * scale_row
        partner = jnp.where(first_half, pltpu.roll(t, dh - dh // 4, axis=1), pltpu.roll(t, dh // 4, axis=1))
        t = t * cos + partner * sin
        if mul != 1.0:
            t = t * mul
        return t.astype(_BF16)

    for hd in range(ATT_HEADS):
        p = jnp.dot(h, w_ref[:, hd * dh:(hd + 1) * dh], preferred_element_type=_F32)
        q_ref[0, :, hd * dh:(hd + 1) * dh] = norm_rope(p, qs_ref[...], q_premul)
    g_ref[0] = jnp.dot(h, w_ref[:, qw:2 * qw], preferred_element_type=_F32)
    for hd in range(ATT_KV_HEADS):
        p = jnp.dot(h, w_ref[:, 2 * qw + hd * dh:2 * qw + (hd + 1) * dh], preferred_element_type=_F32)
        k_ref[0, :, hd * dh:(hd + 1) * dh] = norm_rope(p, ks_ref[...], 1.0)
    v_ref[0] = jnp.dot(h, w_ref[:, 2 * qw + kw:2 * qw + 2 * kw], preferred_element_type=_F32).astype(_BF16)


def _attn_inproj(x, shift, scale, w, q_scale, k_scale, cos, sin, tm):
    b, n, d = x.shape
    qw = ATT_HEADS * ATT_HEAD_DIM
    kw = ATT_KV_HEADS * ATT_HEAD_DIM
    row = lambda bi, i: (bi, i, 0)
    vec = lambda bi, i: (bi, 0, 0)
    const = lambda bi, i: (0, 0)
    return pl.pallas_call(
        functools.partial(_attn_inproj_kernel, q_premul=ATT_HEAD_DIM ** -0.5),
        grid=(b, n // tm),
        in_specs=[pl.BlockSpec((1, tm, d), row),
                  pl.BlockSpec((1, 1, d), vec),
                  pl.BlockSpec((1, 1, d), vec),
                  pl.BlockSpec(w.shape, const),
                  pl.BlockSpec((1, ATT_HEAD_DIM), const),
                  pl.BlockSpec((1, ATT_HEAD_DIM), const),
                  pl.BlockSpec((tm, ATT_HEAD_DIM), lambda bi, i: (i, 0)),
                  pl.BlockSpec((tm, ATT_HEAD_DIM), lambda bi, i: (i, 0))],
        out_specs=[pl.BlockSpec((1, tm, qw), row),
                   pl.BlockSpec((1, tm, kw), row),
                   pl.BlockSpec((1, tm, kw), row),
                   pl.BlockSpec((1, tm, qw), row)],
        out_shape=[jax.ShapeDtypeStruct((b, n, qw), _BF16),
                   jax.ShapeDtypeStruct((b, n, kw), _BF16),
                   jax.ShapeDtypeStruct((b, n, kw), _BF16),
                   jax.ShapeDtypeStruct((b, n, qw), _F32)],
        compiler_params=_params(2),
        name="attn_inproj",
    )(x, shift, scale, w, q_scale, k_scale, cos, sin)


def _ret_qk_inproj_kernel(x_ref, shift_ref, scale_ref, w_ref, cos_ref, sin_ref, o_ref):
    h = _modulated(x_ref, shift_ref, scale_ref)
    dk = RET_QK_DIM
    half = dk // 2
    cos = cos_ref[...]
    sin = sin_ref[...]
    for hd in range(2 * RET_HEADS):
        mul = 1.0 if hd < RET_HEADS else dk ** -0.5
        x1 = jnp.dot(h, w_ref[:, hd * dk:hd * dk + half], preferred_element_type=_F32)
        x2 = jnp.dot(h, w_ref[:, hd * dk + half:(hd + 1) * dk], preferred_element_type=_F32)
        o1 = x1 * cos - x2 * sin
        o2 = x1 * sin + x2 * cos
        if mul != 1.0:
            o1 = o1 * mul
            o2 = o2 * mul
        o_ref[0, :, hd * dk:hd * dk + half] = o1.astype(_BF16)
        o_ref[0, :, hd * dk + half:(hd + 1) * dk] = o2.astype(_BF16)


def _ret_qk_inproj(x, shift, scale, w, cos, sin, tm):
    b, n, d = x.shape
    wout = w.shape[1]
    row = lambda bi, i: (bi, i, 0)
    vec = lambda bi, i: (bi, 0, 0)
    const = lambda bi, i: (0, 0)
    half = RET_QK_DIM // 2
    return pl.pallas_call(
        _ret_qk_inproj_kernel,
        grid=(b, n // tm),
        in_specs=[pl.BlockSpec((1, tm, d), row),
                  pl.BlockSpec((1, 1, d), vec),
                  pl.BlockSpec((1, 1, d), vec),
                  pl.BlockSpec(w.shape, const),
                  pl.BlockSpec((tm, half), lambda bi, i: (i, 0)),
                  pl.BlockSpec((tm, half), lambda bi, i: (i, 0))],
        out_specs=pl.BlockSpec((1, tm, wout), row),
        out_shape=jax.ShapeDtypeStruct((b, n, wout), _BF16),
        compiler_params=_params(2),
        name="ret_qk_inproj",
    )(x, shift, scale, w, cos, sin)


def _plain_inproj_kernel(x_ref, shift_ref, scale_ref, w_ref, o_ref, *, tn):
    h = _modulated(x_ref, shift_ref, scale_ref)
    for j in range(w_ref.shape[1] // tn):
        o_ref[0, :, j * tn:(j + 1) * tn] = jnp.dot(
            h, w_ref[:, j * tn:(j + 1) * tn], preferred_element_type=_F32).astype(o_ref.dtype)


def _plain_inproj(x, shift, scale, w, out_dtype, tm, name):
    b, n, d = x.shape
    wout = w.shape[1]
    row = lambda bi, i: (bi, i, 0)
    vec = lambda bi, i: (bi, 0, 0)
    return pl.pallas_call(
        functools.partial(_plain_inproj_kernel, tn=512),
        grid=(b, n // tm),
        in_specs=[pl.BlockSpec((1, tm, d), row),
                  pl.BlockSpec((1, 1, d), vec),
                  pl.BlockSpec((1, 1, d), vec),
                  pl.BlockSpec(w.shape, lambda bi, i: (0, 0))],
        out_specs=pl.BlockSpec((1, tm, wout), row),
        out_shape=jax.ShapeDtypeStruct((b, n, wout), out_dtype),
        compiler_params=_params(2),
        name=name,
    )(x, shift, scale, w)


def _flash_kernel(*refs, seg_chunks, tq):
    q_ref, g_ref = refs[0], refs[1]
    kv_refs = refs[2:-1]
    t_ref = refs[-1]
    dh = ATT_HEAD_DIM
    grp = ATT_GROUP
    q = jnp.concatenate([q_ref[0, :, i * dh:(i + 1) * dh] for i in range(grp)], axis=0)
    m = jnp.full((grp * tq, 1), -jnp.inf, _F32)
    l = jnp.zeros((grp * tq, 1), _F32)
    acc = jnp.zeros((grp * tq, dh), _F32)
    for si, tk in enumerate(seg_chunks):
        k_ref, v_ref = kv_refs[2 * si], kv_refs[2 * si + 1]
        for c in range(k_ref.shape[1] // tk):
            k = k_ref[0, c * tk:(c + 1) * tk, :]
            v = v_ref[0, c * tk:(c + 1) * tk, :]
            s = lax.dot_general(q, k, (((1,), (1,)), ((), ())), preferred_element_type=_F32)
            m_new = jnp.maximum(m, jnp.max(s, axis=-1, keepdims=True))
            alpha = jnp.exp(m - m_new)
            p = jnp.exp(s - m_new)
            l = alpha * l + jnp.sum(p, axis=-1, keepdims=True)
            acc = alpha * acc + jnp.dot(p.astype(_BF16), v, preferred_element_type=_F32)
            m = m_new
    o = acc * (1.0 / l)
    for i in range(grp):
        gate = _silu(g_ref[0, :, i * dh:(i + 1) * dh])
        t_ref[0, :, i * dh:(i + 1) * dh] = (o[i * tq:(i + 1) * tq] * gate).astype(_BF16)


def _flash(q, g, kv_segments, tq):
    b, nq, qw = q.shape
    dh = ATT_HEAD_DIM
    gw = ATT_GROUP * dh
    qmap = lambda bi, hi, i: (bi, i, hi)
    kvmap = lambda bi, hi, i: (bi, 0, hi)
    in_specs = [pl.BlockSpec((1, tq, gw), qmap), pl.BlockSpec((1, tq, gw), qmap)]
    args = [q, g]
    for k, v, _ in kv_segments:
        in_specs += [pl.BlockSpec((1, k.shape[1], dh), kvmap), pl.BlockSpec((1, v.shape[1], dh), kvmap)]
        args += [k, v]
    return pl.pallas_call(
        functools.partial(_flash_kernel, seg_chunks=tuple(tk for _, _, tk in kv_segments), tq=tq),
        grid=(b, ATT_KV_HEADS, nq // tq),
        in_specs=in_specs,
        out_specs=pl.BlockSpec((1, tq, gw), qmap),
        out_shape=jax.ShapeDtypeStruct((b, nq, qw), _BF16),
        compiler_params=_params(3),
        name="flash_attention",
    )(*args)


def _retention_kernel(lgf_ref, lgb_ref, qc_ref, kc_ref, vc_ref, ql_ref, kl_ref, vl_ref,
                      oc_ref, ol_ref, state_ref, *, chunk):
    hd = pl.program_id(1)
    c = chunk
    n_ctx = qc_ref.shape[1] // c
    n_lat = ql_ref.shape[1] // c
    ii = lax.broadcasted_iota(jnp.int32, (c, c), 0)
    jj = lax.broadcasted_iota(jnp.int32, (c, c), 1)
    row = lax.broadcasted_iota(jnp.int32, (c, 1), 0).astype(_F32)

    def tables(lg, forward):
        diff = (ii - jj) if forward else (jj - ii)
        keep = (diff >= 0) if forward else (diff > 0)
        decay = jnp.where(keep, jnp.exp(lg * jnp.maximum(diff, 0).astype(_F32)), 0.0)
        if forward:
            xi = jnp.exp(lg * (row + 1.0))
            zeta = jnp.exp(lg * (c - 1.0 - row))
        else:
            xi = jnp.exp(lg * (c - row))
            zeta = jnp.exp(lg * row)
        g_chunk = jnp.exp(jnp.full((1, 1), lg * c, _F32))
        return decay, xi, zeta, g_chunk

    def step(q_ref, k_ref, v_ref, o_ref, start, tabs, accumulate):
        decay, xi, zeta, g_chunk = tabs
        sl = pl.ds(start, c)
        q = q_ref[0, sl, :]
        k = k_ref[0, sl, :]
        v = v_ref[0, sl, :]
        state = state_ref[...]
        s = lax.dot_general(q, k, (((1,), (1,)), ((), ())), preferred_element_type=_F32) * decay
        o = jnp.dot(s.astype(_BF16), v, preferred_element_type=_F32)
        o = o + jnp.dot(q, state.astype(_BF16), preferred_element_type=_F32) * xi
        kz = (k.astype(_F32) * zeta).astype(_BF16)
        upd = lax.dot_general(kz, v, (((0,), (0,)), ((), ())), preferred_element_type=_F32)
        state_ref[...] = state * g_chunk + upd
        if accumulate:
            o_ref[0, sl, :] = o_ref[0, sl, :] + o
        else:
            o_ref[0, sl, :] = o

    def scan(tabs, forward, accumulate):
        state_ref[...] = jnp.zeros_like(state_ref)

        def ctx_pass():
            for i in (range(n_ctx) if forward else reversed(range(n_ctx))):
                step(qc_ref, kc_ref, vc_ref, oc_ref, i * c, tabs, accumulate)

        def lat_pass():
            def body(i, carry):
                idx = i if forward else n_lat - 1 - i
                step(ql_ref, kl_ref, vl_ref, ol_ref, pl.multiple_of(idx * c, c), tabs, accumulate)
                return carry
            lax.fori_loop(0, n_lat, body, 0)

        ctx_pass()
        lat_pass()

    scan(tables(lgf_ref[hd], True), True, False)
    scan(tables(lgb_ref[hd], False), False, True)


def _retention(lg_f, lg_b, qk_c, v_c, qk_l, v_l, chunk):
    b, n_lat, _ = qk_l.shape
    n_ctx = qk_c.shape[1]
    dk, dv, nh = RET_QK_DIM, RET_V_DIM, RET_HEADS
    qmap = lambda bi, hi: (bi, 0, hi)
    kmap = lambda bi, hi: (bi, 0, nh + hi)
    smem = pl.BlockSpec(memory_space=pltpu.SMEM)
    return pl.pallas_call(
        functools.partial(_retention_kernel, chunk=chunk),
        grid=(b, nh),
        in_specs=[smem, smem,
                  pl.BlockSpec((1, n_ctx, dk), qmap), pl.BlockSpec((1, n_ctx, dk), kmap),
                  pl.BlockSpec((1, n_ctx, dv), qmap),
                  pl.BlockSpec((1, n_lat, dk), qmap), pl.BlockSpec((1, n_lat, dk), kmap),
                  pl.BlockSpec((1, n_lat, dv), qmap)],
        out_specs=[pl.BlockSpec((1, n_ctx, dv), qmap), pl.BlockSpec((1, n_lat, dv), qmap)],
        out_shape=[jax.ShapeDtypeStruct((b, n_ctx, nh * dv), _F32),
                   jax.ShapeDtypeStruct((b, n_lat, nh * dv), _F32)],
        scratch_shapes=[pltpu.VMEM((dk, dv), _F32)],
        compiler_params=_params(2),
        name="retention",
    )(lg_f, lg_b, qk_c, qk_c, v_c, qk_l, qk_l, v_l)


def _layer_norm_rows(z, g, b):
    mu = jnp.mean(z, axis=-1, keepdims=True)
    zc = z - mu
    var = jnp.mean(zc * zc, axis=-1, keepdims=True)
    return zc * lax.rsqrt(var + LN_EPS) * g + b


def _attn_outproj_kernel(t_ref, x_ref, w_ref, gate_ref, lng_ref, lnb_ref, o_ref, *, alpha):
    y = jnp.dot(t_ref[0], w_ref[...], preferred_element_type=_F32)
    z = alpha * x_ref[0] + gate_ref[0] * y
    o_ref[0] = _layer_norm_rows(z, lng_ref[...], lnb_ref[...])


def _ret_outproj_kernel(o_ref_in, g_ref, x_ref, w_ref, gn_ref, gate_ref, lng_ref, lnb_ref, o_ref, *, alpha):
    dv = RET_V_DIM
    y = None
    for hd in range(RET_HEADS):
        sl = slice(hd * dv, (hd + 1) * dv)
        oh = o_ref_in[0, :, sl]
        mu = jnp.mean(oh, axis=-1, keepdims=True)
        oc = oh - mu
        var = jnp.mean(oc * oc, axis=-1, keepdims=True)
        on = oc * lax.rsqrt(var + GN_EPS) * gn_ref[:, sl]
        t = (on * _silu(g_ref[0, :, sl])).astype(_BF16)
        part = jnp.dot(t, w_ref[sl, :], preferred_element_type=_F32)
        y = part if y is None else y + part
    z = alpha * x_ref[0] + gate_ref[0] * y
    o_ref[0] = _layer_norm_rows(z, lng_ref[...], lnb_ref[...])


def _attn_outproj(t, x, w, gate, ln_g, ln_b, alpha, tm):
    b, n, d = x.shape
    row = lambda bi, i: (bi, i, 0)
    const = lambda bi, i: (0, 0)
    return pl.pallas_call(
        functools.partial(_attn_outproj_kernel, alpha=alpha),
        grid=(b, n // tm),
        in_specs=[pl.BlockSpec((1, tm, t.shape[2]), row),
                  pl.BlockSpec((1, tm, d), row),
                  pl.BlockSpec(w.shape, const),
                  pl.BlockSpec((1, 1, d), lambda bi, i: (bi, 0, 0)),
                  pl.BlockSpec((1, d), const),
                  pl.BlockSpec((1, d), const)],
        out_specs=pl.BlockSpec((1, tm, d), row),
        out_shape=jax.ShapeDtypeStruct((b, n, d), _F32),
        compiler_params=_params(2),
        name="attn_outproj",
    )(t, x, w, gate, ln_g, ln_b)


def _ret_outproj(o, g, x, w, gn_g, gate, ln_g, ln_b, alpha, tm):
    b, n, d = x.shape
    vw = o.shape[2]
    row = lambda bi, i: (bi, i, 0)
    const = lambda bi, i: (0, 0)
    return pl.pallas_call(
        functools.partial(_ret_outproj_kernel, alpha=alpha),
        grid=(b, n // tm),
        in_specs=[pl.BlockSpec((1, tm, vw), row),
                  pl.BlockSpec((1, tm, vw), row),
                  pl.BlockSpec((1, tm, d), row),
                  pl.BlockSpec(w.shape, const),
                  pl.BlockSpec((1, vw), const),
                  pl.BlockSpec((1, 1, d), lambda bi, i: (bi, 0, 0)),
                  pl.BlockSpec((1, d), const),
                  pl.BlockSpec((1, d), const)],
        out_specs=pl.BlockSpec((1, tm, d), row),
        out_shape=jax.ShapeDtypeStruct((b, n, d), _F32),
        compiler_params=_params(2),
        name="ret_outproj",
    )(o, g, x, w, gn_g, gate, ln_g, ln_b)


def _axial_tables(s):
    dh = ATT_HEAD_DIM
    quarter = dh // 4
    t = jnp.arange(s)
    pos = jnp.stack([t // GRID_W, t % GRID_W], axis=1).astype(_F32)
    freqs = ROPE_THETA ** (-jnp.arange(quarter, dtype=_F32) / quarter)
    ang = pos[:, :, None] * freqs[None, None, :]
    cos = jnp.concatenate([jnp.cos(ang), jnp.cos(ang)], axis=-1).reshape(s, dh)
    sin = jnp.concatenate([-jnp.sin(ang), jnp.sin(ang)], axis=-1).reshape(s, dh)
    return cos, sin


def _rope_tables(pos):
    half = RET_QK_DIM // 2
    freqs = ROPE_THETA ** (-jnp.arange(half, dtype=_F32) / half)
    ang = pos.astype(_F32)[:, None] * freqs[None, :]
    return jnp.cos(ang), jnp.sin(ang)


def _row_tile(n, want):
    return want if n % want == 0 else n


def kernel(x, c, ctx, c_ctx, mod_w, mod_b, ln_g, ln_b, attn_w_in, attn_w_out, attn_q_scale, attn_k_scale,
           ret_w_in, ret_w_out, ret_gn_g, ret_log_decay_fwd, ret_log_decay_bwd):
    b, s, d = x.shape
    l = ctx.shape[1]
    depth = mod_w.shape[0]
    alpha = (2.0 * depth) ** 0.25

    rows = 8 * ((b + 1 + 7) // 8)
    cvec = jnp.zeros((rows, d), _F32).at[:b].set(c).at[b].set(c_ctx)
    mods = _modulation(cvec, mod_w, mod_b)

    cos_ax, sin_ax = _axial_tables(s)
    cos_id, sin_id = jnp.ones((l, ATT_HEAD_DIM), _F32), jnp.zeros((l, ATT_HEAD_DIM), _F32)
    cos_c, sin_c = _rope_tables(jnp.arange(l))
    cos_l, sin_l = _rope_tables(l + jnp.arange(s))

    tm_l = _row_tile(s, 512)
    tm_c = _row_tile(l, 256)
    tq = _row_tile(s, 128)
    tk = _row_tile(s, 1024)

    for i in range(depth):
        need_ctx = i < depth - 1
        j = i // 2
        shift_l = mods[i, :b, None, :d]
        scale_l = mods[i, :b, None, d:2 * d]
        gate_l = mods[i, :b, None, 2 * d:]
        shift_c = jnp.broadcast_to(mods[i, b, None, None, :d], (b, 1, d))
        scale_c = jnp.broadcast_to(mods[i, b, None, None, d:2 * d], (b, 1, d))
        gate_c = jnp.broadcast_to(mods[i, b, None, None, 2 * d:], (b, 1, d))
        lng = ln_g[i][None, :]
        lnb = ln_b[i][None, :]
        if i % 2 == 0:
            w_in = attn_w_in[j].astype(_BF16)
            w_out = attn_w_out[j].astype(_BF16)
            qs = attn_q_scale[j][None, :]
            ks = attn_k_scale[j][None, :]
            q_l, k_l, v_l, g_l = _attn_inproj(x, shift_l, scale_l, w_in, qs, ks, cos_ax, sin_ax, tm_l)
            q_c, k_c, v_c, g_c = _attn_inproj(ctx, shift_c, scale_c, w_in, qs, ks, cos_id, sin_id, tm_c)
            t_l = _flash(q_l, g_l, [(k_l, v_l, tk), (k_c, v_c, l)], tq)
            x_new = _attn_outproj(t_l, x, w_out, gate_l, lng, lnb, alpha, tm_l)
            if need_ctx:
                t_c = _flash(q_c, g_c, [(k_c, v_c, l)], _row_tile(l, 128))
                ctx = _attn_outproj(t_c, ctx, w_out, gate_c, lng, lnb, alpha, tm_c)
            x = x_new
        else:
            w_in = ret_w_in[j].astype(_BF16)
            w_out = ret_w_out[j].astype(_BF16)
            qkw = 2 * RET_HEADS * RET_QK_DIM
            vw = RET_HEADS * RET_V_DIM
            w_qk, w_v, w_g = w_in[:, :qkw], w_in[:, qkw:qkw + vw], w_in[:, qkw + vw:]
            qk_l = _ret_qk_inproj(x, shift_l, scale_l, w_qk, cos_l, sin_l, tm_l)
            v_l = _plain_inproj(x, shift_l, scale_l, w_v, _BF16, tm_l, "ret_v_inproj")
            g_l = _plain_inproj(x, shift_l, scale_l, w_g, _F32, tm_l, "ret_g_inproj")
            qk_c = _ret_qk_inproj(ctx, shift_c, scale_c, w_qk, cos_c, sin_c, tm_c)
            v_c = _plain_inproj(ctx, shift_c, scale_c, w_v, _BF16, tm_c, "ret_v_inproj")
            chunk = RET_CHUNK if (l % RET_CHUNK == 0 and s % RET_CHUNK == 0) else 128
            o_c, o_l = _retention(ret_log_decay_fwd[j], ret_log_decay_bwd[j], qk_c, v_c, qk_l, v_l, chunk)
            gn = ret_gn_g[j][None, :]
            x_new = _ret_outproj(o_l, g_l, x, w_out, gn, gate_l, lng, lnb, alpha, tm_l)
            if need_ctx:
                g_c = _plain_inproj(ctx, shift_c, scale_c, w_g, _F32, tm_c, "ret_g_inproj")
                ctx = _ret_outproj(o_c, g_c, ctx, w_out, gn, gate_c, lng, lnb, alpha, tm_c)
            x = x_new
    return x
```

```python
import functools
import math

import jax
import jax.numpy as jnp
from jax import lax
from jax.experimental import pallas as pl
from jax.experimental.pallas import tpu as pltpu

GRID_W = 64
ROPE_THETA = 10000.0

ATT_HEADS = 8
ATT_KV_HEADS = 2
ATT_GROUP = ATT_HEADS // ATT_KV_HEADS
ATT_HEAD_DIM = 128
ONES_ROWS = 16

RET_HEADS = 4
RET_QK_DIM = 256
RET_V_DIM = 512
RET_CHUNK = 256

LN_EPS = 1e-5
QK_EPS = 1e-6
GN_EPS = 1e-5

MXU_COLS = 256
DOT_COLS = 2 * MXU_COLS
OUT_SUB_ROWS = 256
VMEM_LIMIT = 56 * 1024 * 1024

_BF16 = jnp.bfloat16
_F32 = jnp.float32


def _params(n_grid):
    return pltpu.CompilerParams(dimension_semantics=("arbitrary",) * n_grid,
                                vmem_limit_bytes=VMEM_LIMIT)


def _silu(g):
    return g * jax.nn.sigmoid(g)


def _mod_kernel(c_ref, w_ref, b_ref, o_ref):
    sc = _silu(c_ref[...])
    o_ref[0] = jnp.dot(sc, w_ref[0], preferred_element_type=_F32) + b_ref[0]


def _modulation(cvec, mod_w, mod_b):
    depth, d, d3 = mod_w.shape
    r = cvec.shape[0]
    tn = 768
    return pl.pallas_call(
        _mod_kernel,
        grid=(depth, d3 // tn),
        in_specs=[pl.BlockSpec((r, d), lambda i, j: (0, 0)),
                  pl.BlockSpec((1, d, tn), lambda i, j: (i, 0, j)),
                  pl.BlockSpec((1, 1, tn), lambda i, j: (i, 0, j))],
        out_specs=pl.BlockSpec((1, r, tn), lambda i, j: (i, 0, j)),
        out_shape=jax.ShapeDtypeStruct((depth, r, d3), _F32),
        compiler_params=_params(2),
        name="modulation",
    )(cvec, mod_w, mod_b.reshape(depth, 1, d3))


def _modulated(x_ref, shift_ref, scale_ref):
    return (x_ref[0] * (1.0 + scale_ref[0]) + shift_ref[0]).astype(_BF16)


def _slab_dot(h, w_ref, start):
    return jnp.dot(h, w_ref[:, start:start + MXU_COLS], preferred_element_type=_F32)


def _attn_inproj_kernel(x_ref, shift_ref, scale_ref, w_ref, qs_ref, ks_ref, cos_ref, sin_ref,
                        q_ref, k_ref, v_ref, sg_ref, *, q_premul):
    h = _modulated(x_ref, shift_ref, scale_ref)
    dh = ATT_HEAD_DIM
    qw = ATT_HEADS * dh
    kw = ATT_KV_HEADS * dh
    cos = cos_ref[...]
    sin = sin_ref[...]

    def norm_rope(t, scale_row, mul):
        ms = jnp.mean(t * t, axis=-1, keepdims=True)
        t = t * lax.rsqrt(ms + QK_EPS) * scale_row
        t = t * cos + pltpu.roll(t, dh // 2, axis=1) * sin
        if mul != 1.0:
            t = t * mul
        return t.astype(_BF16)

    def heads(o_ref, scale_ref_, mul, first_head, p):
        for u in range(p.shape[1] // dh):
            hd = first_head + u
            o_ref[0, :, hd * dh:(hd + 1) * dh] = norm_rope(p[:, u * dh:(u + 1) * dh], scale_ref_[...], mul)

    def q_out(j):
        return lambda p: heads(q_ref, qs_ref, q_premul, j * (DOT_COLS // dh), p)

    def gate_out(j):
        def fn(p):
            sg_ref[0, :, j * DOT_COLS:(j + 1) * DOT_COLS] = _silu(p).astype(_BF16)
        return fn

    def kv_out(p):
        heads(k_ref, ks_ref, 1.0, 0, p[:, :kw])
        v_ref[0] = p[:, kw:].astype(_BF16)

    work = []
    for j in range(qw // DOT_COLS):
        work.append((j * DOT_COLS, DOT_COLS, q_out(j)))
        work.append((qw + j * DOT_COLS, DOT_COLS, gate_out(j)))
    work.append((2 * qw, 2 * kw, kv_out))
    dot = lambda start, width: jnp.dot(h, w_ref[:, start:start + width], preferred_element_type=_F32)
    p_next = dot(*work[0][:2])
    for n, (_, _, finish) in enumerate(work):
        p = p_next
        if n + 1 < len(work):
            p_next = dot(*work[n + 1][:2])
        finish(p)


def _attn_inproj(x, shift, scale, w, q_scale, k_scale, cos, sin, tm):
    b, n, d = x.shape
    qw = ATT_HEADS * ATT_HEAD_DIM
    kw = ATT_KV_HEADS * ATT_HEAD_DIM
    row = lambda bi, i: (bi, i, 0)
    vec = lambda bi, i: (bi, 0, 0)
    const = lambda bi, i: (0, 0)
    return pl.pallas_call(
        functools.partial(_attn_inproj_kernel, q_premul=ATT_HEAD_DIM ** -0.5 * math.log2(math.e)),
        grid=(b, n // tm),
        in_specs=[pl.BlockSpec((1, tm, d), row),
                  pl.BlockSpec((1, 1, d), vec),
                  pl.BlockSpec((1, 1, d), vec),
                  pl.BlockSpec(w.shape, const),
                  pl.BlockSpec((1, ATT_HEAD_DIM), const),
                  pl.BlockSpec((1, ATT_HEAD_DIM), const),
                  pl.BlockSpec((tm, ATT_HEAD_DIM), lambda bi, i: (i, 0)),
                  pl.BlockSpec((tm, ATT_HEAD_DIM), lambda bi, i: (i, 0))],
        out_specs=[pl.BlockSpec((1, tm, qw), row),
                   pl.BlockSpec((1, tm, kw), row),
                   pl.BlockSpec((1, tm, kw), row),
                   pl.BlockSpec((1, tm, qw), row)],
        out_shape=[jax.ShapeDtypeStruct((b, n, qw), _BF16),
                   jax.ShapeDtypeStruct((b, n, kw), _BF16),
                   jax.ShapeDtypeStruct((b, n, kw), _BF16),
                   jax.ShapeDtypeStruct((b, n, qw), _BF16)],
        compiler_params=_params(2),
        name="attn_inproj",
    )(x, shift, scale, w, q_scale, k_scale, cos, sin)


def _ret_inproj_kernel(x_ref, shift_ref, scale_ref, w_ref, cos_ref, sin_ref, qk_ref, v_ref, sg_ref):
    h = _modulated(x_ref, shift_ref, scale_ref)
    dk = RET_QK_DIM
    half = dk // 2
    qkw = 2 * RET_HEADS * dk
    vw = RET_HEADS * RET_V_DIM
    cos = cos_ref[...]
    sin = sin_ref[...]
    for hd in range(2 * RET_HEADS):
        p = jnp.dot(h, w_ref[:, hd * dk:(hd + 1) * dk], preferred_element_type=_F32)
        x1, x2 = p[:, :half], p[:, half:]
        o1 = x1 * cos - x2 * sin
        o2 = x1 * sin + x2 * cos
        if hd >= RET_HEADS:
            o1 = o1 * dk ** -0.5
            o2 = o2 * dk ** -0.5
        qk_ref[0, :, hd * dk:hd * dk + half] = o1.astype(_BF16)
        qk_ref[0, :, hd * dk + half:(hd + 1) * dk] = o2.astype(_BF16)
    for j in range(vw // MXU_COLS):
        v_ref[0, :, j * MXU_COLS:(j + 1) * MXU_COLS] = _slab_dot(h, w_ref, qkw + j * MXU_COLS).astype(_BF16)
    for j in range(vw // MXU_COLS):
        g = _slab_dot(h, w_ref, qkw + vw + j * MXU_COLS)
        sg_ref[0, :, j * MXU_COLS:(j + 1) * MXU_COLS] = _silu(g).astype(_BF16)


def _ret_inproj(x, shift, scale, w, cos, sin, tm):
    b, n, d = x.shape
    qkw = 2 * RET_HEADS * RET_QK_DIM
    vw = RET_HEADS * RET_V_DIM
    row = lambda bi, i: (bi, i, 0)
    vec = lambda bi, i: (bi, 0, 0)
    half = RET_QK_DIM // 2
    return pl.pallas_call(
        _ret_inproj_kernel,
        grid=(b, n // tm),
        in_specs=[pl.BlockSpec((1, tm, d), row),
                  pl.BlockSpec((1, 1, d), vec),
                  pl.BlockSpec((1, 1, d), vec),
                  pl.BlockSpec(w.shape, lambda bi, i: (0, 0), pipeline_mode=pl.Buffered(1)),
                  pl.BlockSpec((tm, half), lambda bi, i: (i, 0)),
                  pl.BlockSpec((tm, half), lambda bi, i: (i, 0))],
        out_specs=[pl.BlockSpec((1, tm, qkw), row),
                   pl.BlockSpec((1, tm, vw), row),
                   pl.BlockSpec((1, tm, vw), row)],
        out_shape=[jax.ShapeDtypeStruct((b, n, qkw), _BF16),
                   jax.ShapeDtypeStruct((b, n, vw), _BF16),
                   jax.ShapeDtypeStruct((b, n, vw), _BF16)],
        compiler_params=_params(2),
        name="ret_inproj",
    )(x, shift, scale, w, cos, sin)


def _flash_kernel(*refs, seg_chunks, tq):
    q_ref, sg_ref = refs[0], refs[1]
    kv_refs = refs[2:-1]
    t_ref = refs[-1]
    dh = ATT_HEAD_DIM
    grp = ATT_GROUP
    chunks = []
    for si, tk in enumerate(seg_chunks):
        k_ref, vt_ref = kv_refs[2 * si], kv_refs[2 * si + 1]
        chunks += [(k_ref, vt_ref, c * tk, tk) for c in range(k_ref.shape[1] // tk)]
    qs = [q_ref[0, :, i * dh:(i + 1) * dh] for i in range(grp)]

    def scores(ci, i):
        k_ref, _, start, tk = chunks[ci]
        return lax.dot_general(k_ref[0, start:start + tk, :], qs[i], (((1,), (1,)), ((), ())),
                               preferred_element_type=_F32)

    m = [jnp.full((1, tq), -jnp.inf, _F32) for _ in range(grp)]
    acc = [jnp.zeros((dh + ONES_ROWS, tq), _F32) for _ in range(grp)]
    s_next = [scores(0, i) for i in range(grp)]
    for ci, (_, vt_ref, start, tk) in enumerate(chunks):
        for i in range(grp):
            s = s_next[i]
            if ci + 1 < len(chunks):
                s_next[i] = scores(ci + 1, i)
            m_new = jnp.maximum(m[i], jnp.max(s, axis=0, keepdims=True))
            alpha = jnp.exp2(m[i] - m_new)
            p = jnp.exp2(s - m_new).astype(_BF16)
            acc[i] = alpha * acc[i] + jnp.dot(vt_ref[0, 0, :, start:start + tk], p, preferred_element_type=_F32)
            m[i] = m_new
    for i in range(grp):
        o_t = acc[i][:dh] * (1.0 / acc[i][dh:dh + 1])
        gate = sg_ref[0, :, i * dh:(i + 1) * dh].astype(_F32)
        t_ref[0, :, i * dh:(i + 1) * dh] = (o_t.T * gate).astype(_BF16)


def _flash(q, sg, kv_segments, tq):
    b, nq, qw = q.shape
    dh = ATT_HEAD_DIM
    gw = ATT_GROUP * dh
    qmap = lambda bi, hi, i: (bi, i, hi)
    kmap = lambda bi, hi, i: (bi, 0, hi)
    vmap = lambda bi, hi, i: (bi, hi, 0, 0)
    in_specs = [pl.BlockSpec((1, tq, gw), qmap), pl.BlockSpec((1, tq, gw), qmap)]
    args = [q, sg]
    for k, vt, _ in kv_segments:
        in_specs += [pl.BlockSpec((1, k.shape[1], dh), kmap),
                     pl.BlockSpec((1, 1, dh + ONES_ROWS, vt.shape[3]), vmap)]
        args += [k, vt]
    return pl.pallas_call(
        functools.partial(_flash_kernel, seg_chunks=tuple(tk for _, _, tk in kv_segments), tq=tq),
        grid=(b, ATT_KV_HEADS, nq // tq),
        in_specs=in_specs,
        out_specs=pl.BlockSpec((1, tq, gw), qmap),
        out_shape=jax.ShapeDtypeStruct((b, nq, qw), _BF16),
        compiler_params=_params(3),
        name="flash_attention",
    )(*args)


def _v_transposed(v):
    b, n, _ = v.shape
    vt = v.reshape(b, n, ATT_KV_HEADS, ATT_HEAD_DIM).transpose(0, 2, 3, 1)
    return jnp.concatenate([vt, jnp.ones((b, ATT_KV_HEADS, ONES_ROWS, n), v.dtype)], axis=2)


def _retention_kernel(lgf_ref, lgb_ref, gn_ref, qc_ref, kc_ref, vc_ref, sgc_ref, ql_ref, kl_ref, vl_ref, sgl_ref,
                      tc_ref, tl_ref, state_ref, fc_ref, fl_ref, *, chunk, unroll):
    hd = pl.program_id(1)
    c = chunk
    n_ctx = qc_ref.shape[1] // c
    n_lat = ql_ref.shape[1] // c
    ii = lax.broadcasted_iota(jnp.int32, (c, c), 0)
    jj = lax.broadcasted_iota(jnp.int32, (c, c), 1)
    row = lax.broadcasted_iota(jnp.int32, (c, 1), 0).astype(_F32)

    def tables(lg, forward):
        diff = (ii - jj) if forward else (jj - ii)
        keep = (diff >= 0) if forward else (diff > 0)
        decay = jnp.where(keep, jnp.exp(lg * jnp.maximum(diff, 0).astype(_F32)), 0.0)
        if forward:
            xi = jnp.exp(lg * (row + 1.0))
            zeta = jnp.exp(lg * (c - 1.0 - row))
        else:
            xi = jnp.exp(lg * (c - row))
            zeta = jnp.exp(lg * row)
        g_chunk = jnp.exp(jnp.full((1, 1), lg * c, _F32))
        return decay, xi, zeta, g_chunk

    def step(q_ref, k_ref, v_ref, sg_ref, f_ref, t_ref, start, tabs, forward):
        decay, xi, zeta, g_chunk = tabs
        sl = pl.ds(start, c)
        q = q_ref[0, sl, :]
        k = k_ref[0, sl, :]
        v = v_ref[0, sl, :]
        state = state_ref[...]
        s = lax.dot_general(q, k, (((1,), (1,)), ((), ())), preferred_element_type=_F32) * decay
        o = jnp.dot(s.astype(_BF16), v, preferred_element_type=_F32)
        o = o + jnp.dot(q, state.astype(_BF16), preferred_element_type=_F32) * xi
        kz = (k.astype(_F32) * zeta).astype(_BF16)
        upd = lax.dot_general(kz, v, (((0,), (0,)), ((), ())), preferred_element_type=_F32)
        state_ref[...] = state * g_chunk + upd
        if forward:
            f_ref[sl, :] = o
        else:
            o = o + f_ref[sl, :]
            mu = jnp.mean(o, axis=-1, keepdims=True)
            oc = o - mu
            var = jnp.mean(oc * oc, axis=-1, keepdims=True)
            on = oc * lax.rsqrt(var + GN_EPS) * gn_ref[...]
            t_ref[0, sl, :] = (on * sg_ref[0, sl, :].astype(_F32)).astype(_BF16)

    def scan(tabs, forward):
        state_ref[...] = jnp.zeros_like(state_ref)
        for i in (range(n_ctx) if forward else reversed(range(n_ctx))):
            step(qc_ref, kc_ref, vc_ref, sgc_ref, fc_ref, tc_ref, i * c, tabs, forward)

        def body(i, carry):
            idx = i if forward else n_lat - 1 - i
            step(ql_ref, kl_ref, vl_ref, sgl_ref, fl_ref, tl_ref, pl.multiple_of(idx * c, c), tabs, forward)
            return carry
        lax.fori_loop(0, n_lat, body, 0, unroll=unroll)

    scan(tables(lgf_ref[hd], True), True)
    scan(tables(lgb_ref[hd], False), False)


def _retention(lg_f, lg_b, gn_g, qk_c, v_c, sg_c, qk_l, v_l, sg_l, chunk):
    b, n_lat, _ = qk_l.shape
    n_ctx = qk_c.shape[1]
    dk, dv, nh = RET_QK_DIM, RET_V_DIM, RET_HEADS
    qmap = lambda bi, hi: (bi, 0, hi)
    kmap = lambda bi, hi: (bi, 0, nh + hi)
    smem = pl.BlockSpec(memory_space=pltpu.SMEM)
    unroll = math.gcd(n_lat // chunk, 8)
    return pl.pallas_call(
        functools.partial(_retention_kernel, chunk=chunk, unroll=unroll),
        grid=(b, nh),
        in_specs=[smem, smem, pl.BlockSpec((1, dv), lambda bi, hi: (0, hi)),
                  pl.BlockSpec((1, n_ctx, dk), qmap), pl.BlockSpec((1, n_ctx, dk), kmap),
                  pl.BlockSpec((1, n_ctx, dv), qmap), pl.BlockSpec((1, n_ctx, dv), qmap),
                  pl.BlockSpec((1, n_lat, dk), qmap), pl.BlockSpec((1, n_lat, dk), kmap),
                  pl.BlockSpec((1, n_lat, dv), qmap), pl.BlockSpec((1, n_lat, dv), qmap)],
        out_specs=[pl.BlockSpec((1, n_ctx, dv), qmap), pl.BlockSpec((1, n_lat, dv), qmap)],
        out_shape=[jax.ShapeDtypeStruct((b, n_ctx, nh * dv), _BF16),
                   jax.ShapeDtypeStruct((b, n_lat, nh * dv), _BF16)],
        scratch_shapes=[pltpu.VMEM((dk, dv), _F32),
                        pltpu.VMEM((n_ctx, dv), _F32),
                        pltpu.VMEM((n_lat, dv), _F32)],
        compiler_params=_params(2),
        name="retention",
    )(lg_f, lg_b, gn_g, qk_c, qk_c, v_c, sg_c, qk_l, qk_l, v_l, sg_l)


def _outproj_kernel(t_ref, x_ref, w_ref, gate_ref, lng_ref, lnb_ref, o_ref, *, alpha, sub):
    n_sub = t_ref.shape[1] // sub
    proj = lambda r: jnp.dot(t_ref[0, r * sub:(r + 1) * sub, :], w_ref[...], preferred_element_type=_F32)
    y_next = proj(0)
    for r in range(n_sub):
        y = y_next
        if r + 1 < n_sub:
            y_next = proj(r + 1)
        rows = slice(r * sub, (r + 1) * sub)
        z = alpha * x_ref[0, rows, :] + gate_ref[0] * y
        mu = jnp.mean(z, axis=-1, keepdims=True)
        zc = z - mu
        var = jnp.mean(zc * zc, axis=-1, keepdims=True)
        o_ref[0, rows, :] = zc * lax.rsqrt(var + LN_EPS) * lng_ref[...] + lnb_ref[...]


def _outproj(t, x, w, gate, ln_g, ln_b, alpha, tm):
    b, n, d = x.shape
    row = lambda bi, i: (bi, i, 0)
    const = lambda bi, i: (0, 0)
    return pl.pallas_call(
        functools.partial(_outproj_kernel, alpha=alpha, sub=math.gcd(tm, OUT_SUB_ROWS)),
        grid=(b, n // tm),
        in_specs=[pl.BlockSpec((1, tm, t.shape[2]), row),
                  pl.BlockSpec((1, tm, d), row),
                  pl.BlockSpec(w.shape, const),
                  pl.BlockSpec((1, 1, d), lambda bi, i: (bi, 0, 0)),
                  pl.BlockSpec((1, d), const),
                  pl.BlockSpec((1, d), const)],
        out_specs=pl.BlockSpec((1, tm, d), row),
        out_shape=jax.ShapeDtypeStruct((b, n, d), _F32),
        compiler_params=_params(2),
        name="outproj",
    )(t, x, w, gate, ln_g, ln_b)


def _axial_perm():
    quarter = ATT_HEAD_DIM // 4
    return jnp.concatenate([jnp.arange(quarter) + off * quarter for off in (0, 2, 1, 3)])


def _axial_tables(s):
    quarter = ATT_HEAD_DIM // 4
    t = jnp.arange(s)
    pos = jnp.stack([t // GRID_W, t % GRID_W], axis=1).astype(_F32)
    freqs = ROPE_THETA ** (-jnp.arange(quarter, dtype=_F32) / quarter)
    ang = (pos[:, :, None] * freqs[None, None, :]).reshape(s, 2 * quarter)
    cos = jnp.concatenate([jnp.cos(ang), jnp.cos(ang)], axis=-1)
    sin = jnp.concatenate([-jnp.sin(ang), jnp.sin(ang)], axis=-1)
    return cos, sin


def _permute_qk_columns(w_in):
    dh = ATT_HEAD_DIM
    qw = ATT_HEADS * dh
    kw = ATT_KV_HEADS * dh
    perm = _axial_perm()
    q_cols = (jnp.arange(ATT_HEADS)[:, None] * dh + perm[None, :]).reshape(-1)
    k_cols = 2 * qw + (jnp.arange(ATT_KV_HEADS)[:, None] * dh + perm[None, :]).reshape(-1)
    cols = jnp.concatenate([q_cols, jnp.arange(qw, 2 * qw), k_cols, jnp.arange(2 * qw + kw, 2 * qw + 2 * kw)])
    return w_in[:, cols]


def _rope_tables(pos):
    half = RET_QK_DIM // 2
    freqs = ROPE_THETA ** (-jnp.arange(half, dtype=_F32) / half)
    ang = pos.astype(_F32)[:, None] * freqs[None, :]
    return jnp.cos(ang), jnp.sin(ang)


def _row_tile(n, want):
    return want if n % want == 0 else n


def kernel(x, c, ctx, c_ctx, mod_w, mod_b, ln_g, ln_b, attn_w_in, attn_w_out, attn_q_scale, attn_k_scale,
           ret_w_in, ret_w_out, ret_gn_g, ret_log_decay_fwd, ret_log_decay_bwd):
    b, s, d = x.shape
    l = ctx.shape[1]
    depth = mod_w.shape[0]
    alpha = (2.0 * depth) ** 0.25

    rows = 8 * ((b + 1 + 7) // 8)
    cvec = jnp.zeros((rows, d), _F32).at[:b].set(c).at[b].set(c_ctx)
    mods = _modulation(cvec, mod_w, mod_b)

    cos_ax, sin_ax = _axial_tables(s)
    cos_id, sin_id = jnp.ones((l, ATT_HEAD_DIM), _F32), jnp.zeros((l, ATT_HEAD_DIM), _F32)
    cos_c, sin_c = _rope_tables(jnp.arange(l))
    cos_l, sin_l = _rope_tables(l + jnp.arange(s))

    tm_l = _row_tile(s, 512)
    tm_c = _row_tile(l, 256)
    tm_out = _row_tile(s, 1024)
    tq = _row_tile(s, 256)
    tk = _row_tile(s, 512)
    chunk = RET_CHUNK if (l % RET_CHUNK == 0 and s % RET_CHUNK == 0) else 128

    for i in range(depth):
        need_ctx = i < depth - 1
        j = i // 2
        shift_l = mods[i, :b, None, :d]
        scale_l = mods[i, :b, None, d:2 * d]
        gate_l = mods[i, :b, None, 2 * d:]
        shift_c = jnp.broadcast_to(mods[i, b, None, None, :d], (b, 1, d))
        scale_c = jnp.broadcast_to(mods[i, b, None, None, d:2 * d], (b, 1, d))
        gate_c = jnp.broadcast_to(mods[i, b, None, None, 2 * d:], (b, 1, d))
        lng = ln_g[i][None, :]
        lnb = ln_b[i][None, :]
        if i % 2 == 0:
            w_in = _permute_qk_columns(attn_w_in[j]).astype(_BF16)
            w_out = attn_w_out[j].astype(_BF16)
            qs = attn_q_scale[j][_axial_perm()][None, :]
            ks = attn_k_scale[j][_axial_perm()][None, :]
            q_l, k_l, v_l, sg_l = _attn_inproj(x, shift_l, scale_l, w_in, qs, ks, cos_ax, sin_ax, tm_l)
            q_c, k_c, v_c, sg_c = _attn_inproj(ctx, shift_c, scale_c, w_in, qs, ks, cos_id, sin_id, tm_c)
            vt_l, vt_c = _v_transposed(v_l), _v_transposed(v_c)
            t_l = _flash(q_l, sg_l, [(k_l, vt_l, tk), (k_c, vt_c, l)], tq)
            if need_ctx:
                t_c = _flash(q_c, sg_c, [(k_c, vt_c, l)], _row_tile(l, 128))
        else:
            w_in = ret_w_in[j].astype(_BF16)
            w_out = ret_w_out[j].astype(_BF16)
            qk_l, v_l, sg_l = _ret_inproj(x, shift_l, scale_l, w_in, cos_l, sin_l, tm_l)
            qk_c, v_c, sg_c = _ret_inproj(ctx, shift_c, scale_c, w_in, cos_c, sin_c, tm_c)
            t_c, t_l = _retention(ret_log_decay_fwd[j], ret_log_decay_bwd[j], ret_gn_g[j][None, :],
                                  qk_c, v_c, sg_c, qk_l, v_l, sg_l, chunk)
        x = _outproj(t_l, x, w_out, gate_l, lng, lnb, alpha, tm_out)
        if need_ctx:
            ctx = _outproj(t_c, ctx, w_out, gate_c, lng, lnb, alpha, tm_c)
    return x
```

```python
import functools
import math

import jax
import jax.numpy as jnp
from jax import lax
from jax.experimental import pallas as pl
from jax.experimental.pallas import tpu as pltpu

GRID_W = 64
ROPE_THETA = 10000.0

ATT_HEADS = 8
ATT_KV_HEADS = 2
ATT_GROUP = ATT_HEADS // ATT_KV_HEADS
ATT_HEAD_DIM = 128
ONES_ROWS = 16

RET_HEADS = 4
RET_QK_DIM = 256
RET_V_DIM = 512
RET_CHUNK = 256

LN_EPS = 1e-5
QK_EPS = 1e-6
GN_EPS = 1e-5

MXU_COLS = 256
DOT_COLS = 2 * MXU_COLS
OUT_SUB_ROWS = 256
VMEM_LIMIT = 56 * 1024 * 1024

_BF16 = jnp.bfloat16
_F32 = jnp.float32


def _params(n_grid):
    return pltpu.CompilerParams(dimension_semantics=("arbitrary",) * n_grid,
                                vmem_limit_bytes=VMEM_LIMIT)


def _silu(g):
    return g * jax.nn.sigmoid(g)


def _mod_kernel(c_ref, w_ref, b_ref, o_ref):
    sc = _silu(c_ref[...])
    o_ref[0] = jnp.dot(sc, w_ref[0], preferred_element_type=_F32) + b_ref[0]


def _modulation(cvec, mod_w, mod_b):
    depth, d, d3 = mod_w.shape
    r = cvec.shape[0]
    tn = 768
    return pl.pallas_call(
        _mod_kernel,
        grid=(depth, d3 // tn),
        in_specs=[pl.BlockSpec((r, d), lambda i, j: (0, 0)),
                  pl.BlockSpec((1, d, tn), lambda i, j: (i, 0, j)),
                  pl.BlockSpec((1, 1, tn), lambda i, j: (i, 0, j))],
        out_specs=pl.BlockSpec((1, r, tn), lambda i, j: (i, 0, j)),
        out_shape=jax.ShapeDtypeStruct((depth, r, d3), _F32),
        compiler_params=_params(2),
        name="modulation",
    )(cvec, mod_w, mod_b.reshape(depth, 1, d3))


def _modulated(x_ref, shift_ref, scale_ref):
    return (x_ref[0] * (1.0 + scale_ref[0]) + shift_ref[0]).astype(_BF16)


def _slab_dot(h, w_ref, start):
    return jnp.dot(h, w_ref[:, start:start + MXU_COLS], preferred_element_type=_F32)


def _attn_inproj_kernel(x_ref, shift_ref, scale_ref, w_ref, qs_ref, ks_ref, cos_ref, sin_ref,
                        q_ref, k_ref, v_ref, sg_ref, *, q_premul):
    h = _modulated(x_ref, shift_ref, scale_ref)
    dh = ATT_HEAD_DIM
    qw = ATT_HEADS * dh
    kw = ATT_KV_HEADS * dh
    cos = cos_ref[...]
    sin = sin_ref[...]

    def norm_rope(t, scale_row, mul):
        ms = jnp.mean(t * t, axis=-1, keepdims=True)
        t = t * lax.rsqrt(ms + QK_EPS) * scale_row
        t = t * cos + pltpu.roll(t, dh // 2, axis=1) * sin
        if mul != 1.0:
            t = t * mul
        return t.astype(_BF16)

    def heads(o_ref, scale_ref_, mul, first_head, p):
        for u in range(p.shape[1] // dh):
            hd = first_head + u
            o_ref[0, :, hd * dh:(hd + 1) * dh] = norm_rope(p[:, u * dh:(u + 1) * dh], scale_ref_[...], mul)

    def q_out(j):
        return lambda p: heads(q_ref, qs_ref, q_premul, j * (DOT_COLS // dh), p)

    def gate_out(j):
        def fn(p):
            sg_ref[0, :, j * DOT_COLS:(j + 1) * DOT_COLS] = _silu(p).astype(_BF16)
        return fn

    def kv_out(p):
        heads(k_ref, ks_ref, 1.0, 0, p[:, :kw])
        v_ref[0] = p[:, kw:].astype(_BF16)

    work = []
    for j in range(qw // DOT_COLS):
        work.append((j * DOT_COLS, DOT_COLS, q_out(j)))
        work.append((qw + j * DOT_COLS, DOT_COLS, gate_out(j)))
    work.append((2 * qw, 2 * kw, kv_out))
    dot = lambda start, width: jnp.dot(h, w_ref[:, start:start + width], preferred_element_type=_F32)
    p_next = dot(*work[0][:2])
    for n, (_, _, finish) in enumerate(work):
        p = p_next
        if n + 1 < len(work):
            p_next = dot(*work[n + 1][:2])
        finish(p)


def _attn_inproj(x, shift, scale, w, q_scale, k_scale, cos, sin, tm):
    b, n, d = x.shape
    qw = ATT_HEADS * ATT_HEAD_DIM
    kw = ATT_KV_HEADS * ATT_HEAD_DIM
    row = lambda bi, i: (bi, i, 0)
    vec = lambda bi, i: (bi, 0, 0)
    const = lambda bi, i: (0, 0)
    return pl.pallas_call(
        functools.partial(_attn_inproj_kernel, q_premul=ATT_HEAD_DIM ** -0.5 * math.log2(math.e)),
        grid=(b, n // tm),
        in_specs=[pl.BlockSpec((1, tm, d), row),
                  pl.BlockSpec((1, 1, d), vec),
                  pl.BlockSpec((1, 1, d), vec),
                  pl.BlockSpec(w.shape, const),
                  pl.BlockSpec((1, ATT_HEAD_DIM), const),
                  pl.BlockSpec((1, ATT_HEAD_DIM), const),
                  pl.BlockSpec((tm, ATT_HEAD_DIM), lambda bi, i: (i, 0)),
                  pl.BlockSpec((tm, ATT_HEAD_DIM), lambda bi, i: (i, 0))],
        out_specs=[pl.BlockSpec((1, tm, qw), row),
                   pl.BlockSpec((1, tm, kw), row),
                   pl.BlockSpec((1, tm, kw), row),
                   pl.BlockSpec((1, tm, qw), row)],
        out_shape=[jax.ShapeDtypeStruct((b, n, qw), _BF16),
                   jax.ShapeDtypeStruct((b, n, kw), _BF16),
                   jax.ShapeDtypeStruct((b, n, kw), _BF16),
                   jax.ShapeDtypeStruct((b, n, qw), _BF16)],
        compiler_params=_params(2),
        name="attn_inproj",
    )(x, shift, scale, w, q_scale, k_scale, cos, sin)


def _ret_inproj_kernel(x_ref, shift_ref, scale_ref, w_ref, cos_ref, sin_ref, qk_ref, v_ref, sg_ref):
    h = _modulated(x_ref, shift_ref, scale_ref)
    dk = RET_QK_DIM
    half = dk // 2
    qkw = 2 * RET_HEADS * dk
    vw = RET_HEADS * RET_V_DIM
    cos = cos_ref[...]
    sin = sin_ref[...]
    for hd in range(2 * RET_HEADS):
        p = jnp.dot(h, w_ref[:, hd * dk:(hd + 1) * dk], preferred_element_type=_F32)
        x1, x2 = p[:, :half], p[:, half:]
        o1 = x1 * cos - x2 * sin
        o2 = x1 * sin + x2 * cos
        if hd >= RET_HEADS:
            o1 = o1 * dk ** -0.5
            o2 = o2 * dk ** -0.5
        qk_ref[0, :, hd * dk:hd * dk + half] = o1.astype(_BF16)
        qk_ref[0, :, hd * dk + half:(hd + 1) * dk] = o2.astype(_BF16)
    for j in range(vw // MXU_COLS):
        v_ref[0, :, j * MXU_COLS:(j + 1) * MXU_COLS] = _slab_dot(h, w_ref, qkw + j * MXU_COLS).astype(_BF16)
    for j in range(vw // MXU_COLS):
        g = _slab_dot(h, w_ref, qkw + vw + j * MXU_COLS)
        sg_ref[0, :, j * MXU_COLS:(j + 1) * MXU_COLS] = _silu(g).astype(_BF16)


def _ret_inproj(x, shift, scale, w, cos, sin, tm):
    b, n, d = x.shape
    qkw = 2 * RET_HEADS * RET_QK_DIM
    vw = RET_HEADS * RET_V_DIM
    row = lambda bi, i: (bi, i, 0)
    vec = lambda bi, i: (bi, 0, 0)
    half = RET_QK_DIM // 2
    return pl.pallas_call(
        _ret_inproj_kernel,
        grid=(b, n // tm),
        in_specs=[pl.BlockSpec((1, tm, d), row),
                  pl.BlockSpec((1, 1, d), vec),
                  pl.BlockSpec((1, 1, d), vec),
                  pl.BlockSpec(w.shape, lambda bi, i: (0, 0), pipeline_mode=pl.Buffered(1)),
                  pl.BlockSpec((tm, half), lambda bi, i: (i, 0)),
                  pl.BlockSpec((tm, half), lambda bi, i: (i, 0))],
        out_specs=[pl.BlockSpec((1, tm, qkw), row),
                   pl.BlockSpec((1, tm, vw), row),
                   pl.BlockSpec((1, tm, vw), row)],
        out_shape=[jax.ShapeDtypeStruct((b, n, qkw), _BF16),
                   jax.ShapeDtypeStruct((b, n, vw), _BF16),
                   jax.ShapeDtypeStruct((b, n, vw), _BF16)],
        compiler_params=_params(2),
        name="ret_inproj",
    )(x, shift, scale, w, cos, sin)


def _flash_kernel(*refs, seg_chunks, tq):
    q_ref, sg_ref = refs[0], refs[1]
    kv_refs = refs[2:-1]
    t_ref = refs[-1]
    dh = ATT_HEAD_DIM
    grp = ATT_GROUP
    chunks = []
    for si, tk in enumerate(seg_chunks):
        k_ref, vt_ref = kv_refs[2 * si], kv_refs[2 * si + 1]
        chunks += [(k_ref, vt_ref, c * tk, tk) for c in range(k_ref.shape[1] // tk)]
    qs = [q_ref[0, :, i * dh:(i + 1) * dh] for i in range(grp)]

    def scores(ci, i):
        k_ref, _, start, tk = chunks[ci]
        return lax.dot_general(k_ref[0, start:start + tk, :], qs[i], (((1,), (1,)), ((), ())),
                               preferred_element_type=_F32)

    m = [jnp.full((1, tq), -jnp.inf, _F32) for _ in range(grp)]
    acc = [jnp.zeros((dh + ONES_ROWS, tq), _F32) for _ in range(grp)]
    s_next = [scores(0, i) for i in range(grp)]
    for ci, (_, vt_ref, start, tk) in enumerate(chunks):
        for i in range(grp):
            s = s_next[i]
            if ci + 1 < len(chunks):
                s_next[i] = scores(ci + 1, i)
            m_new = jnp.maximum(m[i], jnp.max(s, axis=0, keepdims=True))
            alpha = jnp.exp2(m[i] - m_new)
            p = jnp.exp2(s - m_new).astype(_BF16)
            acc[i] = alpha * acc[i] + jnp.dot(vt_ref[0, 0, :, start:start + tk], p, preferred_element_type=_F32)
            m[i] = m_new
    for i in range(grp):
        o_t = acc[i][:dh] * (1.0 / acc[i][dh:dh + 1])
        gate = sg_ref[0, :, i * dh:(i + 1) * dh].astype(_F32)
        t_ref[0, :, i * dh:(i + 1) * dh] = (o_t.T * gate).astype(_BF16)


def _flash(q, sg, kv_segments, tq):
    b, nq, qw = q.shape
    dh = ATT_HEAD_DIM
    gw = ATT_GROUP * dh
    qmap = lambda bi, hi, i: (bi, i, hi)
    kmap = lambda bi, hi, i: (bi, 0, hi)
    vmap = lambda bi, hi, i: (bi, hi, 0, 0)
    in_specs = [pl.BlockSpec((1, tq, gw), qmap), pl.BlockSpec((1, tq, gw), qmap)]
    args = [q, sg]
    for k, vt, _ in kv_segments:
        in_specs += [pl.BlockSpec((1, k.shape[1], dh), kmap),
                     pl.BlockSpec((1, 1, dh + ONES_ROWS, vt.shape[3]), vmap)]
        args += [k, vt]
    return pl.pallas_call(
        functools.partial(_flash_kernel, seg_chunks=tuple(tk for _, _, tk in kv_segments), tq=tq),
        grid=(b, ATT_KV_HEADS, nq // tq),
        in_specs=in_specs,
        out_specs=pl.BlockSpec((1, tq, gw), qmap),
        out_shape=jax.ShapeDtypeStruct((b, nq, qw), _BF16),
        compiler_params=_params(3),
        name="flash_attention",
    )(*args)


def _v_transposed(v):
    b, n, _ = v.shape
    vt = v.reshape(b, n, ATT_KV_HEADS, ATT_HEAD_DIM).transpose(0, 2, 3, 1)
    return jnp.concatenate([vt, jnp.ones((b, ATT_KV_HEADS, ONES_ROWS, n), v.dtype)], axis=2)


def _retention_kernel(lgf_ref, lgb_ref, qc_ref, kc_ref, vc_ref, sgc_ref, ql_ref, kl_ref, vl_ref, sgl_ref,
                      tc_ref, tl_ref, state_f, state_b, fc_ref, fl_ref, *, chunk, unroll):
    hd = pl.program_id(1)
    c = chunk
    n_ctx = qc_ref.shape[1] // c
    n_lat = ql_ref.shape[1] // c
    ii = lax.broadcasted_iota(jnp.int32, (c, c), 0)
    jj = lax.broadcasted_iota(jnp.int32, (c, c), 1)
    row = lax.broadcasted_iota(jnp.int32, (c, 1), 0).astype(_F32)

    def tables(lg, forward):
        diff = (ii - jj) if forward else (jj - ii)
        keep = (diff >= 0) if forward else (diff > 0)
        decay = jnp.where(keep, jnp.exp(lg * jnp.maximum(diff, 0).astype(_F32)), 0.0)
        if forward:
            xi = jnp.exp(lg * (row + 1.0))
            zeta = jnp.exp(lg * (c - 1.0 - row))
        else:
            xi = jnp.exp(lg * (c - row))
            zeta = jnp.exp(lg * row)
        g_chunk = jnp.exp(jnp.full((1, 1), lg * c, _F32))
        return decay, xi, zeta, g_chunk

    def step(q_ref, k_ref, v_ref, sg_ref, f_ref, t_ref, start, tabs, state_ref, final):
        decay, xi, zeta, g_chunk = tabs
        sl = pl.ds(start, c)
        q = q_ref[0, sl, :]
        k = k_ref[0, sl, :]
        v = v_ref[0, sl, :]
        state = state_ref[...]
        s = lax.dot_general(q, k, (((1,), (1,)), ((), ())), preferred_element_type=_F32) * decay
        o = jnp.dot(s.astype(_BF16), v, preferred_element_type=_F32)
        o = o + jnp.dot(q, state.astype(_BF16), preferred_element_type=_F32) * xi
        kz = (k.astype(_F32) * zeta).astype(_BF16)
        upd = lax.dot_general(kz, v, (((0,), (0,)), ((), ())), preferred_element_type=_F32)
        state_ref[...] = state * g_chunk + upd
        if not final:
            f_ref[sl, :] = o
        else:
            o = o + f_ref[sl, :]
            mu = jnp.mean(o, axis=-1, keepdims=True)
            oc = o - mu
            var = jnp.mean(oc * oc, axis=-1, keepdims=True)
            on = (oc * lax.rsqrt(var + GN_EPS)).astype(_BF16)
            t_ref[0, sl, :] = on * sg_ref[0, sl, :]

    tabs_f = tables(lgf_ref[hd], True)
    tabs_b = tables(lgb_ref[hd], False)
    state_f[...] = jnp.zeros_like(state_f)
    state_b[...] = jnp.zeros_like(state_b)
    for i in range(n_ctx):
        step(qc_ref, kc_ref, vc_ref, sgc_ref, fc_ref, tc_ref, i * c, tabs_f, state_f, False)
    for i in reversed(range(n_ctx)):
        step(qc_ref, kc_ref, vc_ref, sgc_ref, fc_ref, tc_ref, i * c, tabs_b, state_b, True)

    half = n_lat // 2

    def lat_body(final):
        def body(i, carry):
            lo = pl.multiple_of(i * c, c)
            hi = pl.multiple_of((n_lat - 1 - i) * c, c)
            step(ql_ref, kl_ref, vl_ref, sgl_ref, fl_ref, tl_ref, lo, tabs_f, state_f, final)
            step(ql_ref, kl_ref, vl_ref, sgl_ref, fl_ref, tl_ref, hi, tabs_b, state_b, final)
            return carry
        return body

    lax.fori_loop(0, half, lat_body(False), 0, unroll=unroll)
    lax.fori_loop(half, n_lat, lat_body(True), 0, unroll=unroll)


def _retention(lg_f, lg_b, qk_c, v_c, sg_c, qk_l, v_l, sg_l, chunk):
    b, n_lat, _ = qk_l.shape
    n_ctx = qk_c.shape[1]
    assert (n_lat // chunk) % 2 == 0, "the two scan directions meet in the middle of the latents"
    dk, dv, nh = RET_QK_DIM, RET_V_DIM, RET_HEADS
    qmap = lambda bi, hi: (bi, 0, hi)
    kmap = lambda bi, hi: (bi, 0, nh + hi)
    smem = pl.BlockSpec(memory_space=pltpu.SMEM)
    unroll = math.gcd(n_lat // chunk // 2, 4)
    return pl.pallas_call(
        functools.partial(_retention_kernel, chunk=chunk, unroll=unroll),
        grid=(b, nh),
        in_specs=[smem, smem,
                  pl.BlockSpec((1, n_ctx, dk), qmap), pl.BlockSpec((1, n_ctx, dk), kmap),
                  pl.BlockSpec((1, n_ctx, dv), qmap), pl.BlockSpec((1, n_ctx, dv), qmap),
                  pl.BlockSpec((1, n_lat, dk), qmap), pl.BlockSpec((1, n_lat, dk), kmap),
                  pl.BlockSpec((1, n_lat, dv), qmap), pl.BlockSpec((1, n_lat, dv), qmap)],
        out_specs=[pl.BlockSpec((1, n_ctx, dv), qmap), pl.BlockSpec((1, n_lat, dv), qmap)],
        out_shape=[jax.ShapeDtypeStruct((b, n_ctx, nh * dv), _BF16),
                   jax.ShapeDtypeStruct((b, n_lat, nh * dv), _BF16)],
        scratch_shapes=[pltpu.VMEM((dk, dv), _F32),
                        pltpu.VMEM((dk, dv), _F32),
                        pltpu.VMEM((n_ctx, dv), _F32),
                        pltpu.VMEM((n_lat, dv), _F32)],
        compiler_params=_params(2),
        name="retention",
    )(lg_f, lg_b, qk_c, qk_c, v_c, sg_c, qk_l, qk_l, v_l, sg_l)


def _outproj_kernel(t_ref, x_ref, w_ref, gate_ref, lng_ref, lnb_ref, o_ref, *, alpha, sub):
    n_sub = t_ref.shape[1] // sub
    proj = lambda r: jnp.dot(t_ref[0, r * sub:(r + 1) * sub, :], w_ref[...], preferred_element_type=_F32)
    y_next = proj(0)
    for r in range(n_sub):
        y = y_next
        if r + 1 < n_sub:
            y_next = proj(r + 1)
        rows = slice(r * sub, (r + 1) * sub)
        z = alpha * x_ref[0, rows, :] + gate_ref[0] * y
        mu = jnp.mean(z, axis=-1, keepdims=True)
        zc = z - mu
        var = jnp.mean(zc * zc, axis=-1, keepdims=True)
        o_ref[0, rows, :] = zc * lax.rsqrt(var + LN_EPS) * lng_ref[...] + lnb_ref[...]


def _outproj(t, x, w, gate, ln_g, ln_b, alpha, tm):
    b, n, d = x.shape
    row = lambda bi, i: (bi, i, 0)
    const = lambda bi, i: (0, 0)
    return pl.pallas_call(
        functools.partial(_outproj_kernel, alpha=alpha, sub=math.gcd(tm, OUT_SUB_ROWS)),
        grid=(b, n // tm),
        in_specs=[pl.BlockSpec((1, tm, t.shape[2]), row),
                  pl.BlockSpec((1, tm, d), row),
                  pl.BlockSpec(w.shape, const),
                  pl.BlockSpec((1, 1, d), lambda bi, i: (bi, 0, 0)),
                  pl.BlockSpec((1, d), const),
                  pl.BlockSpec((1, d), const)],
        out_specs=pl.BlockSpec((1, tm, d), row),
        out_shape=jax.ShapeDtypeStruct((b, n, d), _F32),
        compiler_params=_params(2),
        name="outproj",
    )(t, x, w, gate, ln_g, ln_b)


def _axial_perm():
    quarter = ATT_HEAD_DIM // 4
    return jnp.concatenate([jnp.arange(quarter) + off * quarter for off in (0, 2, 1, 3)])


def _axial_tables(s):
    quarter = ATT_HEAD_DIM // 4
    t = jnp.arange(s)
    pos = jnp.stack([t // GRID_W, t % GRID_W], axis=1).astype(_F32)
    freqs = ROPE_THETA ** (-jnp.arange(quarter, dtype=_F32) / quarter)
    ang = (pos[:, :, None] * freqs[None, None, :]).reshape(s, 2 * quarter)
    cos = jnp.concatenate([jnp.cos(ang), jnp.cos(ang)], axis=-1)
    sin = jnp.concatenate([-jnp.sin(ang), jnp.sin(ang)], axis=-1)
    return cos, sin


def _permute_qk_columns(w_in):
    dh = ATT_HEAD_DIM
    qw = ATT_HEADS * dh
    kw = ATT_KV_HEADS * dh
    perm = _axial_perm()
    q_cols = (jnp.arange(ATT_HEADS)[:, None] * dh + perm[None, :]).reshape(-1)
    k_cols = 2 * qw + (jnp.arange(ATT_KV_HEADS)[:, None] * dh + perm[None, :]).reshape(-1)
    cols = jnp.concatenate([q_cols, jnp.arange(qw, 2 * qw), k_cols, jnp.arange(2 * qw + kw, 2 * qw + 2 * kw)])
    return w_in[:, cols]


def _rope_tables(pos):
    half = RET_QK_DIM // 2
    freqs = ROPE_THETA ** (-jnp.arange(half, dtype=_F32) / half)
    ang = pos.astype(_F32)[:, None] * freqs[None, :]
    return jnp.cos(ang), jnp.sin(ang)


def _row_tile(n, want):
    return want if n % want == 0 else n


def kernel(x, c, ctx, c_ctx, mod_w, mod_b, ln_g, ln_b, attn_w_in, attn_w_out, attn_q_scale, attn_k_scale,
           ret_w_in, ret_w_out, ret_gn_g, ret_log_decay_fwd, ret_log_decay_bwd):
    b, s, d = x.shape
    l = ctx.shape[1]
    depth = mod_w.shape[0]
    alpha = (2.0 * depth) ** 0.25

    rows = 8 * ((b + 1 + 7) // 8)
    cvec = jnp.zeros((rows, d), _F32).at[:b].set(c).at[b].set(c_ctx)
    mods = _modulation(cvec, mod_w, mod_b)

    cos_ax, sin_ax = _axial_tables(s)
    cos_id, sin_id = jnp.ones((l, ATT_HEAD_DIM), _F32), jnp.zeros((l, ATT_HEAD_DIM), _F32)
    cos_c, sin_c = _rope_tables(jnp.arange(l))
    cos_l, sin_l = _rope_tables(l + jnp.arange(s))

    tm_l = _row_tile(s, 512)
    tm_c = _row_tile(l, 256)
    tm_out = _row_tile(s, 1024)
    tq = _row_tile(s, 256)
    tk = _row_tile(s, 512)
    chunk = RET_CHUNK if (l % RET_CHUNK == 0 and s % RET_CHUNK == 0) else 128

    for i in range(depth):
        need_ctx = i < depth - 1
        j = i // 2
        shift_l = mods[i, :b, None, :d]
        scale_l = mods[i, :b, None, d:2 * d]
        gate_l = mods[i, :b, None, 2 * d:]
        shift_c = jnp.broadcast_to(mods[i, b, None, None, :d], (b, 1, d))
        scale_c = jnp.broadcast_to(mods[i, b, None, None, d:2 * d], (b, 1, d))
        gate_c = jnp.broadcast_to(mods[i, b, None, None, 2 * d:], (b, 1, d))
        lng = ln_g[i][None, :]
        lnb = ln_b[i][None, :]
        if i % 2 == 0:
            w_in = _permute_qk_columns(attn_w_in[j]).astype(_BF16)
            w_out = attn_w_out[j].astype(_BF16)
            qs = attn_q_scale[j][_axial_perm()][None, :]
            ks = attn_k_scale[j][_axial_perm()][None, :]
            q_l, k_l, v_l, sg_l = _attn_inproj(x, shift_l, scale_l, w_in, qs, ks, cos_ax, sin_ax, tm_l)
            q_c, k_c, v_c, sg_c = _attn_inproj(ctx, shift_c, scale_c, w_in, qs, ks, cos_id, sin_id, tm_c)
            vt_l, vt_c = _v_transposed(v_l), _v_transposed(v_c)
            t_l = _flash(q_l, sg_l, [(k_l, vt_l, tk), (k_c, vt_c, l)], tq)
            if need_ctx:
                t_c = _flash(q_c, sg_c, [(k_c, vt_c, l)], _row_tile(l, 128))
        else:
            w_in = ret_w_in[j].astype(_BF16)
            w_out = (ret_gn_g[j][:, None] * ret_w_out[j]).astype(_BF16)
            qk_l, v_l, sg_l = _ret_inproj(x, shift_l, scale_l, w_in, cos_l, sin_l, tm_l)
            qk_c, v_c, sg_c = _ret_inproj(ctx, shift_c, scale_c, w_in, cos_c, sin_c, tm_c)
            t_c, t_l = _retention(ret_log_decay_fwd[j], ret_log_decay_bwd[j],
                                  qk_c, v_c, sg_c, qk_l, v_l, sg_l, chunk)
        x = _outproj(t_l, x, w_out, gate_l, lng, lnb, alpha, tm_out)
        if need_ctx:
            ctx = _outproj(t_c, ctx, w_out, gate_c, lng, lnb, alpha, tm_c)
    return x
```

```python
import functools
import math

import jax
import jax.numpy as jnp
from jax import lax
from jax.experimental import pallas as pl
from jax.experimental.pallas import tpu as pltpu

GRID_W = 64
ROPE_THETA = 10000.0

ATT_HEADS = 8
ATT_KV_HEADS = 2
ATT_GROUP = ATT_HEADS // ATT_KV_HEADS
ATT_HEAD_DIM = 128
ONES_ROWS = 16

RET_HEADS = 4
RET_QK_DIM = 256
RET_V_DIM = 512
RET_CHUNK = 256

LN_EPS = 1e-5
QK_EPS = 1e-6
GN_EPS = 1e-5

MXU_COLS = 256
DOT_COLS = 2 * MXU_COLS
FLASH_PREFETCH = 4
FLASH_SUB_BLOCKS = 2
OUT_SUB_ROWS = 256
VMEM_LIMIT = 56 * 1024 * 1024

_BF16 = jnp.bfloat16
_F32 = jnp.float32


def _params(n_grid):
    return pltpu.CompilerParams(dimension_semantics=("arbitrary",) * n_grid,
                                vmem_limit_bytes=VMEM_LIMIT)


def _silu(g):
    return g * jax.nn.sigmoid(g)


def _mod_kernel(c_ref, w_ref, b_ref, o_ref, *, gate_mul):
    sc = _silu(c_ref[...])
    out = jnp.dot(sc, w_ref[0], preferred_element_type=_F32) + b_ref[0]
    o_ref[0] = out * jnp.where(pl.program_id(1) == 2, gate_mul, 1.0)


def _modulation(cvec, mod_w, mod_b, alpha):
    depth, d, d3 = mod_w.shape
    r = cvec.shape[0]
    return pl.pallas_call(
        functools.partial(_mod_kernel, gate_mul=1.0 / alpha),
        grid=(depth, d3 // d),
        in_specs=[pl.BlockSpec((r, d), lambda i, j: (0, 0)),
                  pl.BlockSpec((1, d, d), lambda i, j: (i, 0, j)),
                  pl.BlockSpec((1, 1, d), lambda i, j: (i, 0, j))],
        out_specs=pl.BlockSpec((1, r, d), lambda i, j: (i, 0, j)),
        out_shape=jax.ShapeDtypeStruct((depth, r, d3), _F32),
        compiler_params=_params(2),
        name="modulation",
    )(cvec, mod_w, mod_b.reshape(depth, 1, d3))


def _modulated(x_ref, shift_ref, scale_ref):
    return (x_ref[0] * (1.0 + scale_ref[0]) + shift_ref[0]).astype(_BF16)


def _slab_dot(h, w_ref, start):
    return jnp.dot(h, w_ref[:, start:start + MXU_COLS], preferred_element_type=_F32)


def _attn_inproj_kernel(x_ref, shift_ref, scale_ref, w_ref, qs_ref, ks_ref, cos_ref, sin_ref,
                        q_ref, k_ref, v_ref, sg_ref, *, q_premul):
    h = _modulated(x_ref, shift_ref, scale_ref)
    dh = ATT_HEAD_DIM
    qw = ATT_HEADS * dh
    kw = ATT_KV_HEADS * dh
    cos = cos_ref[...]
    sin = sin_ref[...]

    def norm_rope_heads(o_ref, scale_row, mul, p):
        ts = [p[:, u * dh:(u + 1) * dh] for u in range(p.shape[1] // dh)]
        ms = [jnp.mean(t * t, axis=-1, keepdims=True) for t in ts]
        rs = [lax.rsqrt(m + QK_EPS) for m in ms]
        ts = [t * r * scale_row for t, r in zip(ts, rs)]
        rolled = [pltpu.roll(t, dh // 2, axis=1) for t in ts]
        ts = [t * cos + r * sin for t, r in zip(ts, rolled)]
        for hd, t in enumerate(ts):
            o_ref[0, :, hd * dh:(hd + 1) * dh] = (t if mul == 1.0 else t * mul).astype(_BF16)

    dot = lambda start, width: jnp.dot(h, w_ref[:, start:start + width], preferred_element_type=_F32)
    p = dot(2 * qw, 2 * kw)
    norm_rope_heads(k_ref, ks_ref[...], 1.0, p[:, :kw])
    v_ref[0] = p[:, kw:].astype(_BF16)
    norm_rope_heads(q_ref, qs_ref[...], q_premul, dot(0, qw))
    sg_ref[0] = _silu(dot(qw, qw)).astype(_BF16)


def _attn_inproj(x, shift, scale, w, q_scale, k_scale, cos, sin, tm):
    b, n, d = x.shape
    qw = ATT_HEADS * ATT_HEAD_DIM
    kw = ATT_KV_HEADS * ATT_HEAD_DIM
    row = lambda bi, i: (bi, i, 0)
    vec = lambda bi, i: (bi, 0, 0)
    const = lambda bi, i: (0, 0)
    return pl.pallas_call(
        functools.partial(_attn_inproj_kernel, q_premul=ATT_HEAD_DIM ** -0.5 * math.log2(math.e)),
        grid=(b, n // tm),
        in_specs=[pl.BlockSpec((1, tm, d), row),
                  pl.BlockSpec((1, 1, d), vec),
                  pl.BlockSpec((1, 1, d), vec),
                  pl.BlockSpec(w.shape, const),
                  pl.BlockSpec((1, ATT_HEAD_DIM), const),
                  pl.BlockSpec((1, ATT_HEAD_DIM), const),
                  pl.BlockSpec((tm, ATT_HEAD_DIM), lambda bi, i: (i, 0)),
                  pl.BlockSpec((tm, ATT_HEAD_DIM), lambda bi, i: (i, 0))],
        out_specs=[pl.BlockSpec((1, tm, qw), row),
                   pl.BlockSpec((1, tm, kw), row),
                   pl.BlockSpec((1, tm, kw), row),
                   pl.BlockSpec((1, tm, qw), row)],
        out_shape=[jax.ShapeDtypeStruct((b, n, qw), _BF16),
                   jax.ShapeDtypeStruct((b, n, kw), _BF16),
                   jax.ShapeDtypeStruct((b, n, kw), _BF16),
                   jax.ShapeDtypeStruct((b, n, qw), _BF16)],
        compiler_params=_params(2),
        name="attn_inproj",
    )(x, shift, scale, w, q_scale, k_scale, cos, sin)


def _ret_inproj_kernel(x_ref, shift_ref, scale_ref, w_ref, cos_ref, sin_ref, qk_ref, v_ref, sg_ref):
    h = _modulated(x_ref, shift_ref, scale_ref)
    dk = RET_QK_DIM
    half = dk // 2
    qkw = 2 * RET_HEADS * dk
    vw = RET_HEADS * RET_V_DIM
    cos = cos_ref[...]
    sin = sin_ref[...]
    for hd in range(2 * RET_HEADS):
        p = jnp.dot(h, w_ref[:, hd * dk:(hd + 1) * dk], preferred_element_type=_F32)
        x1, x2 = p[:, :half], p[:, half:]
        o1 = x1 * cos - x2 * sin
        o2 = x1 * sin + x2 * cos
        if hd >= RET_HEADS:
            o1 = o1 * dk ** -0.5
            o2 = o2 * dk ** -0.5
        qk_ref[0, :, hd * dk:hd * dk + half] = o1.astype(_BF16)
        qk_ref[0, :, hd * dk + half:(hd + 1) * dk] = o2.astype(_BF16)
    for j in range(vw // MXU_COLS):
        v_ref[0, :, j * MXU_COLS:(j + 1) * MXU_COLS] = _slab_dot(h, w_ref, qkw + j * MXU_COLS).astype(_BF16)
    for j in range(vw // MXU_COLS):
        g = _slab_dot(h, w_ref, qkw + vw + j * MXU_COLS)
        sg_ref[0, :, j * MXU_COLS:(j + 1) * MXU_COLS] = _silu(g).astype(_BF16)


def _ret_inproj(x, shift, scale, w, cos, sin, tm):
    b, n, d = x.shape
    qkw = 2 * RET_HEADS * RET_QK_DIM
    vw = RET_HEADS * RET_V_DIM
    row = lambda bi, i: (bi, i, 0)
    vec = lambda bi, i: (bi, 0, 0)
    half = RET_QK_DIM // 2
    return pl.pallas_call(
        _ret_inproj_kernel,
        grid=(b, n // tm),
        in_specs=[pl.BlockSpec((1, tm, d), row),
                  pl.BlockSpec((1, 1, d), vec),
                  pl.BlockSpec((1, 1, d), vec),
                  pl.BlockSpec(w.shape, lambda bi, i: (0, 0), pipeline_mode=pl.Buffered(1)),
                  pl.BlockSpec((tm, half), lambda bi, i: (i, 0)),
                  pl.BlockSpec((tm, half), lambda bi, i: (i, 0))],
        out_specs=[pl.BlockSpec((1, tm, qkw), row),
                   pl.BlockSpec((1, tm, vw), row),
                   pl.BlockSpec((1, tm, vw), row)],
        out_shape=[jax.ShapeDtypeStruct((b, n, qkw), _BF16),
                   jax.ShapeDtypeStruct((b, n, vw), _BF16),
                   jax.ShapeDtypeStruct((b, n, vw), _BF16)],
        compiler_params=_params(2),
        name="ret_inproj",
    )(x, shift, scale, w, cos, sin)


def _flash_kernel(*refs, seg_chunks, tq, n_sub):
    q_ref, sg_ref = refs[0], refs[1]
    kv_refs = refs[2:-1]
    t_ref = refs[-1]
    dh = ATT_HEAD_DIM
    grp = ATT_GROUP
    chunks = []
    for si, tk in enumerate(seg_chunks):
        k_ref, vt_ref = kv_refs[2 * si], kv_refs[2 * si + 1]
        chunks += [(k_ref, vt_ref, c * tk, tk) for c in range(k_ref.shape[1] // tk)]
    units = [(sb, ci, i) for sb in range(n_sub) for ci in range(len(chunks)) for i in range(grp)]

    def scores(sb, ci, i):
        k_ref, _, start, tk = chunks[ci]
        q = q_ref[0, sb * tq:(sb + 1) * tq, i * dh:(i + 1) * dh]
        return lax.dot_general(k_ref[0, start:start + tk, :], q, (((1,), (1,)), ((), ())),
                               preferred_element_type=_F32)

    m, acc = {}, {}
    pending = [scores(*u) for u in units[:FLASH_PREFETCH]]
    for n, (sb, ci, i) in enumerate(units):
        _, vt_ref, start, tk = chunks[ci]
        s = pending.pop(0)
        if n + FLASH_PREFETCH < len(units):
            pending.append(scores(*units[n + FLASH_PREFETCH]))
        if ci == 0:
            m[sb, i] = jnp.full((1, tq), -jnp.inf, _F32)
            acc[sb, i] = jnp.zeros((dh + ONES_ROWS, tq), _F32)
        m_new = jnp.maximum(m[sb, i], jnp.max(s, axis=0, keepdims=True))
        alpha = jnp.exp2(m[sb, i] - m_new)
        p = jnp.exp2(s - m_new).astype(_BF16)
        acc[sb, i] = alpha * acc[sb, i] + jnp.dot(vt_ref[0, 0, :, start:start + tk], p,
                                                  preferred_element_type=_F32)
        m[sb, i] = m_new
        if ci == len(chunks) - 1:
            a = acc.pop((sb, i))
            o_t = a[:dh] * (1.0 / a[dh:dh + 1])
            gate = sg_ref[0, sb * tq:(sb + 1) * tq, i * dh:(i + 1) * dh].astype(_F32)
            t_ref[0, sb * tq:(sb + 1) * tq, i * dh:(i + 1) * dh] = (o_t.T * gate).astype(_BF16)


def _flash(q, sg, kv_segments, tq, n_sub=1):
    b, nq, qw = q.shape
    dh = ATT_HEAD_DIM
    gw = ATT_GROUP * dh
    qmap = lambda bi, hi, i: (bi, i, hi)
    kmap = lambda bi, hi, i: (bi, 0, hi)
    vmap = lambda bi, hi, i: (bi, hi, 0, 0)
    tstep = tq * n_sub
    in_specs = [pl.BlockSpec((1, tstep, gw), qmap), pl.BlockSpec((1, tstep, gw), qmap)]
    args = [q, sg]
    for k, vt, _ in kv_segments:
        in_specs += [pl.BlockSpec((1, k.shape[1], dh), kmap),
                     pl.BlockSpec((1, 1, dh + ONES_ROWS, vt.shape[3]), vmap)]
        args += [k, vt]
    return pl.pallas_call(
        functools.partial(_flash_kernel, seg_chunks=tuple(tk for _, _, tk in kv_segments), tq=tq, n_sub=n_sub),
        grid=(b, ATT_KV_HEADS, nq // tstep),
        in_specs=in_specs,
        out_specs=pl.BlockSpec((1, tstep, gw), qmap),
        out_shape=jax.ShapeDtypeStruct((b, nq, qw), _BF16),
        compiler_params=_params(3),
        name="flash_attention",
    )(*args)


def _v_transposed(v):
    b, n, _ = v.shape
    vt = v.reshape(b, n, ATT_KV_HEADS, ATT_HEAD_DIM).transpose(0, 2, 3, 1)
    return jnp.concatenate([vt, jnp.ones((b, ATT_KV_HEADS, ONES_ROWS, n), v.dtype)], axis=2)


def _retention_kernel(lgf_ref, lgb_ref, qc_ref, kc_ref, vc_ref, sgc_ref, ql_ref, kl_ref, vl_ref, sgl_ref,
                      tc_ref, tl_ref, state_f, state_b, fc_ref, fl_ref, *, chunk, unroll):
    hd = pl.program_id(1)
    c = chunk
    n_ctx = qc_ref.shape[1] // c
    n_lat = ql_ref.shape[1] // c
    ii = lax.broadcasted_iota(jnp.int32, (c, c), 0)
    jj = lax.broadcasted_iota(jnp.int32, (c, c), 1)
    row = lax.broadcasted_iota(jnp.int32, (c, 1), 0).astype(_F32)

    def tables(lg, forward):
        diff = (ii - jj) if forward else (jj - ii)
        keep = (diff >= 0) if forward else (diff > 0)
        decay = jnp.where(keep, jnp.exp(lg * jnp.maximum(diff, 0).astype(_F32)), 0.0)
        if forward:
            xi = jnp.exp(lg * (row + 1.0))
            zeta = jnp.exp(lg * (c - 1.0 - row))
        else:
            xi = jnp.exp(lg * (c - row))
            zeta = jnp.exp(lg * row)
        g_chunk = jnp.exp(jnp.full((1, 1), lg * c, _F32))
        return decay, xi, zeta, g_chunk

    def step(q_ref, k_ref, v_ref, sg_ref, f_ref, t_ref, start, tabs, state_ref, final):
        decay, xi, zeta, g_chunk = tabs
        sl = pl.ds(start, c)
        q = q_ref[0, sl, :]
        k = k_ref[0, sl, :]
        v = v_ref[0, sl, :]
        state = state_ref[...]
        s = lax.dot_general(q, k, (((1,), (1,)), ((), ())), preferred_element_type=_F32) * decay
        o = jnp.dot(s.astype(_BF16), v, preferred_element_type=_F32)
        o = o + jnp.dot(q, state.astype(_BF16), preferred_element_type=_F32) * xi
        kz = (k.astype(_F32) * zeta).astype(_BF16)
        upd = lax.dot_general(kz, v, (((0,), (0,)), ((), ())), preferred_element_type=_F32)
        state_ref[...] = state * g_chunk + upd
        if not final:
            f_ref[sl, :] = o
        else:
            o = o + f_ref[sl, :]
            mu = jnp.mean(o, axis=-1, keepdims=True)
            oc = o - mu
            var = jnp.mean(oc * oc, axis=-1, keepdims=True)
            on = (oc * lax.rsqrt(var + GN_EPS)).astype(_BF16)
            t_ref[0, sl, :] = on * sg_ref[0, sl, :]

    tabs_f = tables(lgf_ref[hd], True)
    tabs_b = tables(lgb_ref[hd], False)
    state_f[...] = jnp.zeros_like(state_f)
    state_b[...] = jnp.zeros_like(state_b)
    for i in range(n_ctx):
        step(qc_ref, kc_ref, vc_ref, sgc_ref, fc_ref, tc_ref, i * c, tabs_f, state_f, False)
    for i in reversed(range(n_ctx)):
        step(qc_ref, kc_ref, vc_ref, sgc_ref, fc_ref, tc_ref, i * c, tabs_b, state_b, True)

    half = n_lat // 2

    def lat_body(final):
        def body(i, carry):
            lo = pl.multiple_of(i * c, c)
            hi = pl.multiple_of((n_lat - 1 - i) * c, c)
            step(ql_ref, kl_ref, vl_ref, sgl_ref, fl_ref, tl_ref, lo, tabs_f, state_f, final)
            step(ql_ref, kl_ref, vl_ref, sgl_ref, fl_ref, tl_ref, hi, tabs_b, state_b, final)
            return carry
        return body

    lax.fori_loop(0, half, lat_body(False), 0, unroll=unroll)
    lax.fori_loop(half, n_lat, lat_body(True), 0, unroll=unroll)


def _retention(lg_f, lg_b, qk_c, v_c, sg_c, qk_l, v_l, sg_l, chunk):
    b, n_lat, _ = qk_l.shape
    n_ctx = qk_c.shape[1]
    assert (n_lat // chunk) % 2 == 0, "the two scan directions meet in the middle of the latents"
    dk, dv, nh = RET_QK_DIM, RET_V_DIM, RET_HEADS
    qmap = lambda bi, hi: (bi, 0, hi)
    kmap = lambda bi, hi: (bi, 0, nh + hi)
    smem = pl.BlockSpec(memory_space=pltpu.SMEM)
    unroll = math.gcd(n_lat // chunk // 2, 4)
    return pl.pallas_call(
        functools.partial(_retention_kernel, chunk=chunk, unroll=unroll),
        grid=(b, nh),
        in_specs=[smem, smem,
                  pl.BlockSpec((1, n_ctx, dk), qmap), pl.BlockSpec((1, n_ctx, dk), kmap),
                  pl.BlockSpec((1, n_ctx, dv), qmap), pl.BlockSpec((1, n_ctx, dv), qmap),
                  pl.BlockSpec((1, n_lat, dk), qmap), pl.BlockSpec((1, n_lat, dk), kmap),
                  pl.BlockSpec((1, n_lat, dv), qmap), pl.BlockSpec((1, n_lat, dv), qmap)],
        out_specs=[pl.BlockSpec((1, n_ctx, dv), qmap), pl.BlockSpec((1, n_lat, dv), qmap)],
        out_shape=[jax.ShapeDtypeStruct((b, n_ctx, nh * dv), _BF16),
                   jax.ShapeDtypeStruct((b, n_lat, nh * dv), _BF16)],
        scratch_shapes=[pltpu.VMEM((dk, dv), _F32),
                        pltpu.VMEM((dk, dv), _F32),
                        pltpu.VMEM((n_ctx, dv), _F32),
                        pltpu.VMEM((n_lat, dv), _F32)],
        compiler_params=_params(2),
        name="retention",
    )(lg_f, lg_b, qk_c, qk_c, v_c, sg_c, qk_l, qk_l, v_l, sg_l)


def _outproj_kernel(t_ref, x_ref, w_ref, gate_ref, lng_ref, lnb_ref, o_ref, *, eps, sub):
    n_sub = t_ref.shape[1] // sub
    proj = lambda r: jnp.dot(t_ref[0, r * sub:(r + 1) * sub, :], w_ref[...], preferred_element_type=_F32)
    y_next = proj(0)
    for r in range(n_sub):
        y = y_next
        if r + 1 < n_sub:
            y_next = proj(r + 1)
        rows = slice(r * sub, (r + 1) * sub)
        z = x_ref[0, rows, :] + gate_ref[0] * y
        mu = jnp.mean(z, axis=-1, keepdims=True)
        zc = z - mu
        var = jnp.mean(zc * zc, axis=-1, keepdims=True)
        o_ref[0, rows, :] = zc * lax.rsqrt(var + eps) * lng_ref[...] + lnb_ref[...]


def _outproj(t, x, w, gate, ln_g, ln_b, alpha, tm):
    b, n, d = x.shape
    row = lambda bi, i: (bi, i, 0)
    const = lambda bi, i: (0, 0)
    return pl.pallas_call(
        functools.partial(_outproj_kernel, eps=LN_EPS / alpha ** 2, sub=math.gcd(tm, OUT_SUB_ROWS)),
        grid=(b, n // tm),
        in_specs=[pl.BlockSpec((1, tm, t.shape[2]), row),
                  pl.BlockSpec((1, tm, d), row),
                  pl.BlockSpec(w.shape, const),
                  pl.BlockSpec((1, 1, d), lambda bi, i: (bi, 0, 0)),
                  pl.BlockSpec((1, d), const),
                  pl.BlockSpec((1, d), const)],
        out_specs=pl.BlockSpec((1, tm, d), row),
        out_shape=jax.ShapeDtypeStruct((b, n, d), _F32),
        compiler_params=_params(2),
        name="outproj",
    )(t, x, w, gate, ln_g, ln_b)


def _axial_perm():
    quarter = ATT_HEAD_DIM // 4
    return jnp.concatenate([jnp.arange(quarter) + off * quarter for off in (0, 2, 1, 3)])


def _axial_tables(s):
    quarter = ATT_HEAD_DIM // 4
    t = jnp.arange(s)
    pos = jnp.stack([t // GRID_W, t % GRID_W], axis=1).astype(_F32)
    freqs = ROPE_THETA ** (-jnp.arange(quarter, dtype=_F32) / quarter)
    ang = (pos[:, :, None] * freqs[None, None, :]).reshape(s, 2 * quarter)
    cos = jnp.concatenate([jnp.cos(ang), jnp.cos(ang)], axis=-1)
    sin = jnp.concatenate([-jnp.sin(ang), jnp.sin(ang)], axis=-1)
    return cos, sin


def _permute_qk_columns(w_in):
    dh = ATT_HEAD_DIM
    qw = ATT_HEADS * dh
    kw = ATT_KV_HEADS * dh
    perm = _axial_perm()
    q_cols = (jnp.arange(ATT_HEADS)[:, None] * dh + perm[None, :]).reshape(-1)
    k_cols = 2 * qw + (jnp.arange(ATT_KV_HEADS)[:, None] * dh + perm[None, :]).reshape(-1)
    cols = jnp.concatenate([q_cols, jnp.arange(qw, 2 * qw), k_cols, jnp.arange(2 * qw + kw, 2 * qw + 2 * kw)])
    return w_in[:, cols]


def _rope_tables(pos):
    half = RET_QK_DIM // 2
    freqs = ROPE_THETA ** (-jnp.arange(half, dtype=_F32) / half)
    ang = pos.astype(_F32)[:, None] * freqs[None, :]
    return jnp.cos(ang), jnp.sin(ang)


def _row_tile(n, want):
    return want if n % want == 0 else n


def kernel(x, c, ctx, c_ctx, mod_w, mod_b, ln_g, ln_b, attn_w_in, attn_w_out, attn_q_scale, attn_k_scale,
           ret_w_in, ret_w_out, ret_gn_g, ret_log_decay_fwd, ret_log_decay_bwd):
    b, s, d = x.shape
    l = ctx.shape[1]
    depth = mod_w.shape[0]
    alpha = (2.0 * depth) ** 0.25

    rows = 8 * ((b + 1 + 7) // 8)
    cvec = jnp.zeros((rows, d), _F32).at[:b].set(c).at[b].set(c_ctx)
    mods = _modulation(cvec, mod_w, mod_b, alpha)

    cos_ax, sin_ax = _axial_tables(s)
    cos_id, sin_id = jnp.ones((l, ATT_HEAD_DIM), _F32), jnp.zeros((l, ATT_HEAD_DIM), _F32)
    cos_c, sin_c = _rope_tables(jnp.arange(l))
    cos_l, sin_l = _rope_tables(l + jnp.arange(s))

    tm_l = _row_tile(s, 512)
    tm_c = _row_tile(l, 256)
    tm_out = _row_tile(s, 1024)
    tq = _row_tile(s, 256)
    tk = _row_tile(s, 512)
    chunk = RET_CHUNK if (l % RET_CHUNK == 0 and s % RET_CHUNK == 0) else 128

    for i in range(depth):
        need_ctx = i < depth - 1
        j = i // 2
        shift_l = mods[i, :b, None, :d]
        scale_l = mods[i, :b, None, d:2 * d]
        gate_l = mods[i, :b, None, 2 * d:]
        shift_c = jnp.broadcast_to(mods[i, b, None, None, :d], (b, 1, d))
        scale_c = jnp.broadcast_to(mods[i, b, None, None, d:2 * d], (b, 1, d))
        gate_c = jnp.broadcast_to(mods[i, b, None, None, 2 * d:], (b, 1, d))
        lng = ln_g[i][None, :]
        lnb = ln_b[i][None, :]
        if i % 2 == 0:
            w_in = _permute_qk_columns(attn_w_in[j]).astype(_BF16)
            w_out = attn_w_out[j].astype(_BF16)
            qs = attn_q_scale[j][_axial_perm()][None, :]
            ks = attn_k_scale[j][_axial_perm()][None, :]
            q_l, k_l, v_l, sg_l = _attn_inproj(x, shift_l, scale_l, w_in, qs, ks, cos_ax, sin_ax, tm_l)
            q_c, k_c, v_c, sg_c = _attn_inproj(ctx, shift_c, scale_c, w_in, qs, ks, cos_id, sin_id, tm_c)
            vt_l, vt_c = _v_transposed(v_l), _v_transposed(v_c)
            t_l = _flash(q_l, sg_l, [(k_l, vt_l, tk), (k_c, vt_c, l)], tq,
                         FLASH_SUB_BLOCKS if s % (tq * FLASH_SUB_BLOCKS) == 0 else 1)
            if need_ctx:
                t_c = _flash(q_c, sg_c, [(k_c, vt_c, l)], _row_tile(l, 128))
        else:
            w_in = ret_w_in[j].astype(_BF16)
            w_out = (ret_gn_g[j][:, None] * ret_w_out[j]).astype(_BF16)
            qk_l, v_l, sg_l = _ret_inproj(x, shift_l, scale_l, w_in, cos_l, sin_l, tm_l)
            qk_c, v_c, sg_c = _ret_inproj(ctx, shift_c, scale_c, w_in, cos_c, sin_c, tm_c)
            t_c, t_l = _retention(ret_log_decay_fwd[j], ret_log_decay_bwd[j],
                                  qk_c, v_c, sg_c, qk_l, v_l, sg_l, chunk)
        x = _outproj(t_l, x, w_out, gate_l, lng, lnb, alpha, tm_out)
        if need_ctx:
            ctx = _outproj(t_c, ctx, w_out, gate_c, lng, lnb, alpha, tm_c)
    return x
```

```python
import functools
import math

import jax
import jax.numpy as jnp
import numpy as np
from jax import lax
from jax.experimental import pallas as pl
from jax.experimental.pallas import tpu as pltpu

GRID_W = 64
ROPE_THETA = 10000.0

ATT_HEADS = 8
ATT_KV_HEADS = 2
ATT_GROUP = ATT_HEADS // ATT_KV_HEADS
ATT_HEAD_DIM = 128
ONES_ROWS = 16

RET_HEADS = 4
RET_QK_DIM = 256
RET_V_DIM = 512
RET_CHUNK = 256

LN_EPS = 1e-5
QK_EPS = 1e-6
GN_EPS = 1e-5

MXU_COLS = 256
DOT_COLS = 2 * MXU_COLS
FLASH_PREFETCH = 4
FLASH_SUB_BLOCKS = 2
OUT_SUB_ROWS = 256
VMEM_LIMIT = 56 * 1024 * 1024

_BF16 = jnp.bfloat16
_F32 = jnp.float32


def _params(n_grid):
    return pltpu.CompilerParams(dimension_semantics=("arbitrary",) * n_grid,
                                vmem_limit_bytes=VMEM_LIMIT)


def _silu(g):
    return g * jax.nn.sigmoid(g)


def _mod_kernel(c_ref, w_ref, b_ref, o_ref, *, gate_mul):
    sc = _silu(c_ref[...])
    out = jnp.dot(sc, w_ref[0], preferred_element_type=_F32) + b_ref[0]
    o_ref[0] = out * jnp.where(pl.program_id(1) == 2, gate_mul, 1.0)


def _modulation(cvec, mod_w, mod_b, alpha):
    depth, d, d3 = mod_w.shape
    r = cvec.shape[0]
    return pl.pallas_call(
        functools.partial(_mod_kernel, gate_mul=1.0 / alpha),
        grid=(depth, d3 // d),
        in_specs=[pl.BlockSpec((r, d), lambda i, j: (0, 0)),
                  pl.BlockSpec((1, d, d), lambda i, j: (i, 0, j)),
                  pl.BlockSpec((1, 1, d), lambda i, j: (i, 0, j))],
        out_specs=pl.BlockSpec((1, r, d), lambda i, j: (i, 0, j)),
        out_shape=jax.ShapeDtypeStruct((depth, r, d3), _F32),
        compiler_params=_params(2),
        name="modulation",
    )(cvec, mod_w, mod_b.reshape(depth, 1, d3))


def _modulated(x_ref, shift_ref, scale_ref):
    return (x_ref[0] * (1.0 + scale_ref[0]) + shift_ref[0]).astype(_BF16)


def _slab_dot(h, w_ref, start):
    return jnp.dot(h, w_ref[:, start:start + MXU_COLS], preferred_element_type=_F32)


def _attn_inproj_kernel(x_ref, shift_ref, scale_ref, w_ref, qs_ref, ks_ref, cos_ref, sin_ref,
                        q_ref, k_ref, vt_ref, sg_ref, *, q_premul):
    h = _modulated(x_ref, shift_ref, scale_ref)
    dh = ATT_HEAD_DIM
    qw = ATT_HEADS * dh
    kw = ATT_KV_HEADS * dh
    cos = cos_ref[...]
    sin = sin_ref[...]

    def norm_rope_heads(o_ref, scale_row, mul, p):
        ts = [p[:, u * dh:(u + 1) * dh] for u in range(p.shape[1] // dh)]
        ms = [jnp.mean(t * t, axis=-1, keepdims=True) for t in ts]
        rs = [lax.rsqrt(m + QK_EPS) for m in ms]
        ts = [t * r * scale_row for t, r in zip(ts, rs)]
        rolled = [pltpu.roll(t, dh // 2, axis=1) for t in ts]
        ts = [t * cos + r * sin for t, r in zip(ts, rolled)]
        for hd, t in enumerate(ts):
            o_ref[0, :, hd * dh:(hd + 1) * dh] = (t if mul == 1.0 else t * mul).astype(_BF16)

    dot = lambda start, width: jnp.dot(h, w_ref[:, start:start + width], preferred_element_type=_F32)
    p = dot(2 * qw, 2 * kw)
    norm_rope_heads(k_ref, ks_ref[...], 1.0, p[:, :kw])
    for hd in range(ATT_KV_HEADS):
        vt_ref[0, hd, :dh, :] = p[:, kw + hd * dh:kw + (hd + 1) * dh].T.astype(_BF16)
        vt_ref[0, hd, dh:, :] = jnp.ones((ONES_ROWS, p.shape[0]), _BF16)
    norm_rope_heads(q_ref, qs_ref[...], q_premul, dot(0, qw))
    sg_ref[0] = _silu(dot(qw, qw)).astype(_BF16)


def _attn_inproj(x, shift, scale, w, layer, q_scale, k_scale, cos, sin, tm):
    b, n, d = x.shape
    qw = ATT_HEADS * ATT_HEAD_DIM
    kw = ATT_KV_HEADS * ATT_HEAD_DIM
    row = lambda bi, i: (bi, i, 0)
    vec = lambda bi, i: (bi, 0, 0)
    const = lambda bi, i: (0, 0)
    return pl.pallas_call(
        functools.partial(_attn_inproj_kernel, q_premul=ATT_HEAD_DIM ** -0.5 * math.log2(math.e)),
        grid=(b, n // tm),
        in_specs=[pl.BlockSpec((1, tm, d), row),
                  pl.BlockSpec((1, 1, d), vec),
                  pl.BlockSpec((1, 1, d), vec),
                  pl.BlockSpec((None,) + w.shape[1:], lambda bi, i: (layer, 0, 0)),
                  pl.BlockSpec((1, ATT_HEAD_DIM), const),
                  pl.BlockSpec((1, ATT_HEAD_DIM), const),
                  pl.BlockSpec((tm, ATT_HEAD_DIM), lambda bi, i: (i, 0)),
                  pl.BlockSpec((tm, ATT_HEAD_DIM), lambda bi, i: (i, 0))],
        out_specs=[pl.BlockSpec((1, tm, qw), row),
                   pl.BlockSpec((1, tm, kw), row),
                   pl.BlockSpec((1, ATT_KV_HEADS, ATT_HEAD_DIM + ONES_ROWS, tm), lambda bi, i: (bi, 0, 0, i)),
                   pl.BlockSpec((1, tm, qw), row)],
        out_shape=[jax.ShapeDtypeStruct((b, n, qw), _BF16),
                   jax.ShapeDtypeStruct((b, n, kw), _BF16),
                   jax.ShapeDtypeStruct((b, ATT_KV_HEADS, ATT_HEAD_DIM + ONES_ROWS, n), _BF16),
                   jax.ShapeDtypeStruct((b, n, qw), _BF16)],
        compiler_params=_params(2),
        name="attn_inproj",
    )(x, shift, scale, w, q_scale, k_scale, cos, sin)


def _ret_inproj_kernel(x_ref, shift_ref, scale_ref, w_ref, cos_ref, sin_ref, qk_ref, v_ref, sg_ref):
    h = _modulated(x_ref, shift_ref, scale_ref)
    dk = RET_QK_DIM
    half = dk // 2
    qkw = 2 * RET_HEADS * dk
    vw = RET_HEADS * RET_V_DIM
    cos = cos_ref[...]
    sin = sin_ref[...]
    for hd in range(2 * RET_HEADS):
        p = jnp.dot(h, w_ref[:, hd * dk:(hd + 1) * dk], preferred_element_type=_F32)
        x1, x2 = p[:, :half], p[:, half:]
        o1 = x1 * cos - x2 * sin
        o2 = x1 * sin + x2 * cos
        if hd >= RET_HEADS:
            o1 = o1 * dk ** -0.5
            o2 = o2 * dk ** -0.5
        qk_ref[0, :, hd * dk:hd * dk + half] = o1.astype(_BF16)
        qk_ref[0, :, hd * dk + half:(hd + 1) * dk] = o2.astype(_BF16)
    for j in range(vw // MXU_COLS):
        v_ref[0, :, j * MXU_COLS:(j + 1) * MXU_COLS] = _slab_dot(h, w_ref, qkw + j * MXU_COLS).astype(_BF16)
    for j in range(vw // MXU_COLS):
        g = _slab_dot(h, w_ref, qkw + vw + j * MXU_COLS)
        sg_ref[0, :, j * MXU_COLS:(j + 1) * MXU_COLS] = _silu(g).astype(_BF16)


def _ret_inproj(x, shift, scale, w, layer, cos, sin, tm):
    b, n, d = x.shape
    qkw = 2 * RET_HEADS * RET_QK_DIM
    vw = RET_HEADS * RET_V_DIM
    row = lambda bi, i: (bi, i, 0)
    vec = lambda bi, i: (bi, 0, 0)
    half = RET_QK_DIM // 2
    return pl.pallas_call(
        _ret_inproj_kernel,
        grid=(b, n // tm),
        in_specs=[pl.BlockSpec((1, tm, d), row),
                  pl.BlockSpec((1, 1, d), vec),
                  pl.BlockSpec((1, 1, d), vec),
                  pl.BlockSpec((None,) + w.shape[1:], lambda bi, i: (layer, 0, 0), pipeline_mode=pl.Buffered(1)),
                  pl.BlockSpec((tm, half), lambda bi, i: (i, 0)),
                  pl.BlockSpec((tm, half), lambda bi, i: (i, 0))],
        out_specs=[pl.BlockSpec((1, tm, qkw), row),
                   pl.BlockSpec((1, tm, vw), row),
                   pl.BlockSpec((1, tm, vw), row)],
        out_shape=[jax.ShapeDtypeStruct((b, n, qkw), _BF16),
                   jax.ShapeDtypeStruct((b, n, vw), _BF16),
                   jax.ShapeDtypeStruct((b, n, vw), _BF16)],
        compiler_params=_params(2),
        name="ret_inproj",
    )(x, shift, scale, w, cos, sin)


def _flash_kernel(*refs, seg_chunks, tq, n_sub):
    q_ref, sg_ref = refs[0], refs[1]
    kv_refs = refs[2:-1]
    t_ref = refs[-1]
    dh = ATT_HEAD_DIM
    grp = ATT_GROUP
    chunks = []
    for si, tk in enumerate(seg_chunks):
        k_ref, vt_ref = kv_refs[2 * si], kv_refs[2 * si + 1]
        chunks += [(k_ref, vt_ref, c * tk, tk) for c in range(k_ref.shape[1] // tk)]
    units = [(sb, ci, i) for sb in range(n_sub) for ci in range(len(chunks)) for i in range(grp)]

    def scores(sb, ci, i):
        k_ref, _, start, tk = chunks[ci]
        q = q_ref[0, sb * tq:(sb + 1) * tq, i * dh:(i + 1) * dh]
        return lax.dot_general(k_ref[0, start:start + tk, :], q, (((1,), (1,)), ((), ())),
                               preferred_element_type=_F32)

    m, acc = {}, {}
    pending = [scores(*u) for u in units[:FLASH_PREFETCH]]
    for n, (sb, ci, i) in enumerate(units):
        _, vt_ref, start, tk = chunks[ci]
        s = pending.pop(0)
        if n + FLASH_PREFETCH < len(units):
            pending.append(scores(*units[n + FLASH_PREFETCH]))
        if ci == 0:
            m[sb, i] = jnp.full((1, tq), -jnp.inf, _F32)
            acc[sb, i] = jnp.zeros((dh + ONES_ROWS, tq), _F32)
        m_new = jnp.maximum(m[sb, i], jnp.max(s, axis=0, keepdims=True))
        alpha = jnp.exp2(m[sb, i] - m_new)
        p = jnp.exp2(s - m_new).astype(_BF16)
        acc[sb, i] = alpha * acc[sb, i] + jnp.dot(vt_ref[0, 0, :, start:start + tk], p,
                                                  preferred_element_type=_F32)
        m[sb, i] = m_new
        if ci == len(chunks) - 1:
            a = acc.pop((sb, i))
            o_t = a[:dh] * (1.0 / a[dh:dh + 1])
            gate = sg_ref[0, sb * tq:(sb + 1) * tq, i * dh:(i + 1) * dh].astype(_F32)
            t_ref[0, sb * tq:(sb + 1) * tq, i * dh:(i + 1) * dh] = (o_t.T * gate).astype(_BF16)


def _flash(q, sg, kv_segments, tq, n_sub=1):
    b, nq, qw = q.shape
    dh = ATT_HEAD_DIM
    gw = ATT_GROUP * dh
    qmap = lambda bi, hi, i: (bi, i, hi)
    kmap = lambda bi, hi, i: (bi, 0, hi)
    vmap = lambda bi, hi, i: (bi, hi, 0, 0)
    tstep = tq * n_sub
    in_specs = [pl.BlockSpec((1, tstep, gw), qmap), pl.BlockSpec((1, tstep, gw), qmap)]
    args = [q, sg]
    for k, vt, _ in kv_segments:
        in_specs += [pl.BlockSpec((1, k.shape[1], dh), kmap),
                     pl.BlockSpec((1, 1, dh + ONES_ROWS, vt.shape[3]), vmap)]
        args += [k, vt]
    return pl.pallas_call(
        functools.partial(_flash_kernel, seg_chunks=tuple(tk for _, _, tk in kv_segments), tq=tq, n_sub=n_sub),
        grid=(b, ATT_KV_HEADS, nq // tstep),
        in_specs=in_specs,
        out_specs=pl.BlockSpec((1, tstep, gw), qmap),
        out_shape=jax.ShapeDtypeStruct((b, nq, qw), _BF16),
        compiler_params=_params(3),
        name="flash_attention",
    )(*args)


def _retention_kernel(lgf_ref, lgb_ref, qc_ref, kc_ref, vc_ref, sgc_ref, ql_ref, kl_ref, vl_ref, sgl_ref,
                      tc_ref, tl_ref, state_f, state_b, fc_ref, fl_ref, *, chunk, unroll):
    hd = pl.program_id(1)
    c = chunk
    n_ctx = qc_ref.shape[1] // c
    n_lat = ql_ref.shape[1] // c
    ii = lax.broadcasted_iota(jnp.int32, (c, c), 0)
    jj = lax.broadcasted_iota(jnp.int32, (c, c), 1)
    row = lax.broadcasted_iota(jnp.int32, (c, 1), 0).astype(_F32)

    def tables(lg, forward):
        diff = (ii - jj) if forward else (jj - ii)
        keep = (diff >= 0) if forward else (diff > 0)
        decay = jnp.where(keep, jnp.exp(lg * jnp.maximum(diff, 0).astype(_F32)), 0.0)
        if forward:
            xi = jnp.exp(lg * (row + 1.0))
            zeta = jnp.exp(lg * (c - 1.0 - row))
        else:
            xi = jnp.exp(lg * (c - row))
            zeta = jnp.exp(lg * row)
        g_chunk = jnp.exp(jnp.full((1, 1), lg * c, _F32))
        return decay, xi, zeta, g_chunk

    def step(q_ref, k_ref, v_ref, sg_ref, f_ref, t_ref, start, tabs, state_ref, final):
        decay, xi, zeta, g_chunk = tabs
        sl = pl.ds(start, c)
        q = q_ref[0, sl, :]
        k = k_ref[0, sl, :]
        v = v_ref[0, sl, :]
        state = state_ref[...]
        s = lax.dot_general(q, k, (((1,), (1,)), ((), ())), preferred_element_type=_F32) * decay
        o = jnp.dot(s.astype(_BF16), v, preferred_element_type=_F32)
        o = o + jnp.dot(q, state.astype(_BF16), preferred_element_type=_F32) * xi
        kz = (k.astype(_F32) * zeta).astype(_BF16)
        upd = lax.dot_general(kz, v, (((0,), (0,)), ((), ())), preferred_element_type=_F32)
        state_ref[...] = state * g_chunk + upd
        if not final:
            f_ref[sl, :] = o
        else:
            o = o + f_ref[sl, :]
            mu = jnp.mean(o, axis=-1, keepdims=True)
            oc = o - mu
            var = jnp.mean(oc * oc, axis=-1, keepdims=True)
            on = (oc * lax.rsqrt(var + GN_EPS)).astype(_BF16)
            t_ref[0, sl, :] = on * sg_ref[0, sl, :]

    tabs_f = tables(lgf_ref[hd], True)
    tabs_b = tables(lgb_ref[hd], False)
    state_f[...] = jnp.zeros_like(state_f)
    state_b[...] = jnp.zeros_like(state_b)
    for i in range(n_ctx):
        step(qc_ref, kc_ref, vc_ref, sgc_ref, fc_ref, tc_ref, i * c, tabs_f, state_f, False)
    for i in reversed(range(n_ctx)):
        step(qc_ref, kc_ref, vc_ref, sgc_ref, fc_ref, tc_ref, i * c, tabs_b, state_b, True)

    half = n_lat // 2

    def lat_body(final):
        def body(i, carry):
            lo = pl.multiple_of(i * c, c)
            hi = pl.multiple_of((n_lat - 1 - i) * c, c)
            step(ql_ref, kl_ref, vl_ref, sgl_ref, fl_ref, tl_ref, lo, tabs_f, state_f, final)
            step(ql_ref, kl_ref, vl_ref, sgl_ref, fl_ref, tl_ref, hi, tabs_b, state_b, final)
            return carry
        return body

    lax.fori_loop(0, half, lat_body(False), 0, unroll=unroll)
    lax.fori_loop(half, n_lat, lat_body(True), 0, unroll=unroll)


def _retention(lg_f, lg_b, qk_c, v_c, sg_c, qk_l, v_l, sg_l, chunk):
    b, n_lat, _ = qk_l.shape
    n_ctx = qk_c.shape[1]
    assert (n_lat // chunk) % 2 == 0, "the two scan directions meet in the middle of the latents"
    dk, dv, nh = RET_QK_DIM, RET_V_DIM, RET_HEADS
    qmap = lambda bi, hi: (bi, 0, hi)
    kmap = lambda bi, hi: (bi, 0, nh + hi)
    smem = pl.BlockSpec(memory_space=pltpu.SMEM)
    unroll = math.gcd(n_lat // chunk // 2, 4)
    return pl.pallas_call(
        functools.partial(_retention_kernel, chunk=chunk, unroll=unroll),
        grid=(b, nh),
        in_specs=[smem, smem,
                  pl.BlockSpec((1, n_ctx, dk), qmap), pl.BlockSpec((1, n_ctx, dk), kmap),
                  pl.BlockSpec((1, n_ctx, dv), qmap), pl.BlockSpec((1, n_ctx, dv), qmap),
                  pl.BlockSpec((1, n_lat, dk), qmap), pl.BlockSpec((1, n_lat, dk), kmap),
                  pl.BlockSpec((1, n_lat, dv), qmap), pl.BlockSpec((1, n_lat, dv), qmap)],
        out_specs=[pl.BlockSpec((1, n_ctx, dv), qmap), pl.BlockSpec((1, n_lat, dv), qmap)],
        out_shape=[jax.ShapeDtypeStruct((b, n_ctx, nh * dv), _BF16),
                   jax.ShapeDtypeStruct((b, n_lat, nh * dv), _BF16)],
        scratch_shapes=[pltpu.VMEM((dk, dv), _F32),
                        pltpu.VMEM((dk, dv), _F32),
                        pltpu.VMEM((n_ctx, dv), _F32),
                        pltpu.VMEM((n_lat, dv), _F32)],
        compiler_params=_params(2),
        name="retention",
    )(lg_f, lg_b, qk_c, qk_c, v_c, sg_c, qk_l, qk_l, v_l, sg_l)


def _outproj_kernel(t_ref, x_ref, w_ref, gate_ref, lng_ref, lnb_ref, o_ref, *, eps, sub):
    n_sub = t_ref.shape[1] // sub
    proj = lambda r: jnp.dot(t_ref[0, r * sub:(r + 1) * sub, :], w_ref[...], preferred_element_type=_F32)
    y_next = proj(0)
    for r in range(n_sub):
        y = y_next
        if r + 1 < n_sub:
            y_next = proj(r + 1)
        rows = slice(r * sub, (r + 1) * sub)
        z = x_ref[0, rows, :] + gate_ref[0] * y
        mu = jnp.mean(z, axis=-1, keepdims=True)
        zc = z - mu
        var = jnp.mean(zc * zc, axis=-1, keepdims=True)
        o_ref[0, rows, :] = zc * lax.rsqrt(var + eps) * lng_ref[...] + lnb_ref[...]


def _outproj(t, x, w, layer, gate, ln_g, ln_b, alpha, tm):
    b, n, d = x.shape
    row = lambda bi, i: (bi, i, 0)
    const = lambda bi, i: (0, 0)
    return pl.pallas_call(
        functools.partial(_outproj_kernel, eps=LN_EPS / alpha ** 2, sub=math.gcd(tm, OUT_SUB_ROWS)),
        grid=(b, n // tm),
        in_specs=[pl.BlockSpec((1, tm, t.shape[2]), row),
                  pl.BlockSpec((1, tm, d), row),
                  pl.BlockSpec((None,) + w.shape[1:], lambda bi, i: (layer, 0, 0)),
                  pl.BlockSpec((1, 1, d), lambda bi, i: (bi, 0, 0)),
                  pl.BlockSpec((1, d), const),
                  pl.BlockSpec((1, d), const)],
        out_specs=pl.BlockSpec((1, tm, d), row),
        out_shape=jax.ShapeDtypeStruct((b, n, d), _F32),
        compiler_params=_params(2),
        name="outproj",
    )(t, x, w, gate, ln_g, ln_b)


def _axial_perm():
    quarter = ATT_HEAD_DIM // 4
    return np.concatenate([np.arange(quarter) + off * quarter for off in (0, 2, 1, 3)])


def _axial_tables(s):
    quarter = ATT_HEAD_DIM // 4
    t = np.arange(s)
    pos = np.stack([t // GRID_W, t % GRID_W], axis=1).astype(np.float32)
    freqs = (ROPE_THETA ** (-np.arange(quarter, dtype=np.float32) / quarter)).astype(np.float32)
    ang = (pos[:, :, None] * freqs[None, None, :]).reshape(s, 2 * quarter)
    cos = np.concatenate([np.cos(ang), np.cos(ang)], axis=-1)
    sin = np.concatenate([-np.sin(ang), np.sin(ang)], axis=-1)
    return cos.astype(np.float32), sin.astype(np.float32)


def _permute_qk_columns(w_in):
    dh = ATT_HEAD_DIM
    qw = ATT_HEADS * dh
    kw = ATT_KV_HEADS * dh
    perm = _axial_perm()
    q_cols = (jnp.arange(ATT_HEADS)[:, None] * dh + perm[None, :]).reshape(-1)
    k_cols = 2 * qw + (jnp.arange(ATT_KV_HEADS)[:, None] * dh + perm[None, :]).reshape(-1)
    cols = jnp.concatenate([q_cols, jnp.arange(qw, 2 * qw), k_cols, jnp.arange(2 * qw + kw, 2 * qw + 2 * kw)])
    return w_in[..., cols]


def _rope_tables(pos):
    half = RET_QK_DIM // 2
    freqs = (ROPE_THETA ** (-np.arange(half, dtype=np.float32) / half)).astype(np.float32)
    ang = pos.astype(np.float32)[:, None] * freqs[None, :]
    return np.cos(ang).astype(np.float32), np.sin(ang).astype(np.float32)


def _row_tile(n, want):
    return want if n % want == 0 else n


def kernel(x, c, ctx, c_ctx, mod_w, mod_b, ln_g, ln_b, attn_w_in, attn_w_out, attn_q_scale, attn_k_scale,
           ret_w_in, ret_w_out, ret_gn_g, ret_log_decay_fwd, ret_log_decay_bwd):
    b, s, d = x.shape
    l = ctx.shape[1]
    depth = mod_w.shape[0]
    alpha = (2.0 * depth) ** 0.25

    rows = 8 * ((b + 1 + 7) // 8)
    cvec = jnp.zeros((rows, d), _F32).at[:b].set(c).at[b].set(c_ctx)
    mods = _modulation(cvec, mod_w, mod_b, alpha)

    cos_ax, sin_ax = _axial_tables(s)
    cos_id, sin_id = np.ones((l, ATT_HEAD_DIM), np.float32), np.zeros((l, ATT_HEAD_DIM), np.float32)
    cos_c, sin_c = _rope_tables(np.arange(l))
    cos_l, sin_l = _rope_tables(l + np.arange(s))
    attn_w_in_b = _permute_qk_columns(attn_w_in).astype(_BF16)
    attn_w_out_b = attn_w_out.astype(_BF16)
    ret_w_in_b = ret_w_in.astype(_BF16)
    ret_w_out_b = (ret_gn_g[:, :, None] * ret_w_out).astype(_BF16)

    tm_l = _row_tile(s, 512)
    tm_ret = _row_tile(s, 1024)
    tm_c = _row_tile(l, 256)
    tm_out = _row_tile(s, 1024)
    tq = _row_tile(s, 256)
    tk = _row_tile(s, 512)
    chunk = RET_CHUNK if (l % RET_CHUNK == 0 and s % RET_CHUNK == 0) else 128

    for i in range(depth):
        need_ctx = i < depth - 1
        j = i // 2
        shift_l = mods[i, :b, None, :d]
        scale_l = mods[i, :b, None, d:2 * d]
        gate_l = mods[i, :b, None, 2 * d:]
        shift_c = jnp.broadcast_to(mods[i, b, None, None, :d], (b, 1, d))
        scale_c = jnp.broadcast_to(mods[i, b, None, None, d:2 * d], (b, 1, d))
        gate_c = jnp.broadcast_to(mods[i, b, None, None, 2 * d:], (b, 1, d))
        lng = ln_g[i][None, :]
        lnb = ln_b[i][None, :]
        if i % 2 == 0:
            w_out = attn_w_out_b
            qs = attn_q_scale[j][_axial_perm()][None, :]
            ks = attn_k_scale[j][_axial_perm()][None, :]
            q_l, k_l, vt_l, sg_l = _attn_inproj(x, shift_l, scale_l, attn_w_in_b, j, qs, ks, cos_ax, sin_ax, tm_l)
            q_c, k_c, vt_c, sg_c = _attn_inproj(ctx, shift_c, scale_c, attn_w_in_b, j, qs, ks, cos_id, sin_id, tm_c)
            t_l = _flash(q_l, sg_l, [(k_l, vt_l, tk), (k_c, vt_c, l)], tq,
                         FLASH_SUB_BLOCKS if s % (tq * FLASH_SUB_BLOCKS) == 0 else 1)
            if need_ctx:
                t_c = _flash(q_c, sg_c, [(k_c, vt_c, l)], _row_tile(l, 128))
        else:
            w_out = ret_w_out_b
            qk_l, v_l, sg_l = _ret_inproj(x, shift_l, scale_l, ret_w_in_b, j, cos_l, sin_l, tm_ret)
            qk_c, v_c, sg_c = _ret_inproj(ctx, shift_c, scale_c, ret_w_in_b, j, cos_c, sin_c, tm_c)
            t_c, t_l = _retention(ret_log_decay_fwd[j], ret_log_decay_bwd[j],
                                  qk_c, v_c, sg_c, qk_l, v_l, sg_l, chunk)
        x = _outproj(t_l, x, w_out, j, gate_l, lng, lnb, alpha, tm_out)
        if need_ctx:
            ctx = _outproj(t_c, ctx, w_out, j, gate_c, lng, lnb, alpha, tm_c)
    return x
```

```python
import functools
import math

import jax
import jax.numpy as jnp
import numpy as np
from jax import lax
from jax.experimental import pallas as pl
from jax.experimental.pallas import tpu as pltpu

GRID_W = 64
ROPE_THETA = 10000.0

ATT_HEADS = 8
ATT_KV_HEADS = 2
ATT_GROUP = ATT_HEADS // ATT_KV_HEADS
ATT_HEAD_DIM = 128
ONES_ROWS = 16

RET_HEADS = 4
RET_QK_DIM = 256
RET_V_DIM = 512
RET_CHUNK = 256

LN_EPS = 1e-5
QK_EPS = 1e-6
GN_EPS = 1e-5

MXU_COLS = 256
DOT_COLS = 2 * MXU_COLS
FLASH_PREFETCH = 4
FLASH_SUB_BLOCKS = 2
OUT_SUB_ROWS = 256
VMEM_LIMIT = 56 * 1024 * 1024

_BF16 = jnp.bfloat16
_F32 = jnp.float32


def _params(n_grid):
    return pltpu.CompilerParams(dimension_semantics=("arbitrary",) * n_grid,
                                vmem_limit_bytes=VMEM_LIMIT)


def _silu(g):
    return g * jax.nn.sigmoid(g)


def _mod_kernel(c_ref, w_ref, b_ref, o_ref, *, gate_mul):
    sc = _silu(c_ref[...])
    out = jnp.dot(sc, w_ref[0], preferred_element_type=_F32) + b_ref[0]
    o_ref[0] = out * jnp.where(pl.program_id(1) == 2, gate_mul, 1.0)


def _modulation(cvec, mod_w, mod_b, alpha):
    depth, d, d3 = mod_w.shape
    r = cvec.shape[0]
    return pl.pallas_call(
        functools.partial(_mod_kernel, gate_mul=1.0 / alpha),
        grid=(depth, d3 // d),
        in_specs=[pl.BlockSpec((r, d), lambda i, j: (0, 0)),
                  pl.BlockSpec((1, d, d), lambda i, j: (i, 0, j)),
                  pl.BlockSpec((1, 1, d), lambda i, j: (i, 0, j))],
        out_specs=pl.BlockSpec((1, r, d), lambda i, j: (i, 0, j)),
        out_shape=jax.ShapeDtypeStruct((depth, r, d3), _F32),
        compiler_params=_params(2),
        name="modulation",
    )(cvec, mod_w, mod_b.reshape(depth, 1, d3))


def _modulated(x_ref, shift_ref, scale_ref):
    return (x_ref[0] * (1.0 + scale_ref[0]) + shift_ref[0]).astype(_BF16)


def _slab_dot(h, w_ref, start):
    return jnp.dot(h, w_ref[:, start:start + MXU_COLS], preferred_element_type=_F32)


def _attn_inproj_kernel(x_ref, shift_ref, scale_ref, w_ref, qs_ref, ks_ref, cos_ref, sin_ref,
                        q_ref, k_ref, vt_ref, sg_ref, *, q_premul):
    h = _modulated(x_ref, shift_ref, scale_ref)
    dh = ATT_HEAD_DIM
    qw = ATT_HEADS * dh
    kw = ATT_KV_HEADS * dh
    cos = cos_ref[...]
    sin = sin_ref[...]

    def norm_rope_heads(o_ref, scale_row, mul, p):
        ts = [p[:, u * dh:(u + 1) * dh] for u in range(p.shape[1] // dh)]
        ms = [jnp.mean(t * t, axis=-1, keepdims=True) for t in ts]
        rs = [lax.rsqrt(m + QK_EPS) for m in ms]
        ts = [t * r * scale_row for t, r in zip(ts, rs)]
        rolled = [pltpu.roll(t, dh // 2, axis=1) for t in ts]
        ts = [t * cos + r * sin for t, r in zip(ts, rolled)]
        for hd, t in enumerate(ts):
            o_ref[0, :, hd * dh:(hd + 1) * dh] = (t if mul == 1.0 else t * mul).astype(_BF16)

    dot = lambda start, width: jnp.dot(h, w_ref[:, start:start + width], preferred_element_type=_F32)
    p = dot(2 * qw, 2 * kw)
    norm_rope_heads(k_ref, ks_ref[...], 1.0, p[:, :kw])
    for hd in range(ATT_KV_HEADS):
        vt_ref[0, hd, :dh, :] = p[:, kw + hd * dh:kw + (hd + 1) * dh].T.astype(_BF16)
        vt_ref[0, hd, dh:, :] = jnp.ones((ONES_ROWS, p.shape[0]), _BF16)
    norm_rope_heads(q_ref, qs_ref[...], q_premul, dot(0, qw))
    sg_ref[0] = _silu(dot(qw, qw)).astype(_BF16)


def _attn_inproj(x, shift, scale, w, layer, q_scale, k_scale, cos, sin, tm):
    b, n, d = x.shape
    qw = ATT_HEADS * ATT_HEAD_DIM
    kw = ATT_KV_HEADS * ATT_HEAD_DIM
    row = lambda bi, i: (bi, i, 0)
    vec = lambda bi, i: (bi, 0, 0)
    const = lambda bi, i: (0, 0)
    return pl.pallas_call(
        functools.partial(_attn_inproj_kernel, q_premul=ATT_HEAD_DIM ** -0.5 * math.log2(math.e)),
        grid=(b, n // tm),
        in_specs=[pl.BlockSpec((1, tm, d), row),
                  pl.BlockSpec((1, 1, d), vec),
                  pl.BlockSpec((1, 1, d), vec),
                  pl.BlockSpec((None,) + w.shape[1:], lambda bi, i: (layer, 0, 0)),
                  pl.BlockSpec((1, ATT_HEAD_DIM), const),
                  pl.BlockSpec((1, ATT_HEAD_DIM), const),
                  pl.BlockSpec((tm, ATT_HEAD_DIM), lambda bi, i: (i, 0)),
                  pl.BlockSpec((tm, ATT_HEAD_DIM), lambda bi, i: (i, 0))],
        out_specs=[pl.BlockSpec((1, tm, qw), row),
                   pl.BlockSpec((1, tm, kw), row),
                   pl.BlockSpec((1, ATT_KV_HEADS, ATT_HEAD_DIM + ONES_ROWS, tm), lambda bi, i: (bi, 0, 0, i)),
                   pl.BlockSpec((1, tm, qw), row)],
        out_shape=[jax.ShapeDtypeStruct((b, n, qw), _BF16),
                   jax.ShapeDtypeStruct((b, n, kw), _BF16),
                   jax.ShapeDtypeStruct((b, ATT_KV_HEADS, ATT_HEAD_DIM + ONES_ROWS, n), _BF16),
                   jax.ShapeDtypeStruct((b, n, qw), _BF16)],
        compiler_params=_params(2),
        name="attn_inproj",
    )(x, shift, scale, w, q_scale, k_scale, cos, sin)


def _ret_inproj_kernel(x_ref, shift_ref, scale_ref, w_ref, cos_ref, sin_ref, qk_ref, v_ref, sg_ref):
    h = _modulated(x_ref, shift_ref, scale_ref)
    dk = RET_QK_DIM
    half = dk // 2
    qkw = 2 * RET_HEADS * dk
    vw = RET_HEADS * RET_V_DIM
    cos = cos_ref[...]
    sin = sin_ref[...]
    for hd in range(2 * RET_HEADS):
        p = jnp.dot(h, w_ref[:, hd * dk:(hd + 1) * dk], preferred_element_type=_F32)
        x1, x2 = p[:, :half], p[:, half:]
        o1 = x1 * cos - x2 * sin
        o2 = x1 * sin + x2 * cos
        if hd >= RET_HEADS:
            o1 = o1 * dk ** -0.5
            o2 = o2 * dk ** -0.5
        qk_ref[0, :, hd * dk:hd * dk + half] = o1.astype(_BF16)
        qk_ref[0, :, hd * dk + half:(hd + 1) * dk] = o2.astype(_BF16)
    for j in range(vw // MXU_COLS):
        v_ref[0, :, j * MXU_COLS:(j + 1) * MXU_COLS] = _slab_dot(h, w_ref, qkw + j * MXU_COLS).astype(_BF16)
    for j in range(vw // MXU_COLS):
        g = _slab_dot(h, w_ref, qkw + vw + j * MXU_COLS)
        sg_ref[0, :, j * MXU_COLS:(j + 1) * MXU_COLS] = _silu(g).astype(_BF16)


def _ret_inproj(x, shift, scale, w, layer, cos, sin, tm):
    b, n, d = x.shape
    qkw = 2 * RET_HEADS * RET_QK_DIM
    vw = RET_HEADS * RET_V_DIM
    row = lambda bi, i: (bi, i, 0)
    vec = lambda bi, i: (bi, 0, 0)
    half = RET_QK_DIM // 2
    return pl.pallas_call(
        _ret_inproj_kernel,
        grid=(b, n // tm),
        in_specs=[pl.BlockSpec((1, tm, d), row),
                  pl.BlockSpec((1, 1, d), vec),
                  pl.BlockSpec((1, 1, d), vec),
                  pl.BlockSpec((None,) + w.shape[1:], lambda bi, i: (layer, 0, 0), pipeline_mode=pl.Buffered(1)),
                  pl.BlockSpec((tm, half), lambda bi, i: (i, 0)),
                  pl.BlockSpec((tm, half), lambda bi, i: (i, 0))],
        out_specs=[pl.BlockSpec((1, tm, qkw), row),
                   pl.BlockSpec((1, tm, vw), row),
                   pl.BlockSpec((1, tm, vw), row)],
        out_shape=[jax.ShapeDtypeStruct((b, n, qkw), _BF16),
                   jax.ShapeDtypeStruct((b, n, vw), _BF16),
                   jax.ShapeDtypeStruct((b, n, vw), _BF16)],
        compiler_params=_params(2),
        name="ret_inproj",
    )(x, shift, scale, w, cos, sin)


def _flash_kernel(*refs, seg_chunks, tq, n_sub):
    q_ref, sg_ref = refs[0], refs[1]
    kv_refs = refs[2:-2]
    t_ref, s_scr = refs[-2:]
    dh = ATT_HEAD_DIM
    grp = ATT_GROUP
    rows = tq * n_sub
    n_blk = q_ref.shape[1] // rows
    chunks = []
    for si, tk in enumerate(seg_chunks):
        k_ref, vt_ref = kv_refs[2 * si], kv_refs[2 * si + 1]
        chunks += [(k_ref, vt_ref, c * tk, tk) for c in range(k_ref.shape[1] // tk)]
    units = [(sb, ci, i) for sb in range(n_sub) for ci in range(len(chunks)) for i in range(grp)]

    def scores(row0, sb, ci, i):
        k_ref, _, start, tk = chunks[ci]
        q = q_ref[0, pl.ds(row0 + sb * tq, tq), i * dh:(i + 1) * dh]
        return lax.dot_general(k_ref[0, start:start + tk, :], q, (((1,), (1,)), ((), ())),
                               preferred_element_type=_F32)

    for slot in range(FLASH_PREFETCH):
        s_scr[slot] = scores(0, *units[slot])

    def block(blk, carry):
        row0 = pl.multiple_of(blk * rows, rows)
        nxt0 = pl.multiple_of(jnp.minimum(blk + 1, n_blk - 1) * rows, rows)
        pending = [s_scr[slot] for slot in range(FLASH_PREFETCH)]
        m, acc = {}, {}
        for n, (sb, ci, i) in enumerate(units):
            _, vt_ref, start, tk = chunks[ci]
            s = pending.pop(0)
            if n + FLASH_PREFETCH < len(units):
                pending.append(scores(row0, *units[n + FLASH_PREFETCH]))
            else:
                pending.append(scores(nxt0, *units[n + FLASH_PREFETCH - len(units)]))
            if ci == 0:
                m[sb, i] = jnp.full((1, tq), -jnp.inf, _F32)
                acc[sb, i] = jnp.zeros((dh + ONES_ROWS, tq), _F32)
            m_new = jnp.maximum(m[sb, i], jnp.max(s, axis=0, keepdims=True))
            alpha = jnp.exp2(m[sb, i] - m_new)
            p = jnp.exp2(s - m_new).astype(_BF16)
            acc[sb, i] = alpha * acc[sb, i] + jnp.dot(vt_ref[0, 0, :, start:start + tk], p,
                                                      preferred_element_type=_F32)
            m[sb, i] = m_new
            if ci == len(chunks) - 1:
                a = acc.pop((sb, i))
                o_t = a[:dh] * (1.0 / a[dh:dh + 1])
                out_rows = pl.ds(row0 + sb * tq, tq)
                gate = sg_ref[0, out_rows, i * dh:(i + 1) * dh].astype(_F32)
                t_ref[0, out_rows, i * dh:(i + 1) * dh] = (o_t.T * gate).astype(_BF16)
        for slot in range(FLASH_PREFETCH):
            s_scr[slot] = pending[slot]
        return carry

    lax.fori_loop(0, n_blk, block, 0)


def _flash(q, sg, kv_segments, tq, n_sub):
    b, nq, qw = q.shape
    dh = ATT_HEAD_DIM
    gw = ATT_GROUP * dh
    assert nq % (tq * n_sub) == 0 and FLASH_PREFETCH <= ATT_GROUP
    qmap = lambda bi, hi: (bi, 0, hi)
    vmap = lambda bi, hi: (bi, hi, 0, 0)
    in_specs = [pl.BlockSpec((1, nq, gw), qmap), pl.BlockSpec((1, nq, gw), qmap)]
    args = [q, sg]
    for k, vt, _ in kv_segments:
        in_specs += [pl.BlockSpec((1, k.shape[1], dh), qmap),
                     pl.BlockSpec((1, 1, dh + ONES_ROWS, vt.shape[3]), vmap)]
        args += [k, vt]
    return pl.pallas_call(
        functools.partial(_flash_kernel, seg_chunks=tuple(tk for _, _, tk in kv_segments), tq=tq, n_sub=n_sub),
        grid=(b, ATT_KV_HEADS),
        in_specs=in_specs,
        out_specs=pl.BlockSpec((1, nq, gw), qmap),
        out_shape=jax.ShapeDtypeStruct((b, nq, qw), _BF16),
        scratch_shapes=[pltpu.VMEM((FLASH_PREFETCH, kv_segments[0][2], tq), _F32)],
        compiler_params=_params(2),
        name="flash_attention",
    )(*args)


def _retention_kernel(lgf_ref, lgb_ref, qc_ref, kc_ref, vc_ref, sgc_ref, ql_ref, kl_ref, vl_ref, sgl_ref,
                      tc_ref, tl_ref, state_f, state_b, fc_ref, fl_ref, *, chunk, unroll):
    hd = pl.program_id(1)
    c = chunk
    n_ctx = qc_ref.shape[1] // c
    n_lat = ql_ref.shape[1] // c
    ii = lax.broadcasted_iota(jnp.int32, (c, c), 0)
    jj = lax.broadcasted_iota(jnp.int32, (c, c), 1)
    row = lax.broadcasted_iota(jnp.int32, (c, 1), 0).astype(_F32)

    def tables(lg, forward):
        diff = (ii - jj) if forward else (jj - ii)
        keep = (diff >= 0) if forward else (diff > 0)
        decay = jnp.where(keep, jnp.exp(lg * jnp.maximum(diff, 0).astype(_F32)), 0.0)
        if forward:
            xi = jnp.exp(lg * (row + 1.0))
            zeta = jnp.exp(lg * (c - 1.0 - row))
        else:
            xi = jnp.exp(lg * (c - row))
            zeta = jnp.exp(lg * row)
        g_chunk = jnp.exp(jnp.full((1, 1), lg * c, _F32))
        return decay, xi, zeta, g_chunk

    def step(q_ref, k_ref, v_ref, sg_ref, f_ref, t_ref, start, tabs, state_ref, final):
        decay, xi, zeta, g_chunk = tabs
        sl = pl.ds(start, c)
        q = q_ref[0, sl, :]
        k = k_ref[0, sl, :]
        v = v_ref[0, sl, :]
        state = state_ref[...]
        s = lax.dot_general(q, k, (((1,), (1,)), ((), ())), preferred_element_type=_F32) * decay
        o = jnp.dot(s.astype(_BF16), v, preferred_element_type=_F32)
        o = o + jnp.dot(q, state.astype(_BF16), preferred_element_type=_F32) * xi
        kz = (k.astype(_F32) * zeta).astype(_BF16)
        upd = lax.dot_general(kz, v, (((0,), (0,)), ((), ())), preferred_element_type=_F32)
        state_ref[...] = state * g_chunk + upd
        if not final:
            f_ref[sl, :] = o
        else:
            o = o + f_ref[sl, :]
            mu = jnp.mean(o, axis=-1, keepdims=True)
            oc = o - mu
            var = jnp.mean(oc * oc, axis=-1, keepdims=True)
            on = (oc * lax.rsqrt(var + GN_EPS)).astype(_BF16)
            t_ref[0, sl, :] = on * sg_ref[0, sl, :]

    tabs_f = tables(lgf_ref[hd], True)
    tabs_b = tables(lgb_ref[hd], False)
    state_f[...] = jnp.zeros_like(state_f)
    state_b[...] = jnp.zeros_like(state_b)
    for i in range(n_ctx):
        step(qc_ref, kc_ref, vc_ref, sgc_ref, fc_ref, tc_ref, i * c, tabs_f, state_f, False)
    for i in reversed(range(n_ctx)):
        step(qc_ref, kc_ref, vc_ref, sgc_ref, fc_ref, tc_ref, i * c, tabs_b, state_b, True)

    half = n_lat // 2

    def lat_body(final):
        def body(i, carry):
            lo = pl.multiple_of(i * c, c)
            hi = pl.multiple_of((n_lat - 1 - i) * c, c)
            step(ql_ref, kl_ref, vl_ref, sgl_ref, fl_ref, tl_ref, lo, tabs_f, state_f, final)
            step(ql_ref, kl_ref, vl_ref, sgl_ref, fl_ref, tl_ref, hi, tabs_b, state_b, final)
            return carry
        return body

    lax.fori_loop(0, half, lat_body(False), 0, unroll=unroll)
    lax.fori_loop(half, n_lat, lat_body(True), 0, unroll=unroll)


def _retention(lg_f, lg_b, qk_c, v_c, sg_c, qk_l, v_l, sg_l, chunk):
    b, n_lat, _ = qk_l.shape
    n_ctx = qk_c.shape[1]
    assert (n_lat // chunk) % 2 == 0, "the two scan directions meet in the middle of the latents"
    dk, dv, nh = RET_QK_DIM, RET_V_DIM, RET_HEADS
    qmap = lambda bi, hi: (bi, 0, hi)
    kmap = lambda bi, hi: (bi, 0, nh + hi)
    smem = pl.BlockSpec(memory_space=pltpu.SMEM)
    unroll = math.gcd(n_lat // chunk // 2, 4)
    return pl.pallas_call(
        functools.partial(_retention_kernel, chunk=chunk, unroll=unroll),
        grid=(b, nh),
        in_specs=[smem, smem,
                  pl.BlockSpec((1, n_ctx, dk), qmap), pl.BlockSpec((1, n_ctx, dk), kmap),
                  pl.BlockSpec((1, n_ctx, dv), qmap), pl.BlockSpec((1, n_ctx, dv), qmap),
                  pl.BlockSpec((1, n_lat, dk), qmap), pl.BlockSpec((1, n_lat, dk), kmap),
                  pl.BlockSpec((1, n_lat, dv), qmap), pl.BlockSpec((1, n_lat, dv), qmap)],
        out_specs=[pl.BlockSpec((1, n_ctx, dv), qmap), pl.BlockSpec((1, n_lat, dv), qmap)],
        out_shape=[jax.ShapeDtypeStruct((b, n_ctx, nh * dv), _BF16),
                   jax.ShapeDtypeStruct((b, n_lat, nh * dv), _BF16)],
        scratch_shapes=[pltpu.VMEM((dk, dv), _F32),
                        pltpu.VMEM((dk, dv), _F32),
                        pltpu.VMEM((n_ctx, dv), _F32),
                        pltpu.VMEM((n_lat, dv), _F32)],
        compiler_params=_params(2),
        name="retention",
    )(lg_f, lg_b, qk_c, qk_c, v_c, sg_c, qk_l, qk_l, v_l, sg_l)


def _outproj_kernel(t_ref, x_ref, w_ref, gate_ref, lng_ref, lnb_ref, o_ref, *, eps, sub):
    n_sub = t_ref.shape[1] // sub
    proj = lambda r: jnp.dot(t_ref[0, r * sub:(r + 1) * sub, :], w_ref[...], preferred_element_type=_F32)
    y_next = proj(0)
    for r in range(n_sub):
        y = y_next
        if r + 1 < n_sub:
            y_next = proj(r + 1)
        rows = slice(r * sub, (r + 1) * sub)
        z = x_ref[0, rows, :] + gate_ref[0] * y
        mu = jnp.mean(z, axis=-1, keepdims=True)
        zc = z - mu
        var = jnp.mean(zc * zc, axis=-1, keepdims=True)
        o_ref[0, rows, :] = zc * lax.rsqrt(var + eps) * lng_ref[...] + lnb_ref[...]


def _outproj(t, x, w, layer, gate, ln_g, ln_b, alpha, tm):
    b, n, d = x.shape
    row = lambda bi, i: (bi, i, 0)
    const = lambda bi, i: (0, 0)
    return pl.pallas_call(
        functools.partial(_outproj_kernel, eps=LN_EPS / alpha ** 2, sub=math.gcd(tm, OUT_SUB_ROWS)),
        grid=(b, n // tm),
        in_specs=[pl.BlockSpec((1, tm, t.shape[2]), row),
                  pl.BlockSpec((1, tm, d), row),
                  pl.BlockSpec((None,) + w.shape[1:], lambda bi, i: (layer, 0, 0)),
                  pl.BlockSpec((1, 1, d), lambda bi, i: (bi, 0, 0)),
                  pl.BlockSpec((1, d), const),
                  pl.BlockSpec((1, d), const)],
        out_specs=pl.BlockSpec((1, tm, d), row),
        out_shape=jax.ShapeDtypeStruct((b, n, d), _F32),
        compiler_params=_params(2),
        name="outproj",
    )(t, x, w, gate, ln_g, ln_b)


def _axial_perm():
    quarter = ATT_HEAD_DIM // 4
    return np.concatenate([np.arange(quarter) + off * quarter for off in (0, 2, 1, 3)])


def _axial_tables(s):
    quarter = ATT_HEAD_DIM // 4
    t = np.arange(s)
    pos = np.stack([t // GRID_W, t % GRID_W], axis=1).astype(np.float32)
    freqs = (ROPE_THETA ** (-np.arange(quarter, dtype=np.float32) / quarter)).astype(np.float32)
    ang = (pos[:, :, None] * freqs[None, None, :]).reshape(s, 2 * quarter)
    cos = np.concatenate([np.cos(ang), np.cos(ang)], axis=-1)
    sin = np.concatenate([-np.sin(ang), np.sin(ang)], axis=-1)
    return cos.astype(np.float32), sin.astype(np.float32)


def _permute_qk_columns(w_in):
    dh = ATT_HEAD_DIM
    qw = ATT_HEADS * dh
    kw = ATT_KV_HEADS * dh
    perm = _axial_perm()
    q_cols = (jnp.arange(ATT_HEADS)[:, None] * dh + perm[None, :]).reshape(-1)
    k_cols = 2 * qw + (jnp.arange(ATT_KV_HEADS)[:, None] * dh + perm[None, :]).reshape(-1)
    cols = jnp.concatenate([q_cols, jnp.arange(qw, 2 * qw), k_cols, jnp.arange(2 * qw + kw, 2 * qw + 2 * kw)])
    return w_in[..., cols]


def _rope_tables(pos):
    half = RET_QK_DIM // 2
    freqs = (ROPE_THETA ** (-np.arange(half, dtype=np.float32) / half)).astype(np.float32)
    ang = pos.astype(np.float32)[:, None] * freqs[None, :]
    return np.cos(ang).astype(np.float32), np.sin(ang).astype(np.float32)


def _row_tile(n, want):
    return want if n % want == 0 else n


def kernel(x, c, ctx, c_ctx, mod_w, mod_b, ln_g, ln_b, attn_w_in, attn_w_out, attn_q_scale, attn_k_scale,
           ret_w_in, ret_w_out, ret_gn_g, ret_log_decay_fwd, ret_log_decay_bwd):
    b, s, d = x.shape
    l = ctx.shape[1]
    depth = mod_w.shape[0]
    alpha = (2.0 * depth) ** 0.25

    rows = 8 * ((b + 1 + 7) // 8)
    cvec = jnp.zeros((rows, d), _F32).at[:b].set(c).at[b].set(c_ctx)
    mods = _modulation(cvec, mod_w, mod_b, alpha)

    cos_ax, sin_ax = _axial_tables(s)
    cos_id, sin_id = np.ones((l, ATT_HEAD_DIM), np.float32), np.zeros((l, ATT_HEAD_DIM), np.float32)
    cos_c, sin_c = _rope_tables(np.arange(l))
    cos_l, sin_l = _rope_tables(l + np.arange(s))
    attn_w_out_b = attn_w_out.astype(_BF16)
    ret_w_in_b = ret_w_in.astype(_BF16)
    ret_w_out_b = (ret_gn_g[:, :, None] * ret_w_out).astype(_BF16)

    tm_l = _row_tile(s, 512)
    tm_ret = _row_tile(s, 1024)
    tm_c = _row_tile(l, 256)
    tm_out = _row_tile(s, 1024)
    tq = _row_tile(s, 256)
    tk = _row_tile(s, 512)
    chunk = RET_CHUNK if (l % RET_CHUNK == 0 and s % RET_CHUNK == 0) else 128

    for i in range(depth):
        need_ctx = i < depth - 1
        j = i // 2
        shift_l = mods[i, :b, None, :d]
        scale_l = mods[i, :b, None, d:2 * d]
        gate_l = mods[i, :b, None, 2 * d:]
        shift_c = jnp.broadcast_to(mods[i, b, None, None, :d], (b, 1, d))
        scale_c = jnp.broadcast_to(mods[i, b, None, None, d:2 * d], (b, 1, d))
        gate_c = jnp.broadcast_to(mods[i, b, None, None, 2 * d:], (b, 1, d))
        lng = ln_g[i][None, :]
        lnb = ln_b[i][None, :]
        if i % 2 == 0:
            w_out = attn_w_out_b
            w_in = _permute_qk_columns(attn_w_in[j]).astype(_BF16)[None]
            qs = attn_q_scale[j][_axial_perm()][None, :]
            ks = attn_k_scale[j][_axial_perm()][None, :]
            q_l, k_l, vt_l, sg_l = _attn_inproj(x, shift_l, scale_l, w_in, 0, qs, ks, cos_ax, sin_ax, tm_l)
            q_c, k_c, vt_c, sg_c = _attn_inproj(ctx, shift_c, scale_c, w_in, 0, qs, ks, cos_id, sin_id, tm_c)
            t_l = _flash(q_l, sg_l, [(k_l, vt_l, tk), (k_c, vt_c, l)], tq,
                         FLASH_SUB_BLOCKS if s % (tq * FLASH_SUB_BLOCKS) == 0 else 1)
            if need_ctx:
                t_c = _flash(q_c, sg_c, [(k_c, vt_c, l)], _row_tile(l, 128), 1)
        else:
            w_out = ret_w_out_b
            qk_l, v_l, sg_l = _ret_inproj(x, shift_l, scale_l, ret_w_in_b, j, cos_l, sin_l, tm_ret)
            qk_c, v_c, sg_c = _ret_inproj(ctx, shift_c, scale_c, ret_w_in_b, j, cos_c, sin_c, tm_c)
            t_c, t_l = _retention(ret_log_decay_fwd[j], ret_log_decay_bwd[j],
                                  qk_c, v_c, sg_c, qk_l, v_l, sg_l, chunk)
        x = _outproj(t_l, x, w_out, j, gate_l, lng, lnb, alpha, tm_out)
        if need_ctx:
            ctx = _outproj(t_c, ctx, w_out, j, gate_c, lng, lnb, alpha, tm_c)
    return x
```

```python
import functools
import math

import jax
import jax.numpy as jnp
import numpy as np
from jax import lax
from jax.experimental import pallas as pl
from jax.experimental.pallas import tpu as pltpu

GRID_W = 64
ROPE_THETA = 10000.0

ATT_HEADS = 8
ATT_KV_HEADS = 2
ATT_GROUP = ATT_HEADS // ATT_KV_HEADS
ATT_HEAD_DIM = 128
ONES_ROWS = 16

RET_HEADS = 4
RET_QK_DIM = 256
RET_V_DIM = 512
RET_CHUNK = 256

LN_EPS = 1e-5
QK_EPS = 1e-6
GN_EPS = 1e-5

MXU_COLS = 256
DOT_COLS = 2 * MXU_COLS
FLASH_PREFETCH = 4
FLASH_SUB_BLOCKS = 2
OUT_SUB_ROWS = 256
VMEM_LIMIT = 56 * 1024 * 1024

_BF16 = jnp.bfloat16
_F32 = jnp.float32


def _params(n_grid):
    return pltpu.CompilerParams(dimension_semantics=("arbitrary",) * n_grid,
                                vmem_limit_bytes=VMEM_LIMIT)


def _silu(g):
    return g * jax.nn.sigmoid(g)


def _mod_kernel(c_ref, w_ref, b_ref, o_ref, *, gate_mul):
    sc = _silu(c_ref[...])
    out = jnp.dot(sc, w_ref[0], preferred_element_type=_F32) + b_ref[0]
    o_ref[0] = out * jnp.where(pl.program_id(1) == 2, gate_mul, 1.0)


def _modulation(cvec, mod_w, mod_b, alpha):
    depth, d, d3 = mod_w.shape
    r = cvec.shape[0]
    return pl.pallas_call(
        functools.partial(_mod_kernel, gate_mul=1.0 / alpha),
        grid=(depth, d3 // d),
        in_specs=[pl.BlockSpec((r, d), lambda i, j: (0, 0)),
                  pl.BlockSpec((1, d, d), lambda i, j: (i, 0, j)),
                  pl.BlockSpec((1, 1, d), lambda i, j: (i, 0, j))],
        out_specs=pl.BlockSpec((1, r, d), lambda i, j: (i, 0, j)),
        out_shape=jax.ShapeDtypeStruct((depth, r, d3), _F32),
        compiler_params=_params(2),
        name="modulation",
    )(cvec, mod_w, mod_b.reshape(depth, 1, d3))


def _modulated(x_ref, shift_ref, scale_ref):
    return (x_ref[0] * (1.0 + scale_ref[0]) + shift_ref[0]).astype(_BF16)


def _slab_dot(h, w_ref, start):
    return jnp.dot(h, w_ref[:, start:start + MXU_COLS], preferred_element_type=_F32)


def _attn_inproj_kernel(x_ref, shift_ref, scale_ref, w_ref, qs_ref, ks_ref, cos_ref, sin_ref,
                        q_ref, k_ref, vt_ref, sg_ref, *, q_premul):
    h = _modulated(x_ref, shift_ref, scale_ref)
    dh = ATT_HEAD_DIM
    qw = ATT_HEADS * dh
    kw = ATT_KV_HEADS * dh
    cos = cos_ref[...]
    sin = sin_ref[...]

    def norm_rope_heads(o_ref, scale_row, mul, p):
        ts = [p[:, u * dh:(u + 1) * dh] for u in range(p.shape[1] // dh)]
        ms = [jnp.mean(t * t, axis=-1, keepdims=True) for t in ts]
        rs = [lax.rsqrt(m + QK_EPS) for m in ms]
        ts = [t * r * scale_row for t, r in zip(ts, rs)]
        rolled = [pltpu.roll(t, dh // 2, axis=1) for t in ts]
        ts = [t * cos + r * sin for t, r in zip(ts, rolled)]
        for hd, t in enumerate(ts):
            o_ref[0, :, hd * dh:(hd + 1) * dh] = (t if mul == 1.0 else t * mul).astype(_BF16)

    dot = lambda start, width: jnp.dot(h, w_ref[:, start:start + width], preferred_element_type=_F32)
    p = dot(2 * qw, 2 * kw)
    norm_rope_heads(k_ref, ks_ref[...], 1.0, p[:, :kw])
    for hd in range(ATT_KV_HEADS):
        vt_ref[0, hd, :dh, :] = p[:, kw + hd * dh:kw + (hd + 1) * dh].T.astype(_BF16)
        vt_ref[0, hd, dh:, :] = jnp.ones((ONES_ROWS, p.shape[0]), _BF16)
    norm_rope_heads(q_ref, qs_ref[...], q_premul, dot(0, qw))
    sg_ref[0] = _silu(dot(qw, qw)).astype(_BF16)


def _attn_inproj(x, shift, scale, w, layer, q_scale, k_scale, cos, sin, tm):
    b, n, d = x.shape
    qw = ATT_HEADS * ATT_HEAD_DIM
    kw = ATT_KV_HEADS * ATT_HEAD_DIM
    row = lambda bi, i: (bi, i, 0)
    vec = lambda bi, i: (bi, 0, 0)
    const = lambda bi, i: (0, 0)
    return pl.pallas_call(
        functools.partial(_attn_inproj_kernel, q_premul=ATT_HEAD_DIM ** -0.5 * math.log2(math.e)),
        grid=(b, n // tm),
        in_specs=[pl.BlockSpec((1, tm, d), row),
                  pl.BlockSpec((1, 1, d), vec),
                  pl.BlockSpec((1, 1, d), vec),
                  pl.BlockSpec((None,) + w.shape[1:], lambda bi, i: (layer, 0, 0)),
                  pl.BlockSpec((1, ATT_HEAD_DIM), const),
                  pl.BlockSpec((1, ATT_HEAD_DIM), const),
                  pl.BlockSpec((tm, ATT_HEAD_DIM), lambda bi, i: (i, 0)),
                  pl.BlockSpec((tm, ATT_HEAD_DIM), lambda bi, i: (i, 0))],
        out_specs=[pl.BlockSpec((1, tm, qw), row),
                   pl.BlockSpec((1, tm, kw), row),
                   pl.BlockSpec((1, ATT_KV_HEADS, ATT_HEAD_DIM + ONES_ROWS, tm), lambda bi, i: (bi, 0, 0, i)),
                   pl.BlockSpec((1, tm, qw), row)],
        out_shape=[jax.ShapeDtypeStruct((b, n, qw), _BF16),
                   jax.ShapeDtypeStruct((b, n, kw), _BF16),
                   jax.ShapeDtypeStruct((b, ATT_KV_HEADS, ATT_HEAD_DIM + ONES_ROWS, n), _BF16),
                   jax.ShapeDtypeStruct((b, n, qw), _BF16)],
        compiler_params=_params(2),
        name="attn_inproj",
    )(x, shift, scale, w, q_scale, k_scale, cos, sin)


def _ret_inproj_kernel(x_ref, shift_ref, scale_ref, w_ref, cos_ref, sin_ref, qk_ref, v_ref, sg_ref):
    h = _modulated(x_ref, shift_ref, scale_ref)
    dk = RET_QK_DIM
    half = dk // 2
    qkw = 2 * RET_HEADS * dk
    vw = RET_HEADS * RET_V_DIM
    cos = cos_ref[...]
    sin = sin_ref[...]
    for hd in range(2 * RET_HEADS):
        p = jnp.dot(h, w_ref[:, hd * dk:(hd + 1) * dk], preferred_element_type=_F32)
        x1, x2 = p[:, :half], p[:, half:]
        o1 = x1 * cos - x2 * sin
        o2 = x1 * sin + x2 * cos
        if hd >= RET_HEADS:
            o1 = o1 * dk ** -0.5
            o2 = o2 * dk ** -0.5
        qk_ref[0, :, hd * dk:hd * dk + half] = o1.astype(_BF16)
        qk_ref[0, :, hd * dk + half:(hd + 1) * dk] = o2.astype(_BF16)
    for j in range(vw // MXU_COLS):
        v_ref[0, :, j * MXU_COLS:(j + 1) * MXU_COLS] = _slab_dot(h, w_ref, qkw + j * MXU_COLS).astype(_BF16)
    for j in range(vw // MXU_COLS):
        g = _slab_dot(h, w_ref, qkw + vw + j * MXU_COLS)
        sg_ref[0, :, j * MXU_COLS:(j + 1) * MXU_COLS] = _silu(g).astype(_BF16)


def _ret_inproj(x, shift, scale, w, layer, cos, sin, tm):
    b, n, d = x.shape
    qkw = 2 * RET_HEADS * RET_QK_DIM
    vw = RET_HEADS * RET_V_DIM
    row = lambda bi, i: (bi, i, 0)
    vec = lambda bi, i: (bi, 0, 0)
    half = RET_QK_DIM // 2
    return pl.pallas_call(
        _ret_inproj_kernel,
        grid=(b, n // tm),
        in_specs=[pl.BlockSpec((1, tm, d), row),
                  pl.BlockSpec((1, 1, d), vec),
                  pl.BlockSpec((1, 1, d), vec),
                  pl.BlockSpec((None,) + w.shape[1:], lambda bi, i: (layer, 0, 0), pipeline_mode=pl.Buffered(1)),
                  pl.BlockSpec((tm, half), lambda bi, i: (i, 0)),
                  pl.BlockSpec((tm, half), lambda bi, i: (i, 0))],
        out_specs=[pl.BlockSpec((1, tm, qkw), row),
                   pl.BlockSpec((1, tm, vw), row),
                   pl.BlockSpec((1, tm, vw), row)],
        out_shape=[jax.ShapeDtypeStruct((b, n, qkw), _BF16),
                   jax.ShapeDtypeStruct((b, n, vw), _BF16),
                   jax.ShapeDtypeStruct((b, n, vw), _BF16)],
        compiler_params=_params(2),
        name="ret_inproj",
    )(x, shift, scale, w, cos, sin)


def _flash_kernel(*refs, seg_chunks, tq, n_sub):
    q_ref, sg_ref = refs[0], refs[1]
    kv_refs = refs[2:-1]
    t_ref = refs[-1]
    dh = ATT_HEAD_DIM
    grp = ATT_GROUP
    chunks = []
    for si, tk in enumerate(seg_chunks):
        k_ref, vt_ref = kv_refs[2 * si], kv_refs[2 * si + 1]
        chunks += [(k_ref, vt_ref, c * tk, tk) for c in range(k_ref.shape[1] // tk)]
    units = [(sb, ci, i) for ci in range(len(chunks)) for sb in range(n_sub) for i in range(grp)]

    def scores(sb, ci, i):
        k_ref, _, start, tk = chunks[ci]
        q = q_ref[0, sb * tq:(sb + 1) * tq, i * dh:(i + 1) * dh]
        return lax.dot_general(k_ref[0, start:start + tk, :], q, (((1,), (1,)), ((), ())),
                               preferred_element_type=_F32)

    m, acc = {}, {}
    pending = [scores(*u) for u in units[:FLASH_PREFETCH]]
    for n, (sb, ci, i) in enumerate(units):
        _, vt_ref, start, tk = chunks[ci]
        s = pending.pop(0)
        if n + FLASH_PREFETCH < len(units):
            pending.append(scores(*units[n + FLASH_PREFETCH]))
        if ci == 0:
            m[sb, i] = jnp.full((1, tq), -jnp.inf, _F32)
            acc[sb, i] = jnp.zeros((dh + ONES_ROWS, tq), _F32)
        m_new = jnp.maximum(m[sb, i], jnp.max(s, axis=0, keepdims=True))
        alpha = jnp.exp2(m[sb, i] - m_new)
        p = jnp.exp2(s - m_new).astype(_BF16)
        acc[sb, i] = alpha * acc[sb, i] + jnp.dot(vt_ref[0, 0, :, start:start + tk], p,
                                                  preferred_element_type=_F32)
        m[sb, i] = m_new
        if ci == len(chunks) - 1:
            a = acc.pop((sb, i))
            o_t = a[:dh] * (1.0 / a[dh:dh + 1])
            gate = sg_ref[0, sb * tq:(sb + 1) * tq, i * dh:(i + 1) * dh].astype(_F32)
            t_ref[0, sb * tq:(sb + 1) * tq, i * dh:(i + 1) * dh] = (o_t.T * gate).astype(_BF16)


def _flash(q, sg, kv_segments, tq, n_sub=1):
    b, nq, qw = q.shape
    dh = ATT_HEAD_DIM
    gw = ATT_GROUP * dh
    qmap = lambda bi, hi, i: (bi, i, hi)
    kmap = lambda bi, hi, i: (bi, 0, hi)
    vmap = lambda bi, hi, i: (bi, hi, 0, 0)
    tstep = tq * n_sub
    in_specs = [pl.BlockSpec((1, tstep, gw), qmap), pl.BlockSpec((1, tstep, gw), qmap)]
    args = [q, sg]
    for k, vt, _ in kv_segments:
        in_specs += [pl.BlockSpec((1, k.shape[1], dh), kmap),
                     pl.BlockSpec((1, 1, dh + ONES_ROWS, vt.shape[3]), vmap)]
        args += [k, vt]
    return pl.pallas_call(
        functools.partial(_flash_kernel, seg_chunks=tuple(tk for _, _, tk in kv_segments), tq=tq, n_sub=n_sub),
        grid=(b, ATT_KV_HEADS, nq // tstep),
        in_specs=in_specs,
        out_specs=pl.BlockSpec((1, tstep, gw), qmap),
        out_shape=jax.ShapeDtypeStruct((b, nq, qw), _BF16),
        compiler_params=_params(3),
        name="flash_attention",
    )(*args)


def _retention_kernel(lgf_ref, lgb_ref, qc_ref, kc_ref, vc_ref, sgc_ref, ql_ref, kl_ref, vl_ref, sgl_ref,
                      tc_ref, tl_ref, state_f, state_b, fc_ref, fl_ref, *, chunk, unroll):
    hd = pl.program_id(1)
    c = chunk
    n_ctx = qc_ref.shape[1] // c
    n_lat = ql_ref.shape[1] // c
    ii = lax.broadcasted_iota(jnp.int32, (c, c), 0)
    jj = lax.broadcasted_iota(jnp.int32, (c, c), 1)
    row = lax.broadcasted_iota(jnp.int32, (c, 1), 0).astype(_F32)

    def tables(lg, forward):
        diff = (ii - jj) if forward else (jj - ii)
        keep = (diff >= 0) if forward else (diff > 0)
        decay = jnp.where(keep, jnp.exp(lg * jnp.maximum(diff, 0).astype(_F32)), 0.0)
        if forward:
            xi = jnp.exp(lg * (row + 1.0))
            zeta = jnp.exp(lg * (c - 1.0 - row))
        else:
            xi = jnp.exp(lg * (c - row))
            zeta = jnp.exp(lg * row)
        g_chunk = jnp.exp(jnp.full((1, 1), lg * c, _F32))
        return decay, xi, zeta, g_chunk

    def step(q_ref, k_ref, v_ref, sg_ref, f_ref, t_ref, start, tabs, state_ref, final):
        decay, xi, zeta, g_chunk = tabs
        sl = pl.ds(start, c)
        q = q_ref[0, sl, :]
        k = k_ref[0, sl, :]
        v = v_ref[0, sl, :]
        state = state_ref[...]
        s = lax.dot_general(q, k, (((1,), (1,)), ((), ())), preferred_element_type=_F32) * decay
        o = jnp.dot(s.astype(_BF16), v, preferred_element_type=_F32)
        o = o + jnp.dot(q, state.astype(_BF16), preferred_element_type=_F32) * xi
        kz = (k.astype(_F32) * zeta).astype(_BF16)
        upd = lax.dot_general(kz, v, (((0,), (0,)), ((), ())), preferred_element_type=_F32)
        state_ref[...] = state * g_chunk + upd
        if not final:
            f_ref[sl, :] = o
        else:
            o = o + f_ref[sl, :]
            mu = jnp.mean(o, axis=-1, keepdims=True)
            oc = o - mu
            var = jnp.mean(oc * oc, axis=-1, keepdims=True)
            on = (oc * lax.rsqrt(var + GN_EPS)).astype(_BF16)
            t_ref[0, sl, :] = on * sg_ref[0, sl, :]

    tabs_f = tables(lgf_ref[hd], True)
    tabs_b = tables(lgb_ref[hd], False)
    state_f[...] = jnp.zeros_like(state_f)
    state_b[...] = jnp.zeros_like(state_b)
    for i in range(n_ctx):
        step(qc_ref, kc_ref, vc_ref, sgc_ref, fc_ref, tc_ref, i * c, tabs_f, state_f, False)
    for i in reversed(range(n_ctx)):
        step(qc_ref, kc_ref, vc_ref, sgc_ref, fc_ref, tc_ref, i * c, tabs_b, state_b, True)

    half = n_lat // 2

    def lat_body(final):
        def body(i, carry):
            lo = pl.multiple_of(i * c, c)
            hi = pl.multiple_of((n_lat - 1 - i) * c, c)
            step(ql_ref, kl_ref, vl_ref, sgl_ref, fl_ref, tl_ref, lo, tabs_f, state_f, final)
            step(ql_ref, kl_ref, vl_ref, sgl_ref, fl_ref, tl_ref, hi, tabs_b, state_b, final)
            return carry
        return body

    lax.fori_loop(0, half, lat_body(False), 0, unroll=unroll)
    lax.fori_loop(half, n_lat, lat_body(True), 0, unroll=unroll)


def _retention(lg_f, lg_b, qk_c, v_c, sg_c, qk_l, v_l, sg_l, chunk):
    b, n_lat, _ = qk_l.shape
    n_ctx = qk_c.shape[1]
    assert (n_lat // chunk) % 2 == 0, "the two scan directions meet in the middle of the latents"
    dk, dv, nh = RET_QK_DIM, RET_V_DIM, RET_HEADS
    qmap = lambda bi, hi: (bi, 0, hi)
    kmap = lambda bi, hi: (bi, 0, nh + hi)
    smem = pl.BlockSpec(memory_space=pltpu.SMEM)
    unroll = math.gcd(n_lat // chunk // 2, 4)
    return pl.pallas_call(
        functools.partial(_retention_kernel, chunk=chunk, unroll=unroll),
        grid=(b, nh),
        in_specs=[smem, smem,
                  pl.BlockSpec((1, n_ctx, dk), qmap), pl.BlockSpec((1, n_ctx, dk), kmap),
                  pl.BlockSpec((1, n_ctx, dv), qmap), pl.BlockSpec((1, n_ctx, dv), qmap),
                  pl.BlockSpec((1, n_lat, dk), qmap), pl.BlockSpec((1, n_lat, dk), kmap),
                  pl.BlockSpec((1, n_lat, dv), qmap), pl.BlockSpec((1, n_lat, dv), qmap)],
        out_specs=[pl.BlockSpec((1, n_ctx, dv), qmap), pl.BlockSpec((1, n_lat, dv), qmap)],
        out_shape=[jax.ShapeDtypeStruct((b, n_ctx, nh * dv), _BF16),
                   jax.ShapeDtypeStruct((b, n_lat, nh * dv), _BF16)],
        scratch_shapes=[pltpu.VMEM((dk, dv), _F32),
                        pltpu.VMEM((dk, dv), _F32),
                        pltpu.VMEM((n_ctx, dv), _F32),
                        pltpu.VMEM((n_lat, dv), _F32)],
        compiler_params=_params(2),
        name="retention",
    )(lg_f, lg_b, qk_c, qk_c, v_c, sg_c, qk_l, qk_l, v_l, sg_l)


def _outproj_kernel(t_ref, x_ref, w_ref, gate_ref, lng_ref, lnb_ref, o_ref, *, eps, sub):
    n_sub = t_ref.shape[1] // sub
    proj = lambda r: jnp.dot(t_ref[0, r * sub:(r + 1) * sub, :], w_ref[...], preferred_element_type=_F32)
    y_next = proj(0)
    for r in range(n_sub):
        y = y_next
        if r + 1 < n_sub:
            y_next = proj(r + 1)
        rows = slice(r * sub, (r + 1) * sub)
        z = x_ref[0, rows, :] + gate_ref[0] * y
        mu = jnp.mean(z, axis=-1, keepdims=True)
        zc = z - mu
        var = jnp.mean(zc * zc, axis=-1, keepdims=True)
        o_ref[0, rows, :] = zc * lax.rsqrt(var + eps) * lng_ref[...] + lnb_ref[...]


def _outproj(t, x, w, layer, gate, ln_g, ln_b, alpha, tm):
    b, n, d = x.shape
    row = lambda bi, i: (bi, i, 0)
    const = lambda bi, i: (0, 0)
    return pl.pallas_call(
        functools.partial(_outproj_kernel, eps=LN_EPS / alpha ** 2, sub=math.gcd(tm, OUT_SUB_ROWS)),
        grid=(b, n // tm),
        in_specs=[pl.BlockSpec((1, tm, t.shape[2]), row),
                  pl.BlockSpec((1, tm, d), row),
                  pl.BlockSpec((None,) + w.shape[1:], lambda bi, i: (layer, 0, 0)),
                  pl.BlockSpec((1, 1, d), lambda bi, i: (bi, 0, 0)),
                  pl.BlockSpec((1, d), const),
                  pl.BlockSpec((1, d), const)],
        out_specs=pl.BlockSpec((1, tm, d), row),
        out_shape=jax.ShapeDtypeStruct((b, n, d), _F32),
        compiler_params=_params(2),
        name="outproj",
    )(t, x, w, gate, ln_g, ln_b)


def _axial_perm():
    quarter = ATT_HEAD_DIM // 4
    return np.concatenate([np.arange(quarter) + off * quarter for off in (0, 2, 1, 3)])


def _axial_tables(s):
    quarter = ATT_HEAD_DIM // 4
    t = np.arange(s)
    pos = np.stack([t // GRID_W, t % GRID_W], axis=1).astype(np.float32)
    freqs = (ROPE_THETA ** (-np.arange(quarter, dtype=np.float32) / quarter)).astype(np.float32)
    ang = (pos[:, :, None] * freqs[None, None, :]).reshape(s, 2 * quarter)
    cos = np.concatenate([np.cos(ang), np.cos(ang)], axis=-1)
    sin = np.concatenate([-np.sin(ang), np.sin(ang)], axis=-1)
    return cos.astype(np.float32), sin.astype(np.float32)


def _permute_qk_columns(w_in):
    dh = ATT_HEAD_DIM
    qw = ATT_HEADS * dh
    kw = ATT_KV_HEADS * dh
    perm = _axial_perm()
    q_cols = (jnp.arange(ATT_HEADS)[:, None] * dh + perm[None, :]).reshape(-1)
    k_cols = 2 * qw + (jnp.arange(ATT_KV_HEADS)[:, None] * dh + perm[None, :]).reshape(-1)
    cols = jnp.concatenate([q_cols, jnp.arange(qw, 2 * qw), k_cols, jnp.arange(2 * qw + kw, 2 * qw + 2 * kw)])
    return w_in[..., cols]


def _rope_tables(pos):
    half = RET_QK_DIM // 2
    freqs = (ROPE_THETA ** (-np.arange(half, dtype=np.float32) / half)).astype(np.float32)
    ang = pos.astype(np.float32)[:, None] * freqs[None, :]
    return np.cos(ang).astype(np.float32), np.sin(ang).astype(np.float32)


def _row_tile(n, want):
    return want if n % want == 0 else n


def kernel(x, c, ctx, c_ctx, mod_w, mod_b, ln_g, ln_b, attn_w_in, attn_w_out, attn_q_scale, attn_k_scale,
           ret_w_in, ret_w_out, ret_gn_g, ret_log_decay_fwd, ret_log_decay_bwd):
    b, s, d = x.shape
    l = ctx.shape[1]
    depth = mod_w.shape[0]
    alpha = (2.0 * depth) ** 0.25

    rows = 8 * ((b + 1 + 7) // 8)
    cvec = jnp.zeros((rows, d), _F32).at[:b].set(c).at[b].set(c_ctx)
    mods = _modulation(cvec, mod_w, mod_b, alpha)

    cos_ax, sin_ax = _axial_tables(s)
    cos_id, sin_id = np.ones((l, ATT_HEAD_DIM), np.float32), np.zeros((l, ATT_HEAD_DIM), np.float32)
    cos_c, sin_c = _rope_tables(np.arange(l))
    cos_l, sin_l = _rope_tables(l + np.arange(s))
    attn_w_out_b = attn_w_out.astype(_BF16)
    ret_w_in_b = ret_w_in.astype(_BF16)
    ret_w_out_b = (ret_gn_g[:, :, None] * ret_w_out).astype(_BF16)

    tm_l = _row_tile(s, 512)
    tm_ret = _row_tile(s, 1024)
    tm_c = _row_tile(l, 256)
    tm_out = _row_tile(s, 1024)
    tq = _row_tile(s, 256)
    tk = _row_tile(s, 512)
    chunk = RET_CHUNK if (l % RET_CHUNK == 0 and s % RET_CHUNK == 0) else 128

    for i in range(depth):
        need_ctx = i < depth - 1
        j = i // 2
        shift_l = mods[i, :b, None, :d]
        scale_l = mods[i, :b, None, d:2 * d]
        gate_l = mods[i, :b, None, 2 * d:]
        shift_c = jnp.broadcast_to(mods[i, b, None, None, :d], (b, 1, d))
        scale_c = jnp.broadcast_to(mods[i, b, None, None, d:2 * d], (b, 1, d))
        gate_c = jnp.broadcast_to(mods[i, b, None, None, 2 * d:], (b, 1, d))
        lng = ln_g[i][None, :]
        lnb = ln_b[i][None, :]
        if i % 2 == 0:
            w_out = attn_w_out_b
            w_in = _permute_qk_columns(attn_w_in[j]).astype(_BF16)[None]
            qs = attn_q_scale[j][_axial_perm()][None, :]
            ks = attn_k_scale[j][_axial_perm()][None, :]
            q_l, k_l, vt_l, sg_l = _attn_inproj(x, shift_l, scale_l, w_in, 0, qs, ks, cos_ax, sin_ax, tm_l)
            q_c, k_c, vt_c, sg_c = _attn_inproj(ctx, shift_c, scale_c, w_in, 0, qs, ks, cos_id, sin_id, tm_c)
            t_l = _flash(q_l, sg_l, [(k_l, vt_l, tk), (k_c, vt_c, l)], tq,
                         FLASH_SUB_BLOCKS if s % (tq * FLASH_SUB_BLOCKS) == 0 else 1)
            if need_ctx:
                t_c = _flash(q_c, sg_c, [(k_c, vt_c, l)], _row_tile(l, 128))
        else:
            w_out = ret_w_out_b
            qk_l, v_l, sg_l = _ret_inproj(x, shift_l, scale_l, ret_w_in_b, j, cos_l, sin_l, tm_ret)
            qk_c, v_c, sg_c = _ret_inproj(ctx, shift_c, scale_c, ret_w_in_b, j, cos_c, sin_c, tm_c)
            t_c, t_l = _retention(ret_log_decay_fwd[j], ret_log_decay_bwd[j],
                                  qk_c, v_c, sg_c, qk_l, v_l, sg_l, chunk)
        x = _outproj(t_l, x, w_out, j, gate_l, lng, lnb, alpha, tm_out)
        if need_ctx:
            ctx = _outproj(t_c, ctx, w_out, j, gate_c, lng, lnb, alpha, tm_c)
    return x
```

```python
import functools
import math

import jax
import jax.numpy as jnp
import numpy as np
from jax import lax
from jax.experimental import pallas as pl
from jax.experimental.pallas import tpu as pltpu

GRID_W = 64
ROPE_THETA = 10000.0

ATT_HEADS = 8
ATT_KV_HEADS = 2
ATT_GROUP = ATT_HEADS // ATT_KV_HEADS
ATT_HEAD_DIM = 128
ONES_ROWS = 16

RET_HEADS = 4
RET_QK_DIM = 256
RET_V_DIM = 512
RET_CHUNK = 256

LN_EPS = 1e-5
QK_EPS = 1e-6
GN_EPS = 1e-5

MXU_COLS = 256
DOT_COLS = 2 * MXU_COLS
FLASH_PREFETCH = 4
FLASH_SUB_BLOCKS = 2
OUT_SUB_ROWS = 256
VMEM_LIMIT = 56 * 1024 * 1024

_BF16 = jnp.bfloat16
_F32 = jnp.float32


def _params(n_grid):
    return pltpu.CompilerParams(dimension_semantics=("arbitrary",) * n_grid,
                                vmem_limit_bytes=VMEM_LIMIT)


def _silu(g):
    return g * jax.nn.sigmoid(g)


def _mod_kernel(c_ref, w_ref, b_ref, o_ref, *, gate_mul):
    sc = _silu(c_ref[...])
    out = jnp.dot(sc, w_ref[0], preferred_element_type=_F32) + b_ref[0]
    o_ref[0] = out * jnp.where(pl.program_id(1) == 2, gate_mul, 1.0)


def _modulation(cvec, mod_w, mod_b, alpha):
    depth, d, d3 = mod_w.shape
    r = cvec.shape[0]
    return pl.pallas_call(
        functools.partial(_mod_kernel, gate_mul=1.0 / alpha),
        grid=(depth, d3 // d),
        in_specs=[pl.BlockSpec((r, d), lambda i, j: (0, 0)),
                  pl.BlockSpec((1, d, d), lambda i, j: (i, 0, j)),
                  pl.BlockSpec((1, 1, d), lambda i, j: (i, 0, j))],
        out_specs=pl.BlockSpec((1, r, d), lambda i, j: (i, 0, j)),
        out_shape=jax.ShapeDtypeStruct((depth, r, d3), _F32),
        compiler_params=_params(2),
        name="modulation",
    )(cvec, mod_w, mod_b.reshape(depth, 1, d3))


def _modulated(x_ref, shift_ref, scale_ref):
    return (x_ref[0] * (1.0 + scale_ref[0]) + shift_ref[0]).astype(_BF16)


def _slab_dot(h, w_ref, start):
    return jnp.dot(h, w_ref[:, start:start + MXU_COLS], preferred_element_type=_F32)


def _attn_inproj_kernel(x_ref, shift_ref, scale_ref, w_ref, qs_ref, ks_ref, cos_ref, sin_ref,
                        q_ref, k_ref, vt_ref, sg_ref, *, q_premul):
    h = _modulated(x_ref, shift_ref, scale_ref)
    dh = ATT_HEAD_DIM
    qw = ATT_HEADS * dh
    kw = ATT_KV_HEADS * dh
    cos = cos_ref[...]
    sin = sin_ref[...]

    def norm_rope_heads(o_ref, scale_row, mul, p):
        ts = [p[:, u * dh:(u + 1) * dh] for u in range(p.shape[1] // dh)]
        ms = [jnp.mean(t * t, axis=-1, keepdims=True) for t in ts]
        rs = [lax.rsqrt(m + QK_EPS) for m in ms]
        ts = [t * r * scale_row for t, r in zip(ts, rs)]
        rolled = [pltpu.roll(t, dh // 2, axis=1) for t in ts]
        ts = [t * cos + r * sin for t, r in zip(ts, rolled)]
        for hd, t in enumerate(ts):
            o_ref[0, :, hd * dh:(hd + 1) * dh] = (t if mul == 1.0 else t * mul).astype(_BF16)

    dot = lambda start, width: jnp.dot(h, w_ref[:, start:start + width], preferred_element_type=_F32)
    p = dot(2 * qw, 2 * kw)
    norm_rope_heads(k_ref, ks_ref[...], 1.0, p[:, :kw])
    for hd in range(ATT_KV_HEADS):
        vt_ref[0, hd, :dh, :] = p[:, kw + hd * dh:kw + (hd + 1) * dh].T.astype(_BF16)
        vt_ref[0, hd, dh:, :] = jnp.ones((ONES_ROWS, p.shape[0]), _BF16)
    norm_rope_heads(q_ref, qs_ref[...], q_premul, dot(0, qw))
    sg_ref[0] = _silu(dot(qw, qw)).astype(_BF16)


def _attn_inproj(x, shift, scale, w, layer, q_scale, k_scale, cos, sin, tm):
    b, n, d = x.shape
    qw = ATT_HEADS * ATT_HEAD_DIM
    kw = ATT_KV_HEADS * ATT_HEAD_DIM
    row = lambda bi, i: (bi, i, 0)
    vec = lambda bi, i: (bi, 0, 0)
    const = lambda bi, i: (0, 0)
    return pl.pallas_call(
        functools.partial(_attn_inproj_kernel, q_premul=ATT_HEAD_DIM ** -0.5 * math.log2(math.e)),
        grid=(b, n // tm),
        in_specs=[pl.BlockSpec((1, tm, d), row),
                  pl.BlockSpec((1, 1, d), vec),
                  pl.BlockSpec((1, 1, d), vec),
                  pl.BlockSpec((None,) + w.shape[1:], lambda bi, i: (layer, 0, 0)),
                  pl.BlockSpec((1, ATT_HEAD_DIM), const),
                  pl.BlockSpec((1, ATT_HEAD_DIM), const),
                  pl.BlockSpec((tm, ATT_HEAD_DIM), lambda bi, i: (i, 0)),
                  pl.BlockSpec((tm, ATT_HEAD_DIM), lambda bi, i: (i, 0))],
        out_specs=[pl.BlockSpec((1, tm, qw), row),
                   pl.BlockSpec((1, tm, kw), row),
                   pl.BlockSpec((1, ATT_KV_HEADS, ATT_HEAD_DIM + ONES_ROWS, tm), lambda bi, i: (bi, 0, 0, i)),
                   pl.BlockSpec((1, tm, qw), row)],
        out_shape=[jax.ShapeDtypeStruct((b, n, qw), _BF16),
                   jax.ShapeDtypeStruct((b, n, kw), _BF16),
                   jax.ShapeDtypeStruct((b, ATT_KV_HEADS, ATT_HEAD_DIM + ONES_ROWS, n), _BF16),
                   jax.ShapeDtypeStruct((b, n, qw), _BF16)],
        compiler_params=_params(2),
        name="attn_inproj",
    )(x, shift, scale, w, q_scale, k_scale, cos, sin)


def _ret_inproj_kernel(x_ref, shift_ref, scale_ref, w_ref, cos_ref, sin_ref, q_ref, kt_ref, v_ref, sg_ref):
    h = _modulated(x_ref, shift_ref, scale_ref)
    dk = RET_QK_DIM
    half = dk // 2
    qkw = 2 * RET_HEADS * dk
    vw = RET_HEADS * RET_V_DIM
    cos = cos_ref[...]
    sin = sin_ref[...]
    for hd in range(2 * RET_HEADS):
        p = jnp.dot(h, w_ref[:, hd * dk:(hd + 1) * dk], preferred_element_type=_F32)
        x1, x2 = p[:, :half], p[:, half:]
        o1 = x1 * cos - x2 * sin
        o2 = x1 * sin + x2 * cos
        if hd < RET_HEADS:
            q_ref[0, :, hd * dk:hd * dk + half] = o1.astype(_BF16)
            q_ref[0, :, hd * dk + half:(hd + 1) * dk] = o2.astype(_BF16)
        else:
            kh = hd - RET_HEADS
            kt_ref[0, kh, :half, :] = (o1 * dk ** -0.5).T.astype(_BF16)
            kt_ref[0, kh, half:, :] = (o2 * dk ** -0.5).T.astype(_BF16)
    for j in range(vw // MXU_COLS):
        v_ref[0, :, j * MXU_COLS:(j + 1) * MXU_COLS] = _slab_dot(h, w_ref, qkw + j * MXU_COLS).astype(_BF16)
    for j in range(vw // MXU_COLS):
        g = _slab_dot(h, w_ref, qkw + vw + j * MXU_COLS)
        sg_ref[0, :, j * MXU_COLS:(j + 1) * MXU_COLS] = _silu(g).astype(_BF16)


def _ret_inproj(x, shift, scale, w, layer, cos, sin, tm):
    b, n, d = x.shape
    qkw = 2 * RET_HEADS * RET_QK_DIM
    vw = RET_HEADS * RET_V_DIM
    row = lambda bi, i: (bi, i, 0)
    vec = lambda bi, i: (bi, 0, 0)
    half = RET_QK_DIM // 2
    return pl.pallas_call(
        _ret_inproj_kernel,
        grid=(b, n // tm),
        in_specs=[pl.BlockSpec((1, tm, d), row),
                  pl.BlockSpec((1, 1, d), vec),
                  pl.BlockSpec((1, 1, d), vec),
                  pl.BlockSpec((None,) + w.shape[1:], lambda bi, i: (layer, 0, 0), pipeline_mode=pl.Buffered(1)),
                  pl.BlockSpec((tm, half), lambda bi, i: (i, 0)),
                  pl.BlockSpec((tm, half), lambda bi, i: (i, 0))],
        out_specs=[pl.BlockSpec((1, tm, qkw // 2), row),
                   pl.BlockSpec((1, RET_HEADS, RET_QK_DIM, tm), lambda bi, i: (bi, 0, 0, i)),
                   pl.BlockSpec((1, tm, vw), row),
                   pl.BlockSpec((1, tm, vw), row)],
        out_shape=[jax.ShapeDtypeStruct((b, n, qkw // 2), _BF16),
                   jax.ShapeDtypeStruct((b, RET_HEADS, RET_QK_DIM, n), _BF16),
                   jax.ShapeDtypeStruct((b, n, vw), _BF16),
                   jax.ShapeDtypeStruct((b, n, vw), _BF16)],
        compiler_params=_params(2),
        name="ret_inproj",
    )(x, shift, scale, w, cos, sin)


def _flash_kernel(*refs, seg_chunks, tq, n_sub):
    q_ref, sg_ref = refs[0], refs[1]
    kv_refs = refs[2:-1]
    t_ref = refs[-1]
    dh = ATT_HEAD_DIM
    grp = ATT_GROUP
    chunks = []
    for si, tk in enumerate(seg_chunks):
        k_ref, vt_ref = kv_refs[2 * si], kv_refs[2 * si + 1]
        chunks += [(k_ref, vt_ref, c * tk, tk) for c in range(k_ref.shape[1] // tk)]
    units = [(sb, ci, i) for ci in range(len(chunks)) for sb in range(n_sub) for i in range(grp)]

    def scores(sb, ci, i):
        k_ref, _, start, tk = chunks[ci]
        q = q_ref[0, sb * tq:(sb + 1) * tq, i * dh:(i + 1) * dh]
        return lax.dot_general(k_ref[0, start:start + tk, :], q, (((1,), (1,)), ((), ())),
                               preferred_element_type=_F32)

    m, acc = {}, {}
    pending = [scores(*u) for u in units[:FLASH_PREFETCH]]
    for n, (sb, ci, i) in enumerate(units):
        _, vt_ref, start, tk = chunks[ci]
        s = pending.pop(0)
        if n + FLASH_PREFETCH < len(units):
            pending.append(scores(*units[n + FLASH_PREFETCH]))
        if ci == 0:
            m[sb, i] = jnp.full((1, tq), -jnp.inf, _F32)
            acc[sb, i] = jnp.zeros((dh + ONES_ROWS, tq), _F32)
        m_new = jnp.maximum(m[sb, i], jnp.max(s, axis=0, keepdims=True))
        alpha = jnp.exp2(m[sb, i] - m_new)
        p = jnp.exp2(s - m_new).astype(_BF16)
        acc[sb, i] = alpha * acc[sb, i] + jnp.dot(vt_ref[0, 0, :, start:start + tk], p,
                                                  preferred_element_type=_F32)
        m[sb, i] = m_new
        if ci == len(chunks) - 1:
            a = acc.pop((sb, i))
            o_t = a[:dh] * (1.0 / a[dh:dh + 1])
            gate = sg_ref[0, sb * tq:(sb + 1) * tq, i * dh:(i + 1) * dh].astype(_F32)
            t_ref[0, sb * tq:(sb + 1) * tq, i * dh:(i + 1) * dh] = (o_t.T * gate).astype(_BF16)


def _flash(q, sg, kv_segments, tq, n_sub=1):
    b, nq, qw = q.shape
    dh = ATT_HEAD_DIM
    gw = ATT_GROUP * dh
    qmap = lambda bi, hi, i: (bi, i, hi)
    kmap = lambda bi, hi, i: (bi, 0, hi)
    vmap = lambda bi, hi, i: (bi, hi, 0, 0)
    tstep = tq * n_sub
    in_specs = [pl.BlockSpec((1, tstep, gw), qmap), pl.BlockSpec((1, tstep, gw), qmap)]
    args = [q, sg]
    for k, vt, _ in kv_segments:
        in_specs += [pl.BlockSpec((1, k.shape[1], dh), kmap),
                     pl.BlockSpec((1, 1, dh + ONES_ROWS, vt.shape[3]), vmap)]
        args += [k, vt]
    return pl.pallas_call(
        functools.partial(_flash_kernel, seg_chunks=tuple(tk for _, _, tk in kv_segments), tq=tq, n_sub=n_sub),
        grid=(b, ATT_KV_HEADS, nq // tstep),
        in_specs=in_specs,
        out_specs=pl.BlockSpec((1, tstep, gw), qmap),
        out_shape=jax.ShapeDtypeStruct((b, nq, qw), _BF16),
        compiler_params=_params(3),
        name="flash_attention",
    )(*args)


def _retention_kernel(lgf_ref, lgb_ref, qc_ref, kc_ref, vc_ref, sgc_ref, ql_ref, kl_ref, vl_ref, sgl_ref,
                      tc_ref, tl_ref, state_f, state_b, fc_ref, fl_ref, *, chunk, unroll):
    hd = pl.program_id(1)
    c = chunk
    n_ctx = qc_ref.shape[1] // c
    n_lat = ql_ref.shape[1] // c
    ii = lax.broadcasted_iota(jnp.int32, (c, c), 0)
    jj = lax.broadcasted_iota(jnp.int32, (c, c), 1)
    row = lax.broadcasted_iota(jnp.int32, (c, 1), 0).astype(_F32)
    col = lax.broadcasted_iota(jnp.int32, (1, c), 1).astype(_F32)

    def tables(lg, forward):
        diff = (ii - jj) if forward else (jj - ii)
        keep = (diff >= 0) if forward else (diff > 0)
        decay = jnp.where(keep, jnp.exp(lg * jnp.maximum(diff, 0).astype(_F32)), 0.0)
        if forward:
            xi = jnp.exp(lg * (row + 1.0))
            zeta = jnp.exp(lg * (c - 1.0 - col))
        else:
            xi = jnp.exp(lg * (c - row))
            zeta = jnp.exp(lg * col)
        g_chunk = jnp.exp(jnp.full((1, 1), lg * c, _F32))
        return decay, xi, zeta, g_chunk

    def step(q_ref, k_ref, v_ref, sg_ref, f_ref, t_ref, start, tabs, state_ref, final):
        decay, xi, zeta, g_chunk = tabs
        sl = pl.ds(start, c)
        q = q_ref[0, sl, :]
        kt = k_ref[0, 0, :, sl]
        v = v_ref[0, sl, :]
        state = state_ref[...]
        s = jnp.dot(q, kt, preferred_element_type=_F32) * decay
        o = jnp.dot(s.astype(_BF16), v, preferred_element_type=_F32)
        o = o + jnp.dot(q, state.astype(_BF16), preferred_element_type=_F32) * xi
        kz = (kt.astype(_F32) * zeta).astype(_BF16)
        upd = jnp.dot(kz, v, preferred_element_type=_F32)
        state_ref[...] = state * g_chunk + upd
        if not final:
            f_ref[sl, :] = o
        else:
            o = o + f_ref[sl, :]
            mu = jnp.mean(o, axis=-1, keepdims=True)
            oc = o - mu
            var = jnp.mean(oc * oc, axis=-1, keepdims=True)
            on = (oc * lax.rsqrt(var + GN_EPS)).astype(_BF16)
            t_ref[0, sl, :] = on * sg_ref[0, sl, :]

    tabs_f = tables(lgf_ref[hd], True)
    tabs_b = tables(lgb_ref[hd], False)
    state_f[...] = jnp.zeros_like(state_f)
    state_b[...] = jnp.zeros_like(state_b)
    for i in range(n_ctx):
        step(qc_ref, kc_ref, vc_ref, sgc_ref, fc_ref, tc_ref, i * c, tabs_f, state_f, False)
    for i in reversed(range(n_ctx)):
        step(qc_ref, kc_ref, vc_ref, sgc_ref, fc_ref, tc_ref, i * c, tabs_b, state_b, True)

    half = n_lat // 2

    def lat_body(final):
        def body(i, carry):
            lo = pl.multiple_of(i * c, c)
            hi = pl.multiple_of((n_lat - 1 - i) * c, c)
            step(ql_ref, kl_ref, vl_ref, sgl_ref, fl_ref, tl_ref, lo, tabs_f, state_f, final)
            step(ql_ref, kl_ref, vl_ref, sgl_ref, fl_ref, tl_ref, hi, tabs_b, state_b, final)
            return carry
        return body

    lax.fori_loop(0, half, lat_body(False), 0, unroll=unroll)
    lax.fori_loop(half, n_lat, lat_body(True), 0, unroll=unroll)


def _retention(lg_f, lg_b, q_c, kt_c, v_c, sg_c, q_l, kt_l, v_l, sg_l, chunk):
    b, n_lat, _ = q_l.shape
    n_ctx = q_c.shape[1]
    assert (n_lat // chunk) % 2 == 0, "the two scan directions meet in the middle of the latents"
    dk, dv, nh = RET_QK_DIM, RET_V_DIM, RET_HEADS
    qmap = lambda bi, hi: (bi, 0, hi)
    kmap = lambda bi, hi: (bi, hi, 0, 0)
    smem = pl.BlockSpec(memory_space=pltpu.SMEM)
    unroll = math.gcd(n_lat // chunk // 2, 4)
    return pl.pallas_call(
        functools.partial(_retention_kernel, chunk=chunk, unroll=unroll),
        grid=(b, nh),
        in_specs=[smem, smem,
                  pl.BlockSpec((1, n_ctx, dk), qmap), pl.BlockSpec((1, 1, dk, n_ctx), kmap),
                  pl.BlockSpec((1, n_ctx, dv), qmap), pl.BlockSpec((1, n_ctx, dv), qmap),
                  pl.BlockSpec((1, n_lat, dk), qmap), pl.BlockSpec((1, 1, dk, n_lat), kmap),
                  pl.BlockSpec((1, n_lat, dv), qmap), pl.BlockSpec((1, n_lat, dv), qmap)],
        out_specs=[pl.BlockSpec((1, n_ctx, dv), qmap), pl.BlockSpec((1, n_lat, dv), qmap)],
        out_shape=[jax.ShapeDtypeStruct((b, n_ctx, nh * dv), _BF16),
                   jax.ShapeDtypeStruct((b, n_lat, nh * dv), _BF16)],
        scratch_shapes=[pltpu.VMEM((dk, dv), _F32),
                        pltpu.VMEM((dk, dv), _F32),
                        pltpu.VMEM((n_ctx, dv), _F32),
                        pltpu.VMEM((n_lat, dv), _F32)],
        compiler_params=_params(2),
        name="retention",
    )(lg_f, lg_b, q_c, kt_c, v_c, sg_c, q_l, kt_l, v_l, sg_l)


def _outproj_kernel(t_ref, x_ref, w_ref, gate_ref, lng_ref, lnb_ref, o_ref, *, eps, sub):
    n_sub = t_ref.shape[1] // sub
    proj = lambda r: jnp.dot(t_ref[0, r * sub:(r + 1) * sub, :], w_ref[...], preferred_element_type=_F32)
    y_next = proj(0)
    for r in range(n_sub):
        y = y_next
        if r + 1 < n_sub:
            y_next = proj(r + 1)
        rows = slice(r * sub, (r + 1) * sub)
        z = x_ref[0, rows, :] + gate_ref[0] * y
        mu = jnp.mean(z, axis=-1, keepdims=True)
        zc = z - mu
        var = jnp.mean(zc * zc, axis=-1, keepdims=True)
        o_ref[0, rows, :] = zc * lax.rsqrt(var + eps) * lng_ref[...] + lnb_ref[...]


def _outproj(t, x, w, layer, gate, ln_g, ln_b, alpha, tm):
    b, n, d = x.shape
    row = lambda bi, i: (bi, i, 0)
    const = lambda bi, i: (0, 0)
    return pl.pallas_call(
        functools.partial(_outproj_kernel, eps=LN_EPS / alpha ** 2, sub=math.gcd(tm, OUT_SUB_ROWS)),
        grid=(b, n // tm),
        in_specs=[pl.BlockSpec((1, tm, t.shape[2]), row),
                  pl.BlockSpec((1, tm, d), row),
                  pl.BlockSpec((None,) + w.shape[1:], lambda bi, i: (layer, 0, 0)),
                  pl.BlockSpec((1, 1, d), lambda bi, i: (bi, 0, 0)),
                  pl.BlockSpec((1, d), const),
                  pl.BlockSpec((1, d), const)],
        out_specs=pl.BlockSpec((1, tm, d), row),
        out_shape=jax.ShapeDtypeStruct((b, n, d), _F32),
        compiler_params=_params(2),
        name="outproj",
    )(t, x, w, gate, ln_g, ln_b)


def _axial_perm():
    quarter = ATT_HEAD_DIM // 4
    return np.concatenate([np.arange(quarter) + off * quarter for off in (0, 2, 1, 3)])


def _axial_tables(s):
    quarter = ATT_HEAD_DIM // 4
    t = np.arange(s)
    pos = np.stack([t // GRID_W, t % GRID_W], axis=1).astype(np.float32)
    freqs = (ROPE_THETA ** (-np.arange(quarter, dtype=np.float32) / quarter)).astype(np.float32)
    ang = (pos[:, :, None] * freqs[None, None, :]).reshape(s, 2 * quarter)
    cos = np.concatenate([np.cos(ang), np.cos(ang)], axis=-1)
    sin = np.concatenate([-np.sin(ang), np.sin(ang)], axis=-1)
    return cos.astype(np.float32), sin.astype(np.float32)


def _permute_qk_columns(w_in):
    dh = ATT_HEAD_DIM
    qw = ATT_HEADS * dh
    kw = ATT_KV_HEADS * dh
    perm = _axial_perm()
    q_cols = (jnp.arange(ATT_HEADS)[:, None] * dh + perm[None, :]).reshape(-1)
    k_cols = 2 * qw + (jnp.arange(ATT_KV_HEADS)[:, None] * dh + perm[None, :]).reshape(-1)
    cols = jnp.concatenate([q_cols, jnp.arange(qw, 2 * qw), k_cols, jnp.arange(2 * qw + kw, 2 * qw + 2 * kw)])
    return w_in[..., cols]


def _rope_tables(pos):
    half = RET_QK_DIM // 2
    freqs = (ROPE_THETA ** (-np.arange(half, dtype=np.float32) / half)).astype(np.float32)
    ang = pos.astype(np.float32)[:, None] * freqs[None, :]
    return np.cos(ang).astype(np.float32), np.sin(ang).astype(np.float32)


def _row_tile(n, want):
    return want if n % want == 0 else n


def kernel(x, c, ctx, c_ctx, mod_w, mod_b, ln_g, ln_b, attn_w_in, attn_w_out, attn_q_scale, attn_k_scale,
           ret_w_in, ret_w_out, ret_gn_g, ret_log_decay_fwd, ret_log_decay_bwd):
    b, s, d = x.shape
    l = ctx.shape[1]
    depth = mod_w.shape[0]
    alpha = (2.0 * depth) ** 0.25

    rows = 8 * ((b + 1 + 7) // 8)
    cvec = jnp.zeros((rows, d), _F32).at[:b].set(c).at[b].set(c_ctx)
    mods = _modulation(cvec, mod_w, mod_b, alpha)

    cos_ax, sin_ax = _axial_tables(s)
    cos_id, sin_id = np.ones((l, ATT_HEAD_DIM), np.float32), np.zeros((l, ATT_HEAD_DIM), np.float32)
    cos_c, sin_c = _rope_tables(np.arange(l))
    cos_l, sin_l = _rope_tables(l + np.arange(s))
    attn_w_out_b = attn_w_out.astype(_BF16)
    ret_w_in_b = ret_w_in.astype(_BF16)
    ret_w_out_b = (ret_gn_g[:, :, None] * ret_w_out).astype(_BF16)

    tm_l = _row_tile(s, 512)
    tm_ret = _row_tile(s, 1024)
    tm_c = _row_tile(l, 256)
    tm_out = _row_tile(s, 1024)
    tq = _row_tile(s, 256)
    tk = _row_tile(s, 512)
    chunk = RET_CHUNK if (l % RET_CHUNK == 0 and s % RET_CHUNK == 0) else 128

    for i in range(depth):
        need_ctx = i < depth - 1
        j = i // 2
        shift_l = mods[i, :b, None, :d]
        scale_l = mods[i, :b, None, d:2 * d]
        gate_l = mods[i, :b, None, 2 * d:]
        shift_c = jnp.broadcast_to(mods[i, b, None, None, :d], (b, 1, d))
        scale_c = jnp.broadcast_to(mods[i, b, None, None, d:2 * d], (b, 1, d))
        gate_c = jnp.broadcast_to(mods[i, b, None, None, 2 * d:], (b, 1, d))
        lng = ln_g[i][None, :]
        lnb = ln_b[i][None, :]
        if i % 2 == 0:
            w_out = attn_w_out_b
            w_in = _permute_qk_columns(attn_w_in[j]).astype(_BF16)[None]
            qs = attn_q_scale[j][_axial_perm()][None, :]
            ks = attn_k_scale[j][_axial_perm()][None, :]
            q_l, k_l, vt_l, sg_l = _attn_inproj(x, shift_l, scale_l, w_in, 0, qs, ks, cos_ax, sin_ax, tm_l)
            q_c, k_c, vt_c, sg_c = _attn_inproj(ctx, shift_c, scale_c, w_in, 0, qs, ks, cos_id, sin_id, tm_c)
            t_l = _flash(q_l, sg_l, [(k_l, vt_l, tk), (k_c, vt_c, l)], tq,
                         FLASH_SUB_BLOCKS if s % (tq * FLASH_SUB_BLOCKS) == 0 else 1)
            if need_ctx:
                t_c = _flash(q_c, sg_c, [(k_c, vt_c, l)], _row_tile(l, 128))
        else:
            w_out = ret_w_out_b
            q_l, kt_l, v_l, sg_l = _ret_inproj(x, shift_l, scale_l, ret_w_in_b, j, cos_l, sin_l, tm_ret)
            q_c, kt_c, v_c, sg_c = _ret_inproj(ctx, shift_c, scale_c, ret_w_in_b, j, cos_c, sin_c, tm_c)
            t_c, t_l = _retention(ret_log_decay_fwd[j], ret_log_decay_bwd[j],
                                  q_c, kt_c, v_c, sg_c, q_l, kt_l, v_l, sg_l, chunk)
        x = _outproj(t_l, x, w_out, j, gate_l, lng, lnb, alpha, tm_out)
        if need_ctx:
            ctx = _outproj(t_c, ctx, w_out, j, gate_c, lng, lnb, alpha, tm_c)
    return x
```

```python
import functools
import math

import jax
import jax.numpy as jnp
import numpy as np
from jax import lax
from jax.experimental import pallas as pl
from jax.experimental.pallas import tpu as pltpu

GRID_W = 64
ROPE_THETA = 10000.0

ATT_HEADS = 8
ATT_KV_HEADS = 2
ATT_GROUP = ATT_HEADS // ATT_KV_HEADS
ATT_HEAD_DIM = 128
ONES_ROWS = 16

RET_HEADS = 4
RET_QK_DIM = 256
RET_V_DIM = 512
RET_CHUNK = 256

LN_EPS = 1e-5
QK_EPS = 1e-6
GN_EPS = 1e-5

MXU_COLS = 256
DOT_COLS = 2 * MXU_COLS
FLASH_PREFETCH = 4
FLASH_SUB_BLOCKS = 2
OUT_SUB_ROWS = 256
VMEM_LIMIT = 56 * 1024 * 1024

_BF16 = jnp.bfloat16
_F32 = jnp.float32


def _params(n_grid):
    return pltpu.CompilerParams(dimension_semantics=("arbitrary",) * n_grid,
                                vmem_limit_bytes=VMEM_LIMIT)


def _silu(g):
    return g * jax.nn.sigmoid(g)


def _mod_kernel(c_ref, w_ref, b_ref, o_ref, *, gate_mul):
    sc = _silu(c_ref[...])
    out = jnp.dot(sc, w_ref[0], preferred_element_type=_F32) + b_ref[0]
    o_ref[0] = out * jnp.where(pl.program_id(1) == 2, gate_mul, 1.0)


def _modulation(cvec, mod_w, mod_b, alpha):
    depth, d, d3 = mod_w.shape
    r = cvec.shape[0]
    return pl.pallas_call(
        functools.partial(_mod_kernel, gate_mul=1.0 / alpha),
        grid=(depth, d3 // d),
        in_specs=[pl.BlockSpec((r, d), lambda i, j: (0, 0)),
                  pl.BlockSpec((1, d, d), lambda i, j: (i, 0, j)),
                  pl.BlockSpec((1, 1, d), lambda i, j: (i, 0, j))],
        out_specs=pl.BlockSpec((1, r, d), lambda i, j: (i, 0, j)),
        out_shape=jax.ShapeDtypeStruct((depth, r, d3), _F32),
        compiler_params=_params(2),
        name="modulation",
    )(cvec, mod_w, mod_b.reshape(depth, 1, d3))


def _modulated(x_ref, shift_ref, scale_ref):
    return (x_ref[0] * (1.0 + scale_ref[0]) + shift_ref[0]).astype(_BF16)


def _slab_dot(h, w_ref, start):
    return jnp.dot(h, w_ref[:, start:start + MXU_COLS], preferred_element_type=_F32)


def _attn_inproj_kernel(x_ref, shift_ref, scale_ref, w_ref, qs_ref, ks_ref, cos_ref, sin_ref,
                        q_ref, k_ref, vt_ref, sg_ref, *, q_premul):
    h = _modulated(x_ref, shift_ref, scale_ref)
    dh = ATT_HEAD_DIM
    qw = ATT_HEADS * dh
    kw = ATT_KV_HEADS * dh
    cos = cos_ref[...]
    sin = sin_ref[...]

    def norm_rope_heads(o_ref, scale_row, mul, p):
        ts = [p[:, u * dh:(u + 1) * dh] for u in range(p.shape[1] // dh)]
        ms = [jnp.mean(t * t, axis=-1, keepdims=True) for t in ts]
        rs = [lax.rsqrt(m + QK_EPS) for m in ms]
        ts = [t * r * scale_row for t, r in zip(ts, rs)]
        rolled = [pltpu.roll(t, dh // 2, axis=1) for t in ts]
        ts = [t * cos + r * sin for t, r in zip(ts, rolled)]
        for hd, t in enumerate(ts):
            o_ref[0, :, hd * dh:(hd + 1) * dh] = (t if mul == 1.0 else t * mul).astype(_BF16)

    dot = lambda start, width: jnp.dot(h, w_ref[:, start:start + width], preferred_element_type=_F32)
    p = dot(2 * qw, 2 * kw)
    norm_rope_heads(k_ref, ks_ref[...], 1.0, p[:, :kw])
    for hd in range(ATT_KV_HEADS):
        vt_ref[0, hd, :dh, :] = p[:, kw + hd * dh:kw + (hd + 1) * dh].T.astype(_BF16)
        vt_ref[0, hd, dh:, :] = jnp.ones((ONES_ROWS, p.shape[0]), _BF16)
    norm_rope_heads(q_ref, qs_ref[...], q_premul, dot(0, qw))
    sg_ref[0] = _silu(dot(qw, qw)).astype(_BF16)


def _attn_inproj(x, shift, scale, w, layer, q_scale, k_scale, cos, sin, tm):
    b, n, d = x.shape
    qw = ATT_HEADS * ATT_HEAD_DIM
    kw = ATT_KV_HEADS * ATT_HEAD_DIM
    row = lambda bi, i: (bi, i, 0)
    vec = lambda bi, i: (bi, 0, 0)
    const = lambda bi, i: (0, 0)
    return pl.pallas_call(
        functools.partial(_attn_inproj_kernel, q_premul=ATT_HEAD_DIM ** -0.5 * math.log2(math.e)),
        grid=(b, n // tm),
        in_specs=[pl.BlockSpec((1, tm, d), row),
                  pl.BlockSpec((1, 1, d), vec),
                  pl.BlockSpec((1, 1, d), vec),
                  pl.BlockSpec((None,) + w.shape[1:], lambda bi, i: (layer, 0, 0)),
                  pl.BlockSpec((1, ATT_HEAD_DIM), const),
                  pl.BlockSpec((1, ATT_HEAD_DIM), const),
                  pl.BlockSpec((tm, ATT_HEAD_DIM), lambda bi, i: (i, 0)),
                  pl.BlockSpec((tm, ATT_HEAD_DIM), lambda bi, i: (i, 0))],
        out_specs=[pl.BlockSpec((1, tm, qw), row),
                   pl.BlockSpec((1, tm, kw), row),
                   pl.BlockSpec((1, ATT_KV_HEADS, ATT_HEAD_DIM + ONES_ROWS, tm), lambda bi, i: (bi, 0, 0, i)),
                   pl.BlockSpec((1, tm, qw), row)],
        out_shape=[jax.ShapeDtypeStruct((b, n, qw), _BF16),
                   jax.ShapeDtypeStruct((b, n, kw), _BF16),
                   jax.ShapeDtypeStruct((b, ATT_KV_HEADS, ATT_HEAD_DIM + ONES_ROWS, n), _BF16),
                   jax.ShapeDtypeStruct((b, n, qw), _BF16)],
        compiler_params=_params(2),
        name="attn_inproj",
    )(x, shift, scale, w, q_scale, k_scale, cos, sin)


def _ret_project(h, w_ref, cos, sin, qk_ref, v_ref, sg_ref, rows):
    dk = RET_QK_DIM
    half = dk // 2
    qkw = 2 * RET_HEADS * dk
    vw = RET_HEADS * RET_V_DIM
    for hd in range(2 * RET_HEADS):
        p = jnp.dot(h, w_ref[:, hd * dk:(hd + 1) * dk], preferred_element_type=_F32)
        x1, x2 = p[:, :half], p[:, half:]
        o1 = x1 * cos - x2 * sin
        o2 = x1 * sin + x2 * cos
        if hd >= RET_HEADS:
            o1 = o1 * dk ** -0.5
            o2 = o2 * dk ** -0.5
        qk_ref[0, rows, hd * dk:hd * dk + half] = o1.astype(_BF16)
        qk_ref[0, rows, hd * dk + half:(hd + 1) * dk] = o2.astype(_BF16)
    for j in range(vw // MXU_COLS):
        v_ref[0, rows, j * MXU_COLS:(j + 1) * MXU_COLS] = _slab_dot(h, w_ref, qkw + j * MXU_COLS).astype(_BF16)
    for j in range(vw // MXU_COLS):
        g = _slab_dot(h, w_ref, qkw + vw + j * MXU_COLS)
        sg_ref[0, rows, j * MXU_COLS:(j + 1) * MXU_COLS] = _silu(g).astype(_BF16)


def _flash_kernel(*refs, seg_chunks, tq, n_sub):
    q_ref, sg_ref = refs[0], refs[1]
    kv_refs = refs[2:-1]
    t_ref = refs[-1]
    dh = ATT_HEAD_DIM
    grp = ATT_GROUP
    chunks = []
    for si, tk in enumerate(seg_chunks):
        k_ref, vt_ref = kv_refs[2 * si], kv_refs[2 * si + 1]
        chunks += [(k_ref, vt_ref, c * tk, tk) for c in range(k_ref.shape[1] // tk)]
    units = [(sb, ci, i) for ci in range(len(chunks)) for sb in range(n_sub) for i in range(grp)]

    def scores(sb, ci, i):
        k_ref, _, start, tk = chunks[ci]
        q = q_ref[0, sb * tq:(sb + 1) * tq, i * dh:(i + 1) * dh]
        return lax.dot_general(k_ref[0, start:start + tk, :], q, (((1,), (1,)), ((), ())),
                               preferred_element_type=_F32)

    m, acc = {}, {}
    pending = [scores(*u) for u in units[:FLASH_PREFETCH]]
    for n, (sb, ci, i) in enumerate(units):
        _, vt_ref, start, tk = chunks[ci]
        s = pending.pop(0)
        if n + FLASH_PREFETCH < len(units):
            pending.append(scores(*units[n + FLASH_PREFETCH]))
        if ci == 0:
            m[sb, i] = jnp.full((1, tq), -jnp.inf, _F32)
            acc[sb, i] = jnp.zeros((dh + ONES_ROWS, tq), _F32)
        m_new = jnp.maximum(m[sb, i], jnp.max(s, axis=0, keepdims=True))
        alpha = jnp.exp2(m[sb, i] - m_new)
        p = jnp.exp2(s - m_new).astype(_BF16)
        acc[sb, i] = alpha * acc[sb, i] + jnp.dot(vt_ref[0, 0, :, start:start + tk], p,
                                                  preferred_element_type=_F32)
        m[sb, i] = m_new
        if ci == len(chunks) - 1:
            a = acc.pop((sb, i))
            o_t = a[:dh] * (1.0 / a[dh:dh + 1])
            gate = sg_ref[0, sb * tq:(sb + 1) * tq, i * dh:(i + 1) * dh].astype(_F32)
            t_ref[0, sb * tq:(sb + 1) * tq, i * dh:(i + 1) * dh] = (o_t.T * gate).astype(_BF16)


def _flash(q, sg, kv_segments, tq, n_sub=1):
    b, nq, qw = q.shape
    dh = ATT_HEAD_DIM
    gw = ATT_GROUP * dh
    qmap = lambda bi, hi, i: (bi, i, hi)
    kmap = lambda bi, hi, i: (bi, 0, hi)
    vmap = lambda bi, hi, i: (bi, hi, 0, 0)
    tstep = tq * n_sub
    in_specs = [pl.BlockSpec((1, tstep, gw), qmap), pl.BlockSpec((1, tstep, gw), qmap)]
    args = [q, sg]
    for k, vt, _ in kv_segments:
        in_specs += [pl.BlockSpec((1, k.shape[1], dh), kmap),
                     pl.BlockSpec((1, 1, dh + ONES_ROWS, vt.shape[3]), vmap)]
        args += [k, vt]
    return pl.pallas_call(
        functools.partial(_flash_kernel, seg_chunks=tuple(tk for _, _, tk in kv_segments), tq=tq, n_sub=n_sub),
        grid=(b, ATT_KV_HEADS, nq // tstep),
        in_specs=in_specs,
        out_specs=pl.BlockSpec((1, tstep, gw), qmap),
        out_shape=jax.ShapeDtypeStruct((b, nq, qw), _BF16),
        compiler_params=_params(3),
        name="flash_attention",
    )(*args)


def _retention_kernel(lgf_ref, lgb_ref, qc_ref, kc_ref, vc_ref, sgc_ref, ql_ref, kl_ref, vl_ref, sgl_ref,
                      tc_ref, tl_ref, state_f, state_b, fc_ref, fl_ref, *, chunk, unroll):
    hd = pl.program_id(1)
    c = chunk
    n_ctx = qc_ref.shape[1] // c
    n_lat = ql_ref.shape[1] // c
    ii = lax.broadcasted_iota(jnp.int32, (c, c), 0)
    jj = lax.broadcasted_iota(jnp.int32, (c, c), 1)
    row = lax.broadcasted_iota(jnp.int32, (c, 1), 0).astype(_F32)

    def tables(lg, forward):
        diff = (ii - jj) if forward else (jj - ii)
        keep = (diff >= 0) if forward else (diff > 0)
        decay = jnp.where(keep, jnp.exp(lg * jnp.maximum(diff, 0).astype(_F32)), 0.0)
        if forward:
            xi = jnp.exp(lg * (row + 1.0))
            zeta = jnp.exp(lg * (c - 1.0 - row))
        else:
            xi = jnp.exp(lg * (c - row))
            zeta = jnp.exp(lg * row)
        g_chunk = jnp.exp(jnp.full((1, 1), lg * c, _F32))
        return decay, xi, zeta, g_chunk

    def step(q_ref, k_ref, v_ref, sg_ref, f_ref, t_ref, start, tabs, state_ref, final):
        decay, xi, zeta, g_chunk = tabs
        sl = pl.ds(start, c)
        q = q_ref[0, sl, :]
        k = k_ref[0, sl, :]
        v = v_ref[0, sl, :]
        state = state_ref[...]
        s = lax.dot_general(q, k, (((1,), (1,)), ((), ())), preferred_element_type=_F32) * decay
        o = jnp.dot(s.astype(_BF16), v, preferred_element_type=_F32)
        o = o + jnp.dot(q, state.astype(_BF16), preferred_element_type=_F32) * xi
        kz = (k.astype(_F32) * zeta).astype(_BF16)
        upd = lax.dot_general(kz, v, (((0,), (0,)), ((), ())), preferred_element_type=_F32)
        state_ref[...] = state * g_chunk + upd
        if not final:
            f_ref[sl, :] = o
        else:
            o = o + f_ref[sl, :]
            mu = jnp.mean(o, axis=-1, keepdims=True)
            oc = o - mu
            var = jnp.mean(oc * oc, axis=-1, keepdims=True)
            on = (oc * lax.rsqrt(var + GN_EPS)).astype(_BF16)
            t_ref[0, sl, :] = on * sg_ref[0, sl, :]

    tabs_f = tables(lgf_ref[hd], True)
    tabs_b = tables(lgb_ref[hd], False)
    state_f[...] = jnp.zeros_like(state_f)
    state_b[...] = jnp.zeros_like(state_b)
    for i in range(n_ctx):
        step(qc_ref, kc_ref, vc_ref, sgc_ref, fc_ref, tc_ref, i * c, tabs_f, state_f, False)
    for i in reversed(range(n_ctx)):
        step(qc_ref, kc_ref, vc_ref, sgc_ref, fc_ref, tc_ref, i * c, tabs_b, state_b, True)

    half = n_lat // 2

    def lat_body(final):
        def body(i, carry):
            lo = pl.multiple_of(i * c, c)
            hi = pl.multiple_of((n_lat - 1 - i) * c, c)
            step(ql_ref, kl_ref, vl_ref, sgl_ref, fl_ref, tl_ref, lo, tabs_f, state_f, final)
            step(ql_ref, kl_ref, vl_ref, sgl_ref, fl_ref, tl_ref, hi, tabs_b, state_b, final)
            return carry
        return body

    lax.fori_loop(0, half, lat_body(False), 0, unroll=unroll)
    lax.fori_loop(half, n_lat, lat_body(True), 0, unroll=unroll)


def _retention(lg_f, lg_b, qk_c, v_c, sg_c, qk_l, v_l, sg_l, chunk):
    b, n_lat, _ = qk_l.shape
    n_ctx = qk_c.shape[1]
    assert (n_lat // chunk) % 2 == 0, "the two scan directions meet in the middle of the latents"
    dk, dv, nh = RET_QK_DIM, RET_V_DIM, RET_HEADS
    qmap = lambda bi, hi: (bi, 0, hi)
    kmap = lambda bi, hi: (bi, 0, nh + hi)
    smem = pl.BlockSpec(memory_space=pltpu.SMEM)
    unroll = math.gcd(n_lat // chunk // 2, 4)
    return pl.pallas_call(
        functools.partial(_retention_kernel, chunk=chunk, unroll=unroll),
        grid=(b, nh),
        in_specs=[smem, smem,
                  pl.BlockSpec((1, n_ctx, dk), qmap), pl.BlockSpec((1, n_ctx, dk), kmap),
                  pl.BlockSpec((1, n_ctx, dv), qmap), pl.BlockSpec((1, n_ctx, dv), qmap),
                  pl.BlockSpec((1, n_lat, dk), qmap), pl.BlockSpec((1, n_lat, dk), kmap),
                  pl.BlockSpec((1, n_lat, dv), qmap), pl.BlockSpec((1, n_lat, dv), qmap)],
        out_specs=[pl.BlockSpec((1, n_ctx, dv), qmap), pl.BlockSpec((1, n_lat, dv), qmap)],
        out_shape=[jax.ShapeDtypeStruct((b, n_ctx, nh * dv), _BF16),
                   jax.ShapeDtypeStruct((b, n_lat, nh * dv), _BF16)],
        scratch_shapes=[pltpu.VMEM((dk, dv), _F32),
                        pltpu.VMEM((dk, dv), _F32),
                        pltpu.VMEM((n_ctx, dv), _F32),
                        pltpu.VMEM((n_lat, dv), _F32)],
        compiler_params=_params(2),
        name="retention",
    )(lg_f, lg_b, qk_c, qk_c, v_c, sg_c, qk_l, qk_l, v_l, sg_l)


def _layer_norm_slab(x_ref, gate_ref, lng_ref, lnb_ref, rows, y, eps):
    z = x_ref[0, rows, :] + gate_ref[0] * y
    mu = jnp.mean(z, axis=-1, keepdims=True)
    zc = z - mu
    var = jnp.mean(zc * zc, axis=-1, keepdims=True)
    return zc * lax.rsqrt(var + eps) * lng_ref[...] + lnb_ref[...]


def _outproj_kernel(t_ref, x_ref, w_ref, gate_ref, lng_ref, lnb_ref, o_ref, *, eps, sub):
    n_sub = t_ref.shape[1] // sub
    proj = lambda r: jnp.dot(t_ref[0, r * sub:(r + 1) * sub, :], w_ref[...], preferred_element_type=_F32)
    y_next = proj(0)
    for r in range(n_sub):
        y = y_next
        if r + 1 < n_sub:
            y_next = proj(r + 1)
        rows = slice(r * sub, (r + 1) * sub)
        o_ref[0, rows, :] = _layer_norm_slab(x_ref, gate_ref, lng_ref, lnb_ref, rows, y, eps)


def _outproj(t, x, w, layer, gate, ln_g, ln_b, alpha, tm):
    b, n, d = x.shape
    row = lambda bi, i: (bi, i, 0)
    const = lambda bi, i: (0, 0)
    return pl.pallas_call(
        functools.partial(_outproj_kernel, eps=LN_EPS / alpha ** 2, sub=math.gcd(tm, OUT_SUB_ROWS)),
        grid=(b, n // tm),
        in_specs=[pl.BlockSpec((1, tm, t.shape[2]), row),
                  pl.BlockSpec((1, tm, d), row),
                  pl.BlockSpec((None,) + w.shape[1:], lambda bi, i: (layer, 0, 0)),
                  pl.BlockSpec((1, 1, d), lambda bi, i: (bi, 0, 0)),
                  pl.BlockSpec((1, d), const),
                  pl.BlockSpec((1, d), const)],
        out_specs=pl.BlockSpec((1, tm, d), row),
        out_shape=jax.ShapeDtypeStruct((b, n, d), _F32),
        compiler_params=_params(2),
        name="outproj",
    )(t, x, w, gate, ln_g, ln_b)


def _outproj_ret_inproj_kernel(t_ref, x_ref, wo_ref, gate_ref, lng_ref, lnb_ref, shift_ref, scale_ref, wi_ref,
                               cos_ref, sin_ref, xo_ref, qk_ref, v_ref, sg_ref, *, eps, sub):
    n_sub = t_ref.shape[1] // sub
    proj = lambda r: jnp.dot(t_ref[0, r * sub:(r + 1) * sub, :], wo_ref[...], preferred_element_type=_F32)
    y_next = proj(0)
    for r in range(n_sub):
        y = y_next
        if r + 1 < n_sub:
            y_next = proj(r + 1)
        rows = slice(r * sub, (r + 1) * sub)
        x_new = _layer_norm_slab(x_ref, gate_ref, lng_ref, lnb_ref, rows, y, eps)
        xo_ref[0, rows, :] = x_new
        h = (x_new * (1.0 + scale_ref[0]) + shift_ref[0]).astype(_BF16)
        _ret_project(h, wi_ref, cos_ref[rows, :], sin_ref[rows, :], qk_ref, v_ref, sg_ref, rows)


def _outproj_ret_inproj(t, x, wo, lo, gate, ln_g, ln_b, alpha, shift, scale, wi, li, cos, sin, tm):
    b, n, d = x.shape
    qkw = 2 * RET_HEADS * RET_QK_DIM
    vw = RET_HEADS * RET_V_DIM
    half = RET_QK_DIM // 2
    row = lambda bi, i: (bi, i, 0)
    vec = lambda bi, i: (bi, 0, 0)
    const = lambda bi, i: (0, 0)
    return pl.pallas_call(
        functools.partial(_outproj_ret_inproj_kernel, eps=LN_EPS / alpha ** 2, sub=math.gcd(tm, OUT_SUB_ROWS)),
        grid=(b, n // tm),
        in_specs=[pl.BlockSpec((1, tm, t.shape[2]), row),
                  pl.BlockSpec((1, tm, d), row),
                  pl.BlockSpec((None,) + wo.shape[1:], lambda bi, i: (lo, 0, 0), pipeline_mode=pl.Buffered(1)),
                  pl.BlockSpec((1, 1, d), vec),
                  pl.BlockSpec((1, d), const),
                  pl.BlockSpec((1, d), const),
                  pl.BlockSpec((1, 1, d), vec),
                  pl.BlockSpec((1, 1, d), vec),
                  pl.BlockSpec((None,) + wi.shape[1:], lambda bi, i: (li, 0, 0), pipeline_mode=pl.Buffered(1)),
                  pl.BlockSpec((tm, half), lambda bi, i: (i, 0)),
                  pl.BlockSpec((tm, half), lambda bi, i: (i, 0))],
        out_specs=[pl.BlockSpec((1, tm, d), row),
                   pl.BlockSpec((1, tm, qkw), row),
                   pl.BlockSpec((1, tm, vw), row),
                   pl.BlockSpec((1, tm, vw), row)],
        out_shape=[jax.ShapeDtypeStruct((b, n, d), _F32),
                   jax.ShapeDtypeStruct((b, n, qkw), _BF16),
                   jax.ShapeDtypeStruct((b, n, vw), _BF16),
                   jax.ShapeDtypeStruct((b, n, vw), _BF16)],
        compiler_params=_params(2),
        name="outproj_ret_inproj",
    )(t, x, wo, gate, ln_g, ln_b, shift, scale, wi, cos, sin)


def _axial_perm():
    quarter = ATT_HEAD_DIM // 4
    return np.concatenate([np.arange(quarter) + off * quarter for off in (0, 2, 1, 3)])


def _axial_tables(s):
    quarter = ATT_HEAD_DIM // 4
    t = np.arange(s)
    pos = np.stack([t // GRID_W, t % GRID_W], axis=1).astype(np.float32)
    freqs = (ROPE_THETA ** (-np.arange(quarter, dtype=np.float32) / quarter)).astype(np.float32)
    ang = (pos[:, :, None] * freqs[None, None, :]).reshape(s, 2 * quarter)
    cos = np.concatenate([np.cos(ang), np.cos(ang)], axis=-1)
    sin = np.concatenate([-np.sin(ang), np.sin(ang)], axis=-1)
    return cos.astype(np.float32), sin.astype(np.float32)


def _permute_qk_columns(w_in):
    dh = ATT_HEAD_DIM
    qw = ATT_HEADS * dh
    kw = ATT_KV_HEADS * dh
    perm = _axial_perm()
    q_cols = (jnp.arange(ATT_HEADS)[:, None] * dh + perm[None, :]).reshape(-1)
    k_cols = 2 * qw + (jnp.arange(ATT_KV_HEADS)[:, None] * dh + perm[None, :]).reshape(-1)
    cols = jnp.concatenate([q_cols, jnp.arange(qw, 2 * qw), k_cols, jnp.arange(2 * qw + kw, 2 * qw + 2 * kw)])
    return w_in[..., cols]


def _rope_tables(pos):
    half = RET_QK_DIM // 2
    freqs = (ROPE_THETA ** (-np.arange(half, dtype=np.float32) / half)).astype(np.float32)
    ang = pos.astype(np.float32)[:, None] * freqs[None, :]
    return np.cos(ang).astype(np.float32), np.sin(ang).astype(np.float32)


def _row_tile(n, want):
    return want if n % want == 0 else n


def kernel(x, c, ctx, c_ctx, mod_w, mod_b, ln_g, ln_b, attn_w_in, attn_w_out, attn_q_scale, attn_k_scale,
           ret_w_in, ret_w_out, ret_gn_g, ret_log_decay_fwd, ret_log_decay_bwd):
    b, s, d = x.shape
    l = ctx.shape[1]
    depth = mod_w.shape[0]
    alpha = (2.0 * depth) ** 0.25

    rows = 8 * ((b + 1 + 7) // 8)
    cvec = jnp.zeros((rows, d), _F32).at[:b].set(c).at[b].set(c_ctx)
    mods = _modulation(cvec, mod_w, mod_b, alpha)

    cos_ax, sin_ax = _axial_tables(s)
    cos_id, sin_id = np.ones((l, ATT_HEAD_DIM), np.float32), np.zeros((l, ATT_HEAD_DIM), np.float32)
    cos_c, sin_c = _rope_tables(np.arange(l))
    cos_l, sin_l = _rope_tables(l + np.arange(s))
    attn_w_out_b = attn_w_out.astype(_BF16)
    ret_w_in_b = ret_w_in.astype(_BF16)
    ret_w_out_b = (ret_gn_g[:, :, None] * ret_w_out).astype(_BF16)

    tm_l = _row_tile(s, 512)
    tm_c = _row_tile(l, 256)
    tm_out = _row_tile(s, 1024)
    tq = _row_tile(s, 256)
    tk = _row_tile(s, 512)
    chunk = RET_CHUNK if (l % RET_CHUNK == 0 and s % RET_CHUNK == 0) else 128

    def mod_vectors(i):
        lat = [mods[i, :b, None, k * d:(k + 1) * d] for k in range(3)]
        cx = [jnp.broadcast_to(mods[i, b, None, None, k * d:(k + 1) * d], (b, 1, d)) for k in range(3)]
        return lat, cx

    projected = None
    for i in range(depth):
        need_ctx = i < depth - 1
        j = i // 2
        (shift_l, scale_l, gate_l), (shift_c, scale_c, gate_c) = mod_vectors(i)
        lng = ln_g[i][None, :]
        lnb = ln_b[i][None, :]
        if i % 2 == 0:
            w_out = attn_w_out_b
            w_in = _permute_qk_columns(attn_w_in[j]).astype(_BF16)[None]
            qs = attn_q_scale[j][_axial_perm()][None, :]
            ks = attn_k_scale[j][_axial_perm()][None, :]
            q_l, k_l, vt_l, sg_l = _attn_inproj(x, shift_l, scale_l, w_in, 0, qs, ks, cos_ax, sin_ax, tm_l)
            q_c, k_c, vt_c, sg_c = _attn_inproj(ctx, shift_c, scale_c, w_in, 0, qs, ks, cos_id, sin_id, tm_c)
            t_l = _flash(q_l, sg_l, [(k_l, vt_l, tk), (k_c, vt_c, l)], tq,
                         FLASH_SUB_BLOCKS if s % (tq * FLASH_SUB_BLOCKS) == 0 else 1)
            if need_ctx:
                t_c = _flash(q_c, sg_c, [(k_c, vt_c, l)], _row_tile(l, 128))
        else:
            w_out = ret_w_out_b
            (qk_l, v_l, sg_l), (qk_c, v_c, sg_c) = projected
            t_c, t_l = _retention(ret_log_decay_fwd[j], ret_log_decay_bwd[j],
                                  qk_c, v_c, sg_c, qk_l, v_l, sg_l, chunk)
        if i + 1 < depth and (i + 1) % 2 == 1:
            (shift_n, scale_n, _), (shift_nc, scale_nc, _) = mod_vectors(i + 1)
            jn = (i + 1) // 2
            x, *proj_l = _outproj_ret_inproj(t_l, x, w_out, j, gate_l, lng, lnb, alpha, shift_n, scale_n,
                                             ret_w_in_b, jn, cos_l, sin_l, tm_l)
            ctx, *proj_c = _outproj_ret_inproj(t_c, ctx, w_out, j, gate_c, lng, lnb, alpha, shift_nc, scale_nc,
                                               ret_w_in_b, jn, cos_c, sin_c, tm_c)
            projected = (proj_l, proj_c)
        else:
            x = _outproj(t_l, x, w_out, j, gate_l, lng, lnb, alpha, tm_out)
            if need_ctx:
                ctx = _outproj(t_c, ctx, w_out, j, gate_c, lng, lnb, alpha, tm_c)
    return x
```

```python
import functools
import math

import jax
import jax.numpy as jnp
import numpy as np
from jax import lax
from jax.experimental import pallas as pl
from jax.experimental.pallas import tpu as pltpu

GRID_W = 64
ROPE_THETA = 10000.0

ATT_HEADS = 8
ATT_KV_HEADS = 2
ATT_GROUP = ATT_HEADS // ATT_KV_HEADS
ATT_HEAD_DIM = 128
ONES_ROWS = 16

RET_HEADS = 4
RET_QK_DIM = 256
RET_V_DIM = 512
RET_CHUNK = 256

LN_EPS = 1e-5
QK_EPS = 1e-6
GN_EPS = 1e-5

MXU_COLS = 256
DOT_COLS = 2 * MXU_COLS
FLASH_PREFETCH = 4
FLASH_SUB_BLOCKS = 2
OUT_SUB_ROWS = 256
Q_PREMUL = ATT_HEAD_DIM ** -0.5 * math.log2(math.e)
VMEM_LIMIT = 56 * 1024 * 1024

_BF16 = jnp.bfloat16
_F32 = jnp.float32


def _params(n_grid):
    return pltpu.CompilerParams(dimension_semantics=("arbitrary",) * n_grid,
                                vmem_limit_bytes=VMEM_LIMIT)


def _silu(g):
    return g * jax.nn.sigmoid(g)


def _mod_kernel(c_ref, w_ref, b_ref, o_ref, *, gate_mul):
    sc = _silu(c_ref[...])
    out = jnp.dot(sc, w_ref[0], preferred_element_type=_F32) + b_ref[0]
    o_ref[0] = out * jnp.where(pl.program_id(1) == 2, gate_mul, 1.0)


def _modulation(cvec, mod_w, mod_b, alpha):
    depth, d, d3 = mod_w.shape
    r = cvec.shape[0]
    return pl.pallas_call(
        functools.partial(_mod_kernel, gate_mul=1.0 / alpha),
        grid=(depth, d3 // d),
        in_specs=[pl.BlockSpec((r, d), lambda i, j: (0, 0)),
                  pl.BlockSpec((1, d, d), lambda i, j: (i, 0, j)),
                  pl.BlockSpec((1, 1, d), lambda i, j: (i, 0, j))],
        out_specs=pl.BlockSpec((1, r, d), lambda i, j: (i, 0, j)),
        out_shape=jax.ShapeDtypeStruct((depth, r, d3), _F32),
        compiler_params=_params(2),
        name="modulation",
    )(cvec, mod_w, mod_b.reshape(depth, 1, d3))


def _modulated(x_ref, shift_ref, scale_ref):
    return (x_ref[0] * (1.0 + scale_ref[0]) + shift_ref[0]).astype(_BF16)


def _slab_dot(h, w_ref, start):
    return jnp.dot(h, w_ref[:, start:start + MXU_COLS], preferred_element_type=_F32)


def _attn_project(h, w_ref, q_scale, k_scale, cos, sin, q_ref, k_ref, vt_ref, sg_ref, rows, q_premul):
    dh = ATT_HEAD_DIM
    qw = ATT_HEADS * dh
    kw = ATT_KV_HEADS * dh

    def norm_rope_heads(o_ref, scale_row, mul, p):
        ts = [p[:, u * dh:(u + 1) * dh] for u in range(p.shape[1] // dh)]
        ms = [jnp.mean(t * t, axis=-1, keepdims=True) for t in ts]
        rs = [lax.rsqrt(m + QK_EPS) for m in ms]
        ts = [t * r * scale_row for t, r in zip(ts, rs)]
        rolled = [pltpu.roll(t, dh // 2, axis=1) for t in ts]
        ts = [t * cos + r * sin for t, r in zip(ts, rolled)]
        for hd, t in enumerate(ts):
            o_ref[0, rows, hd * dh:(hd + 1) * dh] = (t if mul == 1.0 else t * mul).astype(_BF16)

    dot = lambda start, width: jnp.dot(h, w_ref[:, start:start + width], preferred_element_type=_F32)
    p = dot(2 * qw, 2 * kw)
    norm_rope_heads(k_ref, k_scale, 1.0, p[:, :kw])
    for hd in range(ATT_KV_HEADS):
        vt_ref[0, hd, :dh, rows] = p[:, kw + hd * dh:kw + (hd + 1) * dh].T.astype(_BF16)
        vt_ref[0, hd, dh:, rows] = jnp.ones((ONES_ROWS, p.shape[0]), _BF16)
    norm_rope_heads(q_ref, q_scale, q_premul, dot(0, qw))
    sg_ref[0, rows, :] = _silu(dot(qw, qw)).astype(_BF16)


def _attn_inproj_kernel(x_ref, shift_ref, scale_ref, w_ref, qs_ref, ks_ref, cos_ref, sin_ref,
                        q_ref, k_ref, vt_ref, sg_ref, *, q_premul):
    h = _modulated(x_ref, shift_ref, scale_ref)
    _attn_project(h, w_ref, qs_ref[...], ks_ref[...], cos_ref[...], sin_ref[...], q_ref, k_ref, vt_ref, sg_ref,
                  slice(None), q_premul)


def _attn_inproj(x, shift, scale, w, layer, q_scale, k_scale, cos, sin, tm):
    b, n, d = x.shape
    qw = ATT_HEADS * ATT_HEAD_DIM
    kw = ATT_KV_HEADS * ATT_HEAD_DIM
    row = lambda bi, i: (bi, i, 0)
    vec = lambda bi, i: (bi, 0, 0)
    const = lambda bi, i: (0, 0)
    return pl.pallas_call(
        functools.partial(_attn_inproj_kernel, q_premul=Q_PREMUL),
        grid=(b, n // tm),
        in_specs=[pl.BlockSpec((1, tm, d), row),
                  pl.BlockSpec((1, 1, d), vec),
                  pl.BlockSpec((1, 1, d), vec),
                  pl.BlockSpec((None,) + w.shape[1:], lambda bi, i: (layer, 0, 0)),
                  pl.BlockSpec((1, ATT_HEAD_DIM), const),
                  pl.BlockSpec((1, ATT_HEAD_DIM), const),
                  pl.BlockSpec((tm, ATT_HEAD_DIM), lambda bi, i: (i, 0)),
                  pl.BlockSpec((tm, ATT_HEAD_DIM), lambda bi, i: (i, 0))],
        out_specs=[pl.BlockSpec((1, tm, qw), row),
                   pl.BlockSpec((1, tm, kw), row),
                   pl.BlockSpec((1, ATT_KV_HEADS, ATT_HEAD_DIM + ONES_ROWS, tm), lambda bi, i: (bi, 0, 0, i)),
                   pl.BlockSpec((1, tm, qw), row)],
        out_shape=[jax.ShapeDtypeStruct((b, n, qw), _BF16),
                   jax.ShapeDtypeStruct((b, n, kw), _BF16),
                   jax.ShapeDtypeStruct((b, ATT_KV_HEADS, ATT_HEAD_DIM + ONES_ROWS, n), _BF16),
                   jax.ShapeDtypeStruct((b, n, qw), _BF16)],
        compiler_params=_params(2),
        name="attn_inproj",
    )(x, shift, scale, w, q_scale, k_scale, cos, sin)


def _ret_project(h, w_ref, cos, sin, qk_ref, v_ref, sg_ref, rows):
    dk = RET_QK_DIM
    half = dk // 2
    qkw = 2 * RET_HEADS * dk
    vw = RET_HEADS * RET_V_DIM
    for hd in range(2 * RET_HEADS):
        p = jnp.dot(h, w_ref[:, hd * dk:(hd + 1) * dk], preferred_element_type=_F32)
        x1, x2 = p[:, :half], p[:, half:]
        o1 = x1 * cos - x2 * sin
        o2 = x1 * sin + x2 * cos
        if hd >= RET_HEADS:
            o1 = o1 * dk ** -0.5
            o2 = o2 * dk ** -0.5
        qk_ref[0, rows, hd * dk:hd * dk + half] = o1.astype(_BF16)
        qk_ref[0, rows, hd * dk + half:(hd + 1) * dk] = o2.astype(_BF16)
    for j in range(vw // MXU_COLS):
        v_ref[0, rows, j * MXU_COLS:(j + 1) * MXU_COLS] = _slab_dot(h, w_ref, qkw + j * MXU_COLS).astype(_BF16)
    for j in range(vw // MXU_COLS):
        g = _slab_dot(h, w_ref, qkw + vw + j * MXU_COLS)
        sg_ref[0, rows, j * MXU_COLS:(j + 1) * MXU_COLS] = _silu(g).astype(_BF16)


def _flash_kernel(*refs, seg_chunks, tq, n_sub):
    q_ref, sg_ref = refs[0], refs[1]
    kv_refs = refs[2:-1]
    t_ref = refs[-1]
    dh = ATT_HEAD_DIM
    grp = ATT_GROUP
    chunks = []
    for si, tk in enumerate(seg_chunks):
        k_ref, vt_ref = kv_refs[2 * si], kv_refs[2 * si + 1]
        chunks += [(k_ref, vt_ref, c * tk, tk) for c in range(k_ref.shape[1] // tk)]
    units = [(sb, ci, i) for ci in range(len(chunks)) for sb in range(n_sub) for i in range(grp)]

    def scores(sb, ci, i):
        k_ref, _, start, tk = chunks[ci]
        q = q_ref[0, sb * tq:(sb + 1) * tq, i * dh:(i + 1) * dh]
        return lax.dot_general(k_ref[0, start:start + tk, :], q, (((1,), (1,)), ((), ())),
                               preferred_element_type=_F32)

    m, acc = {}, {}
    pending = [scores(*u) for u in units[:FLASH_PREFETCH]]
    for n, (sb, ci, i) in enumerate(units):
        _, vt_ref, start, tk = chunks[ci]
        s = pending.pop(0)
        if n + FLASH_PREFETCH < len(units):
            pending.append(scores(*units[n + FLASH_PREFETCH]))
        if ci == 0:
            m[sb, i] = jnp.full((1, tq), -jnp.inf, _F32)
            acc[sb, i] = jnp.zeros((dh + ONES_ROWS, tq), _F32)
        m_new = jnp.maximum(m[sb, i], jnp.max(s, axis=0, keepdims=True))
        alpha = jnp.exp2(m[sb, i] - m_new)
        p = jnp.exp2(s - m_new).astype(_BF16)
        acc[sb, i] = alpha * acc[sb, i] + jnp.dot(vt_ref[0, 0, :, start:start + tk], p,
                                                  preferred_element_type=_F32)
        m[sb, i] = m_new
        if ci == len(chunks) - 1:
            a = acc.pop((sb, i))
            o_t = a[:dh] * (1.0 / a[dh:dh + 1])
            gate = sg_ref[0, sb * tq:(sb + 1) * tq, i * dh:(i + 1) * dh].astype(_F32)
            t_ref[0, sb * tq:(sb + 1) * tq, i * dh:(i + 1) * dh] = (o_t.T * gate).astype(_BF16)


def _flash(q, sg, kv_segments, tq, n_sub=1):
    b, nq, qw = q.shape
    dh = ATT_HEAD_DIM
    gw = ATT_GROUP * dh
    qmap = lambda bi, hi, i: (bi, i, hi)
    kmap = lambda bi, hi, i: (bi, 0, hi)
    vmap = lambda bi, hi, i: (bi, hi, 0, 0)
    tstep = tq * n_sub
    in_specs = [pl.BlockSpec((1, tstep, gw), qmap), pl.BlockSpec((1, tstep, gw), qmap)]
    args = [q, sg]
    for k, vt, _ in kv_segments:
        in_specs += [pl.BlockSpec((1, k.shape[1], dh), kmap),
                     pl.BlockSpec((1, 1, dh + ONES_ROWS, vt.shape[3]), vmap)]
        args += [k, vt]
    return pl.pallas_call(
        functools.partial(_flash_kernel, seg_chunks=tuple(tk for _, _, tk in kv_segments), tq=tq, n_sub=n_sub),
        grid=(b, ATT_KV_HEADS, nq // tstep),
        in_specs=in_specs,
        out_specs=pl.BlockSpec((1, tstep, gw), qmap),
        out_shape=jax.ShapeDtypeStruct((b, nq, qw), _BF16),
        compiler_params=_params(3),
        name="flash_attention",
    )(*args)


def _retention_kernel(lgf_ref, lgb_ref, qc_ref, kc_ref, vc_ref, sgc_ref, ql_ref, kl_ref, vl_ref, sgl_ref,
                      tc_ref, tl_ref, state_f, state_b, fc_ref, fl_ref, *, chunk, unroll):
    hd = pl.program_id(1)
    c = chunk
    n_ctx = qc_ref.shape[1] // c
    n_lat = ql_ref.shape[1] // c
    ii = lax.broadcasted_iota(jnp.int32, (c, c), 0)
    jj = lax.broadcasted_iota(jnp.int32, (c, c), 1)
    row = lax.broadcasted_iota(jnp.int32, (c, 1), 0).astype(_F32)

    def tables(lg, forward):
        diff = (ii - jj) if forward else (jj - ii)
        keep = (diff >= 0) if forward else (diff > 0)
        decay = jnp.where(keep, jnp.exp(lg * jnp.maximum(diff, 0).astype(_F32)), 0.0)
        if forward:
            xi = jnp.exp(lg * (row + 1.0))
            zeta = jnp.exp(lg * (c - 1.0 - row))
        else:
            xi = jnp.exp(lg * (c - row))
            zeta = jnp.exp(lg * row)
        g_chunk = jnp.exp(jnp.full((1, 1), lg * c, _F32))
        return decay, xi, zeta, g_chunk

    def step(q_ref, k_ref, v_ref, sg_ref, f_ref, t_ref, start, tabs, state_ref, final):
        decay, xi, zeta, g_chunk = tabs
        sl = pl.ds(start, c)
        q = q_ref[0, sl, :]
        k = k_ref[0, sl, :]
        v = v_ref[0, sl, :]
        state = state_ref[...]
        s = lax.dot_general(q, k, (((1,), (1,)), ((), ())), preferred_element_type=_F32) * decay
        o = jnp.dot(s.astype(_BF16), v, preferred_element_type=_F32)
        o = o + jnp.dot(q, state.astype(_BF16), preferred_element_type=_F32) * xi
        kz = (k.astype(_F32) * zeta).astype(_BF16)
        upd = lax.dot_general(kz, v, (((0,), (0,)), ((), ())), preferred_element_type=_F32)
        state_ref[...] = state * g_chunk + upd
        if not final:
            f_ref[sl, :] = o
        else:
            o = o + f_ref[sl, :]
            mu = jnp.mean(o, axis=-1, keepdims=True)
            oc = o - mu
            var = jnp.mean(oc * oc, axis=-1, keepdims=True)
            on = (oc * lax.rsqrt(var + GN_EPS)).astype(_BF16)
            t_ref[0, sl, :] = on * sg_ref[0, sl, :]

    tabs_f = tables(lgf_ref[hd], True)
    tabs_b = tables(lgb_ref[hd], False)
    state_f[...] = jnp.zeros_like(state_f)
    state_b[...] = jnp.zeros_like(state_b)
    for i in range(n_ctx):
        step(qc_ref, kc_ref, vc_ref, sgc_ref, fc_ref, tc_ref, i * c, tabs_f, state_f, False)
    for i in reversed(range(n_ctx)):
        step(qc_ref, kc_ref, vc_ref, sgc_ref, fc_ref, tc_ref, i * c, tabs_b, state_b, True)

    half = n_lat // 2

    def lat_body(final):
        def body(i, carry):
            lo = pl.multiple_of(i * c, c)
            hi = pl.multiple_of((n_lat - 1 - i) * c, c)
            step(ql_ref, kl_ref, vl_ref, sgl_ref, fl_ref, tl_ref, lo, tabs_f, state_f, final)
            step(ql_ref, kl_ref, vl_ref, sgl_ref, fl_ref, tl_ref, hi, tabs_b, state_b, final)
            return carry
        return body

    lax.fori_loop(0, half, lat_body(False), 0, unroll=unroll)
    lax.fori_loop(half, n_lat, lat_body(True), 0, unroll=unroll)


def _retention(lg_f, lg_b, qk_c, v_c, sg_c, qk_l, v_l, sg_l, chunk):
    b, n_lat, _ = qk_l.shape
    n_ctx = qk_c.shape[1]
    assert (n_lat // chunk) % 2 == 0, "the two scan directions meet in the middle of the latents"
    dk, dv, nh = RET_QK_DIM, RET_V_DIM, RET_HEADS
    qmap = lambda bi, hi: (bi, 0, hi)
    kmap = lambda bi, hi: (bi, 0, nh + hi)
    smem = pl.BlockSpec(memory_space=pltpu.SMEM)
    unroll = math.gcd(n_lat // chunk // 2, 4)
    return pl.pallas_call(
        functools.partial(_retention_kernel, chunk=chunk, unroll=unroll),
        grid=(b, nh),
        in_specs=[smem, smem,
                  pl.BlockSpec((1, n_ctx, dk), qmap), pl.BlockSpec((1, n_ctx, dk), kmap),
                  pl.BlockSpec((1, n_ctx, dv), qmap), pl.BlockSpec((1, n_ctx, dv), qmap),
                  pl.BlockSpec((1, n_lat, dk), qmap), pl.BlockSpec((1, n_lat, dk), kmap),
                  pl.BlockSpec((1, n_lat, dv), qmap), pl.BlockSpec((1, n_lat, dv), qmap)],
        out_specs=[pl.BlockSpec((1, n_ctx, dv), qmap), pl.BlockSpec((1, n_lat, dv), qmap)],
        out_shape=[jax.ShapeDtypeStruct((b, n_ctx, nh * dv), _BF16),
                   jax.ShapeDtypeStruct((b, n_lat, nh * dv), _BF16)],
        scratch_shapes=[pltpu.VMEM((dk, dv), _F32),
                        pltpu.VMEM((dk, dv), _F32),
                        pltpu.VMEM((n_ctx, dv), _F32),
                        pltpu.VMEM((n_lat, dv), _F32)],
        compiler_params=_params(2),
        name="retention",
    )(lg_f, lg_b, qk_c, qk_c, v_c, sg_c, qk_l, qk_l, v_l, sg_l)


def _layer_norm_slab(x_ref, gate_ref, lng_ref, lnb_ref, rows, y, eps):
    z = x_ref[0, rows, :] + gate_ref[0] * y
    mu = jnp.mean(z, axis=-1, keepdims=True)
    zc = z - mu
    var = jnp.mean(zc * zc, axis=-1, keepdims=True)
    return zc * lax.rsqrt(var + eps) * lng_ref[...] + lnb_ref[...]


def _outproj_kernel(t_ref, x_ref, w_ref, gate_ref, lng_ref, lnb_ref, o_ref, *, eps, sub):
    n_sub = t_ref.shape[1] // sub
    proj = lambda r: jnp.dot(t_ref[0, r * sub:(r + 1) * sub, :], w_ref[...], preferred_element_type=_F32)
    y_next = proj(0)
    for r in range(n_sub):
        y = y_next
        if r + 1 < n_sub:
            y_next = proj(r + 1)
        rows = slice(r * sub, (r + 1) * sub)
        o_ref[0, rows, :] = _layer_norm_slab(x_ref, gate_ref, lng_ref, lnb_ref, rows, y, eps)


def _outproj(t, x, w, layer, gate, ln_g, ln_b, alpha, tm):
    b, n, d = x.shape
    row = lambda bi, i: (bi, i, 0)
    const = lambda bi, i: (0, 0)
    return pl.pallas_call(
        functools.partial(_outproj_kernel, eps=LN_EPS / alpha ** 2, sub=math.gcd(tm, OUT_SUB_ROWS)),
        grid=(b, n // tm),
        in_specs=[pl.BlockSpec((1, tm, t.shape[2]), row),
                  pl.BlockSpec((1, tm, d), row),
                  pl.BlockSpec((None,) + w.shape[1:], lambda bi, i: (layer, 0, 0)),
                  pl.BlockSpec((1, 1, d), lambda bi, i: (bi, 0, 0)),
                  pl.BlockSpec((1, d), const),
                  pl.BlockSpec((1, d), const)],
        out_specs=pl.BlockSpec((1, tm, d), row),
        out_shape=jax.ShapeDtypeStruct((b, n, d), _F32),
        compiler_params=_params(2),
        name="outproj",
    )(t, x, w, gate, ln_g, ln_b)


def _outproj_ret_inproj_kernel(t_ref, x_ref, wo_ref, gate_ref, lng_ref, lnb_ref, shift_ref, scale_ref, wi_ref,
                               cos_ref, sin_ref, xo_ref, qk_ref, v_ref, sg_ref, *, eps, sub):
    n_sub = t_ref.shape[1] // sub
    proj = lambda r: jnp.dot(t_ref[0, r * sub:(r + 1) * sub, :], wo_ref[...], preferred_element_type=_F32)
    y_next = proj(0)
    for r in range(n_sub):
        y = y_next
        if r + 1 < n_sub:
            y_next = proj(r + 1)
        rows = slice(r * sub, (r + 1) * sub)
        x_new = _layer_norm_slab(x_ref, gate_ref, lng_ref, lnb_ref, rows, y, eps)
        xo_ref[0, rows, :] = x_new
        h = (x_new * (1.0 + scale_ref[0]) + shift_ref[0]).astype(_BF16)
        _ret_project(h, wi_ref, cos_ref[rows, :], sin_ref[rows, :], qk_ref, v_ref, sg_ref, rows)


def _outproj_ret_inproj(t, x, wo, lo, gate, ln_g, ln_b, alpha, shift, scale, wi, li, cos, sin, tm):
    b, n, d = x.shape
    qkw = 2 * RET_HEADS * RET_QK_DIM
    vw = RET_HEADS * RET_V_DIM
    half = RET_QK_DIM // 2
    row = lambda bi, i: (bi, i, 0)
    vec = lambda bi, i: (bi, 0, 0)
    const = lambda bi, i: (0, 0)
    return pl.pallas_call(
        functools.partial(_outproj_ret_inproj_kernel, eps=LN_EPS / alpha ** 2, sub=math.gcd(tm, OUT_SUB_ROWS)),
        grid=(b, n // tm),
        in_specs=[pl.BlockSpec((1, tm, t.shape[2]), row),
                  pl.BlockSpec((1, tm, d), row),
                  pl.BlockSpec((None,) + wo.shape[1:], lambda bi, i: (lo, 0, 0), pipeline_mode=pl.Buffered(1)),
                  pl.BlockSpec((1, 1, d), vec),
                  pl.BlockSpec((1, d), const),
                  pl.BlockSpec((1, d), const),
                  pl.BlockSpec((1, 1, d), vec),
                  pl.BlockSpec((1, 1, d), vec),
                  pl.BlockSpec((None,) + wi.shape[1:], lambda bi, i: (li, 0, 0), pipeline_mode=pl.Buffered(1)),
                  pl.BlockSpec((tm, half), lambda bi, i: (i, 0)),
                  pl.BlockSpec((tm, half), lambda bi, i: (i, 0))],
        out_specs=[pl.BlockSpec((1, tm, d), row),
                   pl.BlockSpec((1, tm, qkw), row),
                   pl.BlockSpec((1, tm, vw), row),
                   pl.BlockSpec((1, tm, vw), row)],
        out_shape=[jax.ShapeDtypeStruct((b, n, d), _F32),
                   jax.ShapeDtypeStruct((b, n, qkw), _BF16),
                   jax.ShapeDtypeStruct((b, n, vw), _BF16),
                   jax.ShapeDtypeStruct((b, n, vw), _BF16)],
        compiler_params=_params(2),
        name="outproj_ret_inproj",
    )(t, x, wo, gate, ln_g, ln_b, shift, scale, wi, cos, sin)


def _outproj_attn_inproj_kernel(t_ref, x_ref, wo_ref, gate_ref, lng_ref, lnb_ref, shift_ref, scale_ref, wi_ref,
                                qs_ref, ks_ref, cos_ref, sin_ref, xo_ref, q_ref, k_ref, vt_ref, sg_ref, *, eps, sub):
    n_sub = t_ref.shape[1] // sub
    proj = lambda r: jnp.dot(t_ref[0, r * sub:(r + 1) * sub, :], wo_ref[...], preferred_element_type=_F32)
    y_next = proj(0)
    for r in range(n_sub):
        y = y_next
        if r + 1 < n_sub:
            y_next = proj(r + 1)
        rows = slice(r * sub, (r + 1) * sub)
        x_new = _layer_norm_slab(x_ref, gate_ref, lng_ref, lnb_ref, rows, y, eps)
        xo_ref[0, rows, :] = x_new
        h = (x_new * (1.0 + scale_ref[0]) + shift_ref[0]).astype(_BF16)
        _attn_project(h, wi_ref, qs_ref[...], ks_ref[...], cos_ref[rows, :], sin_ref[rows, :],
                      q_ref, k_ref, vt_ref, sg_ref, rows, Q_PREMUL)


def _outproj_attn_inproj(t, x, wo, lo, gate, ln_g, ln_b, alpha, shift, scale, wi, li, q_scale, k_scale, cos, sin, tm):
    b, n, d = x.shape
    qw = ATT_HEADS * ATT_HEAD_DIM
    kw = ATT_KV_HEADS * ATT_HEAD_DIM
    row = lambda bi, i: (bi, i, 0)
    vec = lambda bi, i: (bi, 0, 0)
    const = lambda bi, i: (0, 0)
    return pl.pallas_call(
        functools.partial(_outproj_attn_inproj_kernel, eps=LN_EPS / alpha ** 2, sub=math.gcd(tm, OUT_SUB_ROWS)),
        grid=(b, n // tm),
        in_specs=[pl.BlockSpec((1, tm, t.shape[2]), row),
                  pl.BlockSpec((1, tm, d), row),
                  pl.BlockSpec((None,) + wo.shape[1:], lambda bi, i: (lo, 0, 0), pipeline_mode=pl.Buffered(1)),
                  pl.BlockSpec((1, 1, d), vec),
                  pl.BlockSpec((1, d), const),
                  pl.BlockSpec((1, d), const),
                  pl.BlockSpec((1, 1, d), vec),
                  pl.BlockSpec((1, 1, d), vec),
                  pl.BlockSpec((None,) + wi.shape[1:], lambda bi, i: (li, 0, 0), pipeline_mode=pl.Buffered(1)),
                  pl.BlockSpec((1, ATT_HEAD_DIM), const),
                  pl.BlockSpec((1, ATT_HEAD_DIM), const),
                  pl.BlockSpec((tm, ATT_HEAD_DIM), lambda bi, i: (i, 0)),
                  pl.BlockSpec((tm, ATT_HEAD_DIM), lambda bi, i: (i, 0))],
        out_specs=[pl.BlockSpec((1, tm, d), row),
                   pl.BlockSpec((1, tm, qw), row),
                   pl.BlockSpec((1, tm, kw), row),
                   pl.BlockSpec((1, ATT_KV_HEADS, ATT_HEAD_DIM + ONES_ROWS, tm), lambda bi, i: (bi, 0, 0, i)),
                   pl.BlockSpec((1, tm, qw), row)],
        out_shape=[jax.ShapeDtypeStruct((b, n, d), _F32),
                   jax.ShapeDtypeStruct((b, n, qw), _BF16),
                   jax.ShapeDtypeStruct((b, n, kw), _BF16),
                   jax.ShapeDtypeStruct((b, ATT_KV_HEADS, ATT_HEAD_DIM + ONES_ROWS, n), _BF16),
                   jax.ShapeDtypeStruct((b, n, qw), _BF16)],
        compiler_params=_params(2),
        name="outproj_attn_inproj",
    )(t, x, wo, gate, ln_g, ln_b, shift, scale, wi, q_scale, k_scale, cos, sin)


def _axial_perm():
    quarter = ATT_HEAD_DIM // 4
    return np.concatenate([np.arange(quarter) + off * quarter for off in (0, 2, 1, 3)])


def _axial_tables(s):
    quarter = ATT_HEAD_DIM // 4
    t = np.arange(s)
    pos = np.stack([t // GRID_W, t % GRID_W], axis=1).astype(np.float32)
    freqs = (ROPE_THETA ** (-np.arange(quarter, dtype=np.float32) / quarter)).astype(np.float32)
    ang = (pos[:, :, None] * freqs[None, None, :]).reshape(s, 2 * quarter)
    cos = np.concatenate([np.cos(ang), np.cos(ang)], axis=-1)
    sin = np.concatenate([-np.sin(ang), np.sin(ang)], axis=-1)
    return cos.astype(np.float32), sin.astype(np.float32)


def _permute_qk_columns(w_in):
    dh = ATT_HEAD_DIM
    qw = ATT_HEADS * dh
    kw = ATT_KV_HEADS * dh
    perm = _axial_perm()
    q_cols = (jnp.arange(ATT_HEADS)[:, None] * dh + perm[None, :]).reshape(-1)
    k_cols = 2 * qw + (jnp.arange(ATT_KV_HEADS)[:, None] * dh + perm[None, :]).reshape(-1)
    cols = jnp.concatenate([q_cols, jnp.arange(qw, 2 * qw), k_cols, jnp.arange(2 * qw + kw, 2 * qw + 2 * kw)])
    return w_in[..., cols]


def _rope_tables(pos):
    half = RET_QK_DIM // 2
    freqs = (ROPE_THETA ** (-np.arange(half, dtype=np.float32) / half)).astype(np.float32)
    ang = pos.astype(np.float32)[:, None] * freqs[None, :]
    return np.cos(ang).astype(np.float32), np.sin(ang).astype(np.float32)


def _row_tile(n, want):
    return want if n % want == 0 else n


def kernel(x, c, ctx, c_ctx, mod_w, mod_b, ln_g, ln_b, attn_w_in, attn_w_out, attn_q_scale, attn_k_scale,
           ret_w_in, ret_w_out, ret_gn_g, ret_log_decay_fwd, ret_log_decay_bwd):
    b, s, d = x.shape
    l = ctx.shape[1]
    depth = mod_w.shape[0]
    alpha = (2.0 * depth) ** 0.25

    rows = 8 * ((b + 1 + 7) // 8)
    cvec = jnp.zeros((rows, d), _F32).at[:b].set(c).at[b].set(c_ctx)
    mods = _modulation(cvec, mod_w, mod_b, alpha)

    cos_ax, sin_ax = _axial_tables(s)
    cos_id, sin_id = np.ones((l, ATT_HEAD_DIM), np.float32), np.zeros((l, ATT_HEAD_DIM), np.float32)
    cos_c, sin_c = _rope_tables(np.arange(l))
    cos_l, sin_l = _rope_tables(l + np.arange(s))
    attn_w_out_b = attn_w_out.astype(_BF16)
    ret_w_in_b = ret_w_in.astype(_BF16)
    ret_w_out_b = (ret_gn_g[:, :, None] * ret_w_out).astype(_BF16)

    tm_l = _row_tile(s, 512)
    tm_c = _row_tile(l, 256)
    tm_out = _row_tile(s, 1024)
    tq = _row_tile(s, 256)
    tk = _row_tile(s, 512)
    chunk = RET_CHUNK if (l % RET_CHUNK == 0 and s % RET_CHUNK == 0) else 128

    def mod_vectors(i):
        lat = [mods[i, :b, None, k * d:(k + 1) * d] for k in range(3)]
        cx = [jnp.broadcast_to(mods[i, b, None, None, k * d:(k + 1) * d], (b, 1, d)) for k in range(3)]
        return lat, cx

    def attn_weights(j):
        w_in = _permute_qk_columns(attn_w_in[j]).astype(_BF16)[None]
        return w_in, attn_q_scale[j][_axial_perm()][None, :], attn_k_scale[j][_axial_perm()][None, :]

    projected = None
    for i in range(depth):
        need_ctx = i < depth - 1
        j = i // 2
        (shift_l, scale_l, gate_l), (shift_c, scale_c, gate_c) = mod_vectors(i)
        lng = ln_g[i][None, :]
        lnb = ln_b[i][None, :]
        if i % 2 == 0:
            w_out = attn_w_out_b
            if projected is None:
                w_in, qs, ks = attn_weights(j)
                projected = (_attn_inproj(x, shift_l, scale_l, w_in, 0, qs, ks, cos_ax, sin_ax, tm_l),
                             _attn_inproj(ctx, shift_c, scale_c, w_in, 0, qs, ks, cos_id, sin_id, tm_c))
            (q_l, k_l, vt_l, sg_l), (q_c, k_c, vt_c, sg_c) = projected
            t_l = _flash(q_l, sg_l, [(k_l, vt_l, tk), (k_c, vt_c, l)], tq,
                         FLASH_SUB_BLOCKS if s % (tq * FLASH_SUB_BLOCKS) == 0 else 1)
            if need_ctx:
                t_c = _flash(q_c, sg_c, [(k_c, vt_c, l)], _row_tile(l, 128))
        else:
            w_out = ret_w_out_b
            (qk_l, v_l, sg_l), (qk_c, v_c, sg_c) = projected
            t_c, t_l = _retention(ret_log_decay_fwd[j], ret_log_decay_bwd[j],
                                  qk_c, v_c, sg_c, qk_l, v_l, sg_l, chunk)
        if i + 1 == depth:
            x = _outproj(t_l, x, w_out, j, gate_l, lng, lnb, alpha, tm_out)
            continue
        (shift_n, scale_n, _), (shift_nc, scale_nc, _) = mod_vectors(i + 1)
        jn = (i + 1) // 2
        head = (w_out, j)
        if (i + 1) % 2 == 1:
            tail = (ret_w_in_b, jn)
            x, *proj_l = _outproj_ret_inproj(t_l, x, *head, gate_l, lng, lnb, alpha, shift_n, scale_n, *tail,
                                             cos_l, sin_l, tm_l)
            ctx, *proj_c = _outproj_ret_inproj(t_c, ctx, *head, gate_c, lng, lnb, alpha, shift_nc, scale_nc, *tail,
                                               cos_c, sin_c, tm_c)
        else:
            w_in, qs, ks = attn_weights(jn)
            tail = (w_in, 0, qs, ks)
            x, *proj_l = _outproj_attn_inproj(t_l, x, *head, gate_l, lng, lnb, alpha, shift_n, scale_n, *tail,
                                              cos_ax, sin_ax, tm_l)
            ctx, *proj_c = _outproj_attn_inproj(t_c, ctx, *head, gate_c, lng, lnb, alpha, shift_nc, scale_nc, *tail,
                                                cos_id, sin_id, tm_c)
        projected = (proj_l, proj_c)
    return x
```

```python
import functools
import math

import jax
import jax.numpy as jnp
import numpy as np
from jax import lax
from jax.experimental import pallas as pl
from jax.experimental.pallas import tpu as pltpu

GRID_W = 64
ROPE_THETA = 10000.0

ATT_HEADS = 8
ATT_KV_HEADS = 2
ATT_GROUP = ATT_HEADS // ATT_KV_HEADS
ATT_HEAD_DIM = 128
ONES_ROWS = 16

RET_HEADS = 4
RET_QK_DIM = 256
RET_V_DIM = 512
RET_CHUNK = 256

LN_EPS = 1e-5
QK_EPS = 1e-6
GN_EPS = 1e-5

MXU_COLS = 256
DOT_COLS = 2 * MXU_COLS
FLASH_PREFETCH = 4
FLASH_SUB_BLOCKS = 2
OUT_SUB_ROWS = 256
Q_PREMUL = ATT_HEAD_DIM ** -0.5 * math.log2(math.e)
VMEM_LIMIT = 56 * 1024 * 1024

_BF16 = jnp.bfloat16
_F32 = jnp.float32


def _params(n_grid):
    return pltpu.CompilerParams(dimension_semantics=("arbitrary",) * n_grid,
                                vmem_limit_bytes=VMEM_LIMIT)


def _silu(g):
    return g * jax.nn.sigmoid(g)


def _mod_kernel(c_ref, w_ref, b_ref, o_ref, *, gate_mul):
    sc = _silu(c_ref[...])
    out = jnp.dot(sc, w_ref[0], preferred_element_type=_F32) + b_ref[0]
    o_ref[0] = out * jnp.where(pl.program_id(1) == 2, gate_mul, 1.0)


def _modulation(cvec, mod_w, mod_b, alpha):
    depth, d, d3 = mod_w.shape
    r = cvec.shape[0]
    return pl.pallas_call(
        functools.partial(_mod_kernel, gate_mul=1.0 / alpha),
        grid=(depth, d3 // d),
        in_specs=[pl.BlockSpec((r, d), lambda i, j: (0, 0)),
                  pl.BlockSpec((1, d, d), lambda i, j: (i, 0, j)),
                  pl.BlockSpec((1, 1, d), lambda i, j: (i, 0, j))],
        out_specs=pl.BlockSpec((1, r, d), lambda i, j: (i, 0, j)),
        out_shape=jax.ShapeDtypeStruct((depth, r, d3), _F32),
        compiler_params=_params(2),
        name="modulation",
    )(cvec, mod_w, mod_b.reshape(depth, 1, d3))


def _modulated(x_ref, shift_ref, scale_ref):
    return (x_ref[0] * (1.0 + scale_ref[0]) + shift_ref[0]).astype(_BF16)


def _slab_dot(h, w_ref, start):
    return jnp.dot(h, w_ref[:, start:start + MXU_COLS], preferred_element_type=_F32)


def _attn_project(h, w_ref, q_scale, k_scale, cos, sin, q_ref, k_ref, vt_ref, sg_ref, rows, q_premul):
    dh = ATT_HEAD_DIM
    qw = ATT_HEADS * dh
    kw = ATT_KV_HEADS * dh

    def norm_rope_heads(o_ref, scale_row, mul, p):
        ts = [p[:, u * dh:(u + 1) * dh] for u in range(p.shape[1] // dh)]
        ms = [jnp.mean(t * t, axis=-1, keepdims=True) for t in ts]
        rs = [lax.rsqrt(m + QK_EPS) for m in ms]
        ts = [t * r * scale_row for t, r in zip(ts, rs)]
        rolled = [pltpu.roll(t, dh // 2, axis=1) for t in ts]
        ts = [t * cos + r * sin for t, r in zip(ts, rolled)]
        for hd, t in enumerate(ts):
            o_ref[0, rows, hd * dh:(hd + 1) * dh] = (t if mul == 1.0 else t * mul).astype(_BF16)

    dot = lambda start, width: jnp.dot(h, w_ref[:, start:start + width], preferred_element_type=_F32)
    p = dot(2 * qw, 2 * kw)
    norm_rope_heads(k_ref, k_scale, 1.0, p[:, :kw])
    for hd in range(ATT_KV_HEADS):
        vt_ref[0, hd, :dh, rows] = p[:, kw + hd * dh:kw + (hd + 1) * dh].T.astype(_BF16)
        vt_ref[0, hd, dh:, rows] = jnp.ones((ONES_ROWS, p.shape[0]), _BF16)
    norm_rope_heads(q_ref, q_scale, q_premul, dot(0, qw))
    sg_ref[0, rows, :] = _silu(dot(qw, qw)).astype(_BF16)


def _attn_inproj_kernel(x_ref, shift_ref, scale_ref, w_ref, qs_ref, ks_ref, cos_ref, sin_ref,
                        q_ref, k_ref, vt_ref, sg_ref, *, q_premul):
    h = _modulated(x_ref, shift_ref, scale_ref)
    _attn_project(h, w_ref, qs_ref[...], ks_ref[...], cos_ref[...], sin_ref[...], q_ref, k_ref, vt_ref, sg_ref,
                  slice(None), q_premul)


def _attn_inproj(x, shift, scale, w, layer, q_scale, k_scale, cos, sin, tm):
    b, n, d = x.shape
    qw = ATT_HEADS * ATT_HEAD_DIM
    kw = ATT_KV_HEADS * ATT_HEAD_DIM
    row = lambda bi, i: (bi, i, 0)
    vec = lambda bi, i: (bi, 0, 0)
    const = lambda bi, i: (0, 0)
    return pl.pallas_call(
        functools.partial(_attn_inproj_kernel, q_premul=Q_PREMUL),
        grid=(b, n // tm),
        in_specs=[pl.BlockSpec((1, tm, d), row),
                  pl.BlockSpec((1, 1, d), vec),
                  pl.BlockSpec((1, 1, d), vec),
                  pl.BlockSpec((None,) + w.shape[1:], lambda bi, i: (layer, 0, 0)),
                  pl.BlockSpec((1, ATT_HEAD_DIM), const),
                  pl.BlockSpec((1, ATT_HEAD_DIM), const),
                  pl.BlockSpec((tm, ATT_HEAD_DIM), lambda bi, i: (i, 0)),
                  pl.BlockSpec((tm, ATT_HEAD_DIM), lambda bi, i: (i, 0))],
        out_specs=[pl.BlockSpec((1, tm, qw), row),
                   pl.BlockSpec((1, tm, kw), row),
                   pl.BlockSpec((1, ATT_KV_HEADS, ATT_HEAD_DIM + ONES_ROWS, tm), lambda bi, i: (bi, 0, 0, i)),
                   pl.BlockSpec((1, tm, qw), row)],
        out_shape=[jax.ShapeDtypeStruct((b, n, qw), _BF16),
                   jax.ShapeDtypeStruct((b, n, kw), _BF16),
                   jax.ShapeDtypeStruct((b, ATT_KV_HEADS, ATT_HEAD_DIM + ONES_ROWS, n), _BF16),
                   jax.ShapeDtypeStruct((b, n, qw), _BF16)],
        compiler_params=_params(2),
        name="attn_inproj",
    )(x, shift, scale, w, q_scale, k_scale, cos, sin)


def _ret_project(h, w_ref, cos, sin, qk_ref, v_ref, sg_ref, rows):
    dk = RET_QK_DIM
    half = dk // 2
    qkw = 2 * RET_HEADS * dk
    vw = RET_HEADS * RET_V_DIM
    for hd in range(2 * RET_HEADS):
        p = jnp.dot(h, w_ref[:, hd * dk:(hd + 1) * dk], preferred_element_type=_F32)
        x1, x2 = p[:, :half], p[:, half:]
        o1 = x1 * cos - x2 * sin
        o2 = x1 * sin + x2 * cos
        if hd >= RET_HEADS:
            o1 = o1 * dk ** -0.5
            o2 = o2 * dk ** -0.5
        qk_ref[0, rows, hd * dk:hd * dk + half] = o1.astype(_BF16)
        qk_ref[0, rows, hd * dk + half:(hd + 1) * dk] = o2.astype(_BF16)
    for j in range(vw // MXU_COLS):
        v_ref[0, rows, j * MXU_COLS:(j + 1) * MXU_COLS] = _slab_dot(h, w_ref, qkw + j * MXU_COLS).astype(_BF16)
    for j in range(vw // MXU_COLS):
        g = _slab_dot(h, w_ref, qkw + vw + j * MXU_COLS)
        sg_ref[0, rows, j * MXU_COLS:(j + 1) * MXU_COLS] = _silu(g).astype(_BF16)


def _flash_kernel(*refs, seg_chunks, tq, n_sub):
    q_ref, sg_ref = refs[0], refs[1]
    kv_refs = refs[2:-1]
    t_ref = refs[-1]
    dh = ATT_HEAD_DIM
    grp = ATT_GROUP
    chunks = []
    for si, tk in enumerate(seg_chunks):
        k_ref, vt_ref = kv_refs[2 * si], kv_refs[2 * si + 1]
        chunks += [(k_ref, vt_ref, c * tk, tk) for c in range(k_ref.shape[1] // tk)]
    units = [(sb, ci, i) for ci in range(len(chunks)) for sb in range(n_sub) for i in range(grp)]

    def scores(sb, ci, i):
        k_ref, _, start, tk = chunks[ci]
        q = q_ref[0, sb * tq:(sb + 1) * tq, i * dh:(i + 1) * dh]
        return lax.dot_general(k_ref[0, start:start + tk, :], q, (((1,), (1,)), ((), ())),
                               preferred_element_type=_F32)

    m, acc = {}, {}
    pending = [scores(*u) for u in units[:FLASH_PREFETCH]]
    for n, (sb, ci, i) in enumerate(units):
        _, vt_ref, start, tk = chunks[ci]
        s = pending.pop(0)
        if n + FLASH_PREFETCH < len(units):
            pending.append(scores(*units[n + FLASH_PREFETCH]))
        if ci == 0:
            m[sb, i] = jnp.full((1, tq), -jnp.inf, _F32)
            acc[sb, i] = jnp.zeros((dh + ONES_ROWS, tq), _F32)
        m_new = jnp.maximum(m[sb, i], jnp.max(s, axis=0, keepdims=True))
        alpha = jnp.exp2(m[sb, i] - m_new)
        p = jnp.exp2(s - m_new).astype(_BF16)
        acc[sb, i] = alpha * acc[sb, i] + jnp.dot(vt_ref[0, 0, :, start:start + tk], p,
                                                  preferred_element_type=_F32)
        m[sb, i] = m_new
        if ci == len(chunks) - 1:
            a = acc.pop((sb, i))
            o_t = a[:dh] * (1.0 / a[dh:dh + 1])
            gate = sg_ref[0, sb * tq:(sb + 1) * tq, i * dh:(i + 1) * dh].astype(_F32)
            t_ref[0, sb * tq:(sb + 1) * tq, i * dh:(i + 1) * dh] = (o_t.T * gate).astype(_BF16)


def _flash(q, sg, kv_segments, tq, n_sub=1):
    b, nq, qw = q.shape
    dh = ATT_HEAD_DIM
    gw = ATT_GROUP * dh
    qmap = lambda bi, hi, i: (bi, i, hi)
    kmap = lambda bi, hi, i: (bi, 0, hi)
    vmap = lambda bi, hi, i: (bi, hi, 0, 0)
    tstep = tq * n_sub
    in_specs = [pl.BlockSpec((1, tstep, gw), qmap), pl.BlockSpec((1, tstep, gw), qmap)]
    args = [q, sg]
    for k, vt, _ in kv_segments:
        in_specs += [pl.BlockSpec((1, k.shape[1], dh), kmap),
                     pl.BlockSpec((1, 1, dh + ONES_ROWS, vt.shape[3]), vmap)]
        args += [k, vt]
    return pl.pallas_call(
        functools.partial(_flash_kernel, seg_chunks=tuple(tk for _, _, tk in kv_segments), tq=tq, n_sub=n_sub),
        grid=(b, ATT_KV_HEADS, nq // tstep),
        in_specs=in_specs,
        out_specs=pl.BlockSpec((1, tstep, gw), qmap),
        out_shape=jax.ShapeDtypeStruct((b, nq, qw), _BF16),
        compiler_params=_params(3),
        name="flash_attention",
    )(*args)


def _retention_kernel(lgf_ref, lgb_ref, qc_ref, kc_ref, vc_ref, sgc_ref, ql_ref, kl_ref, vl_ref, sgl_ref,
                      tc_ref, tl_ref, state_f, state_b, fc_ref, fl_ref, *, chunk, unroll):
    hd = pl.program_id(1)
    c = chunk
    n_ctx = qc_ref.shape[1] // c
    n_lat = ql_ref.shape[1] // c
    ii = lax.broadcasted_iota(jnp.int32, (c, c), 0)
    jj = lax.broadcasted_iota(jnp.int32, (c, c), 1)
    row = lax.broadcasted_iota(jnp.int32, (c, 1), 0).astype(_F32)

    def tables(lg, forward):
        diff = (ii - jj) if forward else (jj - ii)
        keep = (diff >= 0) if forward else (diff > 0)
        decay = jnp.where(keep, jnp.exp(lg * jnp.maximum(diff, 0).astype(_F32)), 0.0)
        if forward:
            xi = jnp.exp(lg * (row + 1.0))
            zeta = jnp.exp(lg * (c - 1.0 - row))
        else:
            xi = jnp.exp(lg * (c - row))
            zeta = jnp.exp(lg * row)
        g_chunk = jnp.exp(jnp.full((1, 1), lg * c, _F32))
        return decay, xi, zeta, g_chunk

    def step(q_ref, k_ref, v_ref, sg_ref, f_ref, t_ref, start, tabs, state_ref, final):
        decay, xi, zeta, g_chunk = tabs
        sl = pl.ds(start, c)
        q = q_ref[0, sl, :]
        k = k_ref[0, sl, :]
        v = v_ref[0, sl, :]
        state = state_ref[...]
        s = lax.dot_general(q, k, (((1,), (1,)), ((), ())), preferred_element_type=_F32) * decay
        o = jnp.dot(s.astype(_BF16), v, preferred_element_type=_F32)
        o = o + jnp.dot(q, state.astype(_BF16), preferred_element_type=_F32) * xi
        kz = (k.astype(_F32) * zeta).astype(_BF16)
        upd = lax.dot_general(kz, v, (((0,), (0,)), ((), ())), preferred_element_type=_F32)
        state_ref[...] = state * g_chunk + upd
        if not final:
            f_ref[sl, :] = o
        else:
            o = o + f_ref[sl, :]
            mu = jnp.mean(o, axis=-1, keepdims=True)
            oc = o - mu
            var = jnp.mean(oc * oc, axis=-1, keepdims=True)
            on = (oc * lax.rsqrt(var + GN_EPS)).astype(_BF16)
            t_ref[0, sl, :] = on * sg_ref[0, sl, :]

    tabs_f = tables(lgf_ref[hd], True)
    tabs_b = tables(lgb_ref[hd], False)
    state_f[...] = jnp.zeros_like(state_f)
    state_b[...] = jnp.zeros_like(state_b)
    for i in range(n_ctx):
        step(qc_ref, kc_ref, vc_ref, sgc_ref, fc_ref, tc_ref, i * c, tabs_f, state_f, False)
    for i in reversed(range(n_ctx)):
        step(qc_ref, kc_ref, vc_ref, sgc_ref, fc_ref, tc_ref, i * c, tabs_b, state_b, True)

    half = n_lat // 2

    def lat_body(final):
        def body(i, carry):
            lo = pl.multiple_of(i * c, c)
            hi = pl.multiple_of((n_lat - 1 - i) * c, c)
            step(ql_ref, kl_ref, vl_ref, sgl_ref, fl_ref, tl_ref, lo, tabs_f, state_f, final)
            step(ql_ref, kl_ref, vl_ref, sgl_ref, fl_ref, tl_ref, hi, tabs_b, state_b, final)
            return carry
        return body

    lax.fori_loop(0, half, lat_body(False), 0, unroll=min(2 * unroll, max(half, 1)))
    lax.fori_loop(half, n_lat, lat_body(True), 0, unroll=unroll)


def _retention(lg_f, lg_b, qk_c, v_c, sg_c, qk_l, v_l, sg_l, chunk):
    b, n_lat, _ = qk_l.shape
    n_ctx = qk_c.shape[1]
    assert (n_lat // chunk) % 2 == 0, "the two scan directions meet in the middle of the latents"
    dk, dv, nh = RET_QK_DIM, RET_V_DIM, RET_HEADS
    qmap = lambda bi, hi: (bi, 0, hi)
    kmap = lambda bi, hi: (bi, 0, nh + hi)
    smem = pl.BlockSpec(memory_space=pltpu.SMEM)
    unroll = math.gcd(n_lat // chunk // 2, 4)
    return pl.pallas_call(
        functools.partial(_retention_kernel, chunk=chunk, unroll=unroll),
        grid=(b, nh),
        in_specs=[smem, smem,
                  pl.BlockSpec((1, n_ctx, dk), qmap), pl.BlockSpec((1, n_ctx, dk), kmap),
                  pl.BlockSpec((1, n_ctx, dv), qmap), pl.BlockSpec((1, n_ctx, dv), qmap),
                  pl.BlockSpec((1, n_lat, dk), qmap), pl.BlockSpec((1, n_lat, dk), kmap),
                  pl.BlockSpec((1, n_lat, dv), qmap), pl.BlockSpec((1, n_lat, dv), qmap)],
        out_specs=[pl.BlockSpec((1, n_ctx, dv), qmap), pl.BlockSpec((1, n_lat, dv), qmap)],
        out_shape=[jax.ShapeDtypeStruct((b, n_ctx, nh * dv), _BF16),
                   jax.ShapeDtypeStruct((b, n_lat, nh * dv), _BF16)],
        scratch_shapes=[pltpu.VMEM((dk, dv), _F32),
                        pltpu.VMEM((dk, dv), _F32),
                        pltpu.VMEM((n_ctx, dv), _F32),
                        pltpu.VMEM((n_lat, dv), _F32)],
        compiler_params=_params(2),
        name="retention",
    )(lg_f, lg_b, qk_c, qk_c, v_c, sg_c, qk_l, qk_l, v_l, sg_l)


def _layer_norm_slab(x_ref, gate_ref, lng_ref, lnb_ref, rows, y, eps):
    z = x_ref[0, rows, :] + gate_ref[0] * y
    mu = jnp.mean(z, axis=-1, keepdims=True)
    zc = z - mu
    var = jnp.mean(zc * zc, axis=-1, keepdims=True)
    return zc * lax.rsqrt(var + eps) * lng_ref[...] + lnb_ref[...]


def _outproj_kernel(t_ref, x_ref, w_ref, gate_ref, lng_ref, lnb_ref, o_ref, *, eps, sub):
    n_sub = t_ref.shape[1] // sub
    proj = lambda r: jnp.dot(t_ref[0, r * sub:(r + 1) * sub, :], w_ref[...], preferred_element_type=_F32)
    y_next = proj(0)
    for r in range(n_sub):
        y = y_next
        if r + 1 < n_sub:
            y_next = proj(r + 1)
        rows = slice(r * sub, (r + 1) * sub)
        o_ref[0, rows, :] = _layer_norm_slab(x_ref, gate_ref, lng_ref, lnb_ref, rows, y, eps)


def _outproj(t, x, w, layer, gate, ln_g, ln_b, alpha, tm):
    b, n, d = x.shape
    row = lambda bi, i: (bi, i, 0)
    const = lambda bi, i: (0, 0)
    return pl.pallas_call(
        functools.partial(_outproj_kernel, eps=LN_EPS / alpha ** 2, sub=math.gcd(tm, OUT_SUB_ROWS)),
        grid=(b, n // tm),
        in_specs=[pl.BlockSpec((1, tm, t.shape[2]), row),
                  pl.BlockSpec((1, tm, d), row),
                  pl.BlockSpec((None,) + w.shape[1:], lambda bi, i: (layer, 0, 0)),
                  pl.BlockSpec((1, 1, d), lambda bi, i: (bi, 0, 0)),
                  pl.BlockSpec((1, d), const),
                  pl.BlockSpec((1, d), const)],
        out_specs=pl.BlockSpec((1, tm, d), row),
        out_shape=jax.ShapeDtypeStruct((b, n, d), _F32),
        compiler_params=_params(2),
        name="outproj",
    )(t, x, w, gate, ln_g, ln_b)


def _outproj_ret_inproj_kernel(t_ref, x_ref, wo_ref, gate_ref, lng_ref, lnb_ref, shift_ref, scale_ref, wi_ref,
                               cos_ref, sin_ref, xo_ref, qk_ref, v_ref, sg_ref, *, eps, sub):
    n_sub = t_ref.shape[1] // sub
    proj = lambda r: jnp.dot(t_ref[0, r * sub:(r + 1) * sub, :], wo_ref[...], preferred_element_type=_F32)
    y_next = proj(0)
    for r in range(n_sub):
        y = y_next
        if r + 1 < n_sub:
            y_next = proj(r + 1)
        rows = slice(r * sub, (r + 1) * sub)
        x_new = _layer_norm_slab(x_ref, gate_ref, lng_ref, lnb_ref, rows, y, eps)
        xo_ref[0, rows, :] = x_new
        h = (x_new * (1.0 + scale_ref[0]) + shift_ref[0]).astype(_BF16)
        _ret_project(h, wi_ref, cos_ref[rows, :], sin_ref[rows, :], qk_ref, v_ref, sg_ref, rows)


def _outproj_ret_inproj(t, x, wo, lo, gate, ln_g, ln_b, alpha, shift, scale, wi, li, cos, sin, tm):
    b, n, d = x.shape
    qkw = 2 * RET_HEADS * RET_QK_DIM
    vw = RET_HEADS * RET_V_DIM
    half = RET_QK_DIM // 2
    row = lambda bi, i: (bi, i, 0)
    vec = lambda bi, i: (bi, 0, 0)
    const = lambda bi, i: (0, 0)
    return pl.pallas_call(
        functools.partial(_outproj_ret_inproj_kernel, eps=LN_EPS / alpha ** 2, sub=math.gcd(tm, OUT_SUB_ROWS)),
        grid=(b, n // tm),
        in_specs=[pl.BlockSpec((1, tm, t.shape[2]), row),
                  pl.BlockSpec((1, tm, d), row),
                  pl.BlockSpec((None,) + wo.shape[1:], lambda bi, i: (lo, 0, 0), pipeline_mode=pl.Buffered(1)),
                  pl.BlockSpec((1, 1, d), vec),
                  pl.BlockSpec((1, d), const),
                  pl.BlockSpec((1, d), const),
                  pl.BlockSpec((1, 1, d), vec),
                  pl.BlockSpec((1, 1, d), vec),
                  pl.BlockSpec((None,) + wi.shape[1:], lambda bi, i: (li, 0, 0), pipeline_mode=pl.Buffered(1)),
                  pl.BlockSpec((tm, half), lambda bi, i: (i, 0)),
                  pl.BlockSpec((tm, half), lambda bi, i: (i, 0))],
        out_specs=[pl.BlockSpec((1, tm, d), row),
                   pl.BlockSpec((1, tm, qkw), row),
                   pl.BlockSpec((1, tm, vw), row),
                   pl.BlockSpec((1, tm, vw), row)],
        out_shape=[jax.ShapeDtypeStruct((b, n, d), _F32),
                   jax.ShapeDtypeStruct((b, n, qkw), _BF16),
                   jax.ShapeDtypeStruct((b, n, vw), _BF16),
                   jax.ShapeDtypeStruct((b, n, vw), _BF16)],
        compiler_params=_params(2),
        name="outproj_ret_inproj",
    )(t, x, wo, gate, ln_g, ln_b, shift, scale, wi, cos, sin)


def _outproj_attn_inproj_kernel(t_ref, x_ref, wo_ref, gate_ref, lng_ref, lnb_ref, shift_ref, scale_ref, wi_ref,
                                qs_ref, ks_ref, cos_ref, sin_ref, xo_ref, q_ref, k_ref, vt_ref, sg_ref, *, eps, sub):
    n_sub = t_ref.shape[1] // sub
    proj = lambda r: jnp.dot(t_ref[0, r * sub:(r + 1) * sub, :], wo_ref[...], preferred_element_type=_F32)
    y_next = proj(0)
    for r in range(n_sub):
        y = y_next
        if r + 1 < n_sub:
            y_next = proj(r + 1)
        rows = slice(r * sub, (r + 1) * sub)
        x_new = _layer_norm_slab(x_ref, gate_ref, lng_ref, lnb_ref, rows, y, eps)
        xo_ref[0, rows, :] = x_new
        h = (x_new * (1.0 + scale_ref[0]) + shift_ref[0]).astype(_BF16)
        _attn_project(h, wi_ref, qs_ref[...], ks_ref[...], cos_ref[rows, :], sin_ref[rows, :],
                      q_ref, k_ref, vt_ref, sg_ref, rows, Q_PREMUL)


def _outproj_attn_inproj(t, x, wo, lo, gate, ln_g, ln_b, alpha, shift, scale, wi, li, q_scale, k_scale, cos, sin, tm):
    b, n, d = x.shape
    qw = ATT_HEADS * ATT_HEAD_DIM
    kw = ATT_KV_HEADS * ATT_HEAD_DIM
    row = lambda bi, i: (bi, i, 0)
    vec = lambda bi, i: (bi, 0, 0)
    const = lambda bi, i: (0, 0)
    return pl.pallas_call(
        functools.partial(_outproj_attn_inproj_kernel, eps=LN_EPS / alpha ** 2, sub=math.gcd(tm, OUT_SUB_ROWS)),
        grid=(b, n // tm),
        in_specs=[pl.BlockSpec((1, tm, t.shape[2]), row),
                  pl.BlockSpec((1, tm, d), row),
                  pl.BlockSpec((None,) + wo.shape[1:], lambda bi, i: (lo, 0, 0), pipeline_mode=pl.Buffered(1)),
                  pl.BlockSpec((1, 1, d), vec),
                  pl.BlockSpec((1, d), const),
                  pl.BlockSpec((1, d), const),
                  pl.BlockSpec((1, 1, d), vec),
                  pl.BlockSpec((1, 1, d), vec),
                  pl.BlockSpec((None,) + wi.shape[1:], lambda bi, i: (li, 0, 0), pipeline_mode=pl.Buffered(1)),
                  pl.BlockSpec((1, ATT_HEAD_DIM), const),
                  pl.BlockSpec((1, ATT_HEAD_DIM), const),
                  pl.BlockSpec((tm, ATT_HEAD_DIM), lambda bi, i: (i, 0)),
                  pl.BlockSpec((tm, ATT_HEAD_DIM), lambda bi, i: (i, 0))],
        out_specs=[pl.BlockSpec((1, tm, d), row),
                   pl.BlockSpec((1, tm, qw), row),
                   pl.BlockSpec((1, tm, kw), row),
                   pl.BlockSpec((1, ATT_KV_HEADS, ATT_HEAD_DIM + ONES_ROWS, tm), lambda bi, i: (bi, 0, 0, i)),
                   pl.BlockSpec((1, tm, qw), row)],
        out_shape=[jax.ShapeDtypeStruct((b, n, d), _F32),
                   jax.ShapeDtypeStruct((b, n, qw), _BF16),
                   jax.ShapeDtypeStruct((b, n, kw), _BF16),
                   jax.ShapeDtypeStruct((b, ATT_KV_HEADS, ATT_HEAD_DIM + ONES_ROWS, n), _BF16),
                   jax.ShapeDtypeStruct((b, n, qw), _BF16)],
        compiler_params=_params(2),
        name="outproj_attn_inproj",
    )(t, x, wo, gate, ln_g, ln_b, shift, scale, wi, q_scale, k_scale, cos, sin)


def _axial_perm():
    quarter = ATT_HEAD_DIM // 4
    return np.concatenate([np.arange(quarter) + off * quarter for off in (0, 2, 1, 3)])


def _axial_tables(s):
    quarter = ATT_HEAD_DIM // 4
    t = np.arange(s)
    pos = np.stack([t // GRID_W, t % GRID_W], axis=1).astype(np.float32)
    freqs = (ROPE_THETA ** (-np.arange(quarter, dtype=np.float32) / quarter)).astype(np.float32)
    ang = (pos[:, :, None] * freqs[None, None, :]).reshape(s, 2 * quarter)
    cos = np.concatenate([np.cos(ang), np.cos(ang)], axis=-1)
    sin = np.concatenate([-np.sin(ang), np.sin(ang)], axis=-1)
    return cos.astype(np.float32), sin.astype(np.float32)


def _permute_qk_columns(w_in):
    dh = ATT_HEAD_DIM
    qw = ATT_HEADS * dh
    kw = ATT_KV_HEADS * dh
    perm = _axial_perm()
    q_cols = (jnp.arange(ATT_HEADS)[:, None] * dh + perm[None, :]).reshape(-1)
    k_cols = 2 * qw + (jnp.arange(ATT_KV_HEADS)[:, None] * dh + perm[None, :]).reshape(-1)
    cols = jnp.concatenate([q_cols, jnp.arange(qw, 2 * qw), k_cols, jnp.arange(2 * qw + kw, 2 * qw + 2 * kw)])
    return w_in[..., cols]


def _rope_tables(pos):
    half = RET_QK_DIM // 2
    freqs = (ROPE_THETA ** (-np.arange(half, dtype=np.float32) / half)).astype(np.float32)
    ang = pos.astype(np.float32)[:, None] * freqs[None, :]
    return np.cos(ang).astype(np.float32), np.sin(ang).astype(np.float32)


def _row_tile(n, want):
    return want if n % want == 0 else n


def kernel(x, c, ctx, c_ctx, mod_w, mod_b, ln_g, ln_b, attn_w_in, attn_w_out, attn_q_scale, attn_k_scale,
           ret_w_in, ret_w_out, ret_gn_g, ret_log_decay_fwd, ret_log_decay_bwd):
    b, s, d = x.shape
    l = ctx.shape[1]
    depth = mod_w.shape[0]
    alpha = (2.0 * depth) ** 0.25

    rows = 8 * ((b + 1 + 7) // 8)
    cvec = jnp.zeros((rows, d), _F32).at[:b].set(c).at[b].set(c_ctx)
    mods = _modulation(cvec, mod_w, mod_b, alpha)

    cos_ax, sin_ax = _axial_tables(s)
    cos_id, sin_id = np.ones((l, ATT_HEAD_DIM), np.float32), np.zeros((l, ATT_HEAD_DIM), np.float32)
    cos_c, sin_c = _rope_tables(np.arange(l))
    cos_l, sin_l = _rope_tables(l + np.arange(s))
    attn_w_out_b = attn_w_out.astype(_BF16)
    ret_w_in_b = ret_w_in.astype(_BF16)
    ret_w_out_b = (ret_gn_g[:, :, None] * ret_w_out).astype(_BF16)

    tm_l = _row_tile(s, 512)
    tm_c = _row_tile(l, 256)
    tm_out = _row_tile(s, 1024)
    tq = _row_tile(s, 256)
    tk = _row_tile(s, 512)
    chunk = RET_CHUNK if (l % RET_CHUNK == 0 and s % RET_CHUNK == 0) else 128

    def mod_vectors(i):
        lat = [mods[i, :b, None, k * d:(k + 1) * d] for k in range(3)]
        cx = [jnp.broadcast_to(mods[i, b, None, None, k * d:(k + 1) * d], (b, 1, d)) for k in range(3)]
        return lat, cx

    def attn_weights(j):
        w_in = _permute_qk_columns(attn_w_in[j]).astype(_BF16)[None]
        return w_in, attn_q_scale[j][_axial_perm()][None, :], attn_k_scale[j][_axial_perm()][None, :]

    projected = None
    for i in range(depth):
        need_ctx = i < depth - 1
        j = i // 2
        (shift_l, scale_l, gate_l), (shift_c, scale_c, gate_c) = mod_vectors(i)
        lng = ln_g[i][None, :]
        lnb = ln_b[i][None, :]
        if i % 2 == 0:
            w_out = attn_w_out_b
            if projected is None:
                w_in, qs, ks = attn_weights(j)
                projected = (_attn_inproj(x, shift_l, scale_l, w_in, 0, qs, ks, cos_ax, sin_ax, tm_l),
                             _attn_inproj(ctx, shift_c, scale_c, w_in, 0, qs, ks, cos_id, sin_id, tm_c))
            (q_l, k_l, vt_l, sg_l), (q_c, k_c, vt_c, sg_c) = projected
            t_l = _flash(q_l, sg_l, [(k_l, vt_l, tk), (k_c, vt_c, l)], tq,
                         FLASH_SUB_BLOCKS if s % (tq * FLASH_SUB_BLOCKS) == 0 else 1)
            if need_ctx:
                t_c = _flash(q_c, sg_c, [(k_c, vt_c, l)], _row_tile(l, 128))
        else:
            w_out = ret_w_out_b
            (qk_l, v_l, sg_l), (qk_c, v_c, sg_c) = projected
            t_c, t_l = _retention(ret_log_decay_fwd[j], ret_log_decay_bwd[j],
                                  qk_c, v_c, sg_c, qk_l, v_l, sg_l, chunk)
        if i + 1 == depth:
            x = _outproj(t_l, x, w_out, j, gate_l, lng, lnb, alpha, tm_out)
            continue
        (shift_n, scale_n, _), (shift_nc, scale_nc, _) = mod_vectors(i + 1)
        jn = (i + 1) // 2
        head = (w_out, j)
        if (i + 1) % 2 == 1:
            tail = (ret_w_in_b, jn)
            x, *proj_l = _outproj_ret_inproj(t_l, x, *head, gate_l, lng, lnb, alpha, shift_n, scale_n, *tail,
                                             cos_l, sin_l, tm_l)
            ctx, *proj_c = _outproj_ret_inproj(t_c, ctx, *head, gate_c, lng, lnb, alpha, shift_nc, scale_nc, *tail,
                                               cos_c, sin_c, tm_c)
        else:
            w_in, qs, ks = attn_weights(jn)
            tail = (w_in, 0, qs, ks)
            x, *proj_l = _outproj_attn_inproj(t_l, x, *head, gate_l, lng, lnb, alpha, shift_n, scale_n, *tail,
                                              cos_ax, sin_ax, tm_l)
            ctx, *proj_c = _outproj_attn_inproj(t_c, ctx, *head, gate_c, lng, lnb, alpha, shift_nc, scale_nc, *tail,
                                                cos_id, sin_id, tm_c)
        projected = (proj_l, proj_c)
    return x
```

```python
import functools
import math

import jax
import jax.numpy as jnp
import numpy as np
from jax import lax
from jax.experimental import pallas as pl
from jax.experimental.pallas import tpu as pltpu

GRID_W = 64
ROPE_THETA = 10000.0

ATT_HEADS = 8
ATT_KV_HEADS = 2
ATT_GROUP = ATT_HEADS // ATT_KV_HEADS
ATT_HEAD_DIM = 128
ONES_ROWS = 16

RET_HEADS = 4
RET_QK_DIM = 256
RET_V_DIM = 512
RET_CHUNK = 256

LN_EPS = 1e-5
QK_EPS = 1e-6
GN_EPS = 1e-5

MXU_COLS = 256
FLASH_PREFETCH = 4
FLASH_SUB_BLOCKS = 2
OUT_SUB_ROWS = 256
Q_PREMUL = ATT_HEAD_DIM ** -0.5 * math.log2(math.e)
VMEM_LIMIT = 56 * 1024 * 1024

_BF16 = jnp.bfloat16
_F32 = jnp.float32


def _params(n_grid):
    return pltpu.CompilerParams(dimension_semantics=("arbitrary",) * n_grid,
                                vmem_limit_bytes=VMEM_LIMIT)


def _silu(g):
    return g * jax.nn.sigmoid(g)


def _mod_kernel(c_ref, w_ref, b_ref, o_ref, *, gate_mul):
    sc = _silu(c_ref[...])
    out = jnp.dot(sc, w_ref[0], preferred_element_type=_F32) + b_ref[0]
    o_ref[0] = out * jnp.where(pl.program_id(1) == 2, gate_mul, 1.0)


def _modulation(cvec, mod_w, mod_b, alpha):
    depth, d, d3 = mod_w.shape
    r = cvec.shape[0]
    return pl.pallas_call(
        functools.partial(_mod_kernel, gate_mul=1.0 / alpha),
        grid=(depth, d3 // d),
        in_specs=[pl.BlockSpec((r, d), lambda i, j: (0, 0)),
                  pl.BlockSpec((1, d, d), lambda i, j: (i, 0, j)),
                  pl.BlockSpec((1, 1, d), lambda i, j: (i, 0, j))],
        out_specs=pl.BlockSpec((1, r, d), lambda i, j: (i, 0, j)),
        out_shape=jax.ShapeDtypeStruct((depth, r, d3), _F32),
        compiler_params=_params(2),
        name="modulation",
    )(cvec, mod_w, mod_b.reshape(depth, 1, d3))


def _modulated(x, shift_ref, scale_ref):
    return (x * (1.0 + scale_ref[0]) + shift_ref[0]).astype(_BF16)


def _slab_dot(h, w_ref, start):
    return jnp.dot(h, w_ref[:, start:start + MXU_COLS], preferred_element_type=_F32)


def _attn_project(h, w_ref, q_scale, k_scale, cos, sin, q_ref, k_ref, vt_ref, sg_ref, rows, q_premul):
    dh = ATT_HEAD_DIM
    qw = ATT_HEADS * dh
    kw = ATT_KV_HEADS * dh

    def norm_rope_heads(o_ref, scale_row, mul, p):
        ts = [p[:, u * dh:(u + 1) * dh] for u in range(p.shape[1] // dh)]
        ms = [jnp.mean(t * t, axis=-1, keepdims=True) for t in ts]
        rs = [lax.rsqrt(m + QK_EPS) for m in ms]
        ts = [t * r * scale_row for t, r in zip(ts, rs)]
        rolled = [pltpu.roll(t, dh // 2, axis=1) for t in ts]
        ts = [t * cos + r * sin for t, r in zip(ts, rolled)]
        for hd, t in enumerate(ts):
            o_ref[0, rows, hd * dh:(hd + 1) * dh] = (t if mul == 1.0 else t * mul).astype(_BF16)

    dot = lambda start, width: jnp.dot(h, w_ref[:, start:start + width], preferred_element_type=_F32)
    p = dot(2 * qw, 2 * kw)
    norm_rope_heads(k_ref, k_scale, 1.0, p[:, :kw])
    for hd in range(ATT_KV_HEADS):
        vt_ref[0, hd, :dh, rows] = p[:, kw + hd * dh:kw + (hd + 1) * dh].T.astype(_BF16)
        vt_ref[0, hd, dh:, rows] = jnp.ones((ONES_ROWS, p.shape[0]), _BF16)
    norm_rope_heads(q_ref, q_scale, q_premul, dot(0, qw))
    sg_ref[0, rows, :] = _silu(dot(qw, qw)).astype(_BF16)


def _attn_inproj_kernel(x_ref, shift_ref, scale_ref, w_ref, qs_ref, ks_ref, cos_ref, sin_ref,
                        q_ref, k_ref, vt_ref, sg_ref, *, q_premul, sub):
    for r in range(x_ref.shape[1] // sub):
        rows = slice(r * sub, (r + 1) * sub)
        h = _modulated(x_ref[0, rows, :], shift_ref, scale_ref)
        _attn_project(h, w_ref, qs_ref[...], ks_ref[...], cos_ref[rows, :], sin_ref[rows, :],
                      q_ref, k_ref, vt_ref, sg_ref, rows, q_premul)


def _attn_inproj(x, shift, scale, w, layer, q_scale, k_scale, cos, sin, tm):
    b, n, d = x.shape
    qw = ATT_HEADS * ATT_HEAD_DIM
    kw = ATT_KV_HEADS * ATT_HEAD_DIM
    row = lambda bi, i: (bi, i, 0)
    vec = lambda bi, i: (bi, 0, 0)
    const = lambda bi, i: (0, 0)
    return pl.pallas_call(
        functools.partial(_attn_inproj_kernel, q_premul=Q_PREMUL, sub=math.gcd(tm, OUT_SUB_ROWS)),
        grid=(b, n // tm),
        in_specs=[pl.BlockSpec((1, tm, d), row),
                  pl.BlockSpec((1, 1, d), vec),
                  pl.BlockSpec((1, 1, d), vec),
                  pl.BlockSpec((None,) + w.shape[1:], lambda bi, i: (layer, 0, 0)),
                  pl.BlockSpec((1, ATT_HEAD_DIM), const),
                  pl.BlockSpec((1, ATT_HEAD_DIM), const),
                  pl.BlockSpec((tm, ATT_HEAD_DIM), lambda bi, i: (i, 0)),
                  pl.BlockSpec((tm, ATT_HEAD_DIM), lambda bi, i: (i, 0))],
        out_specs=[pl.BlockSpec((1, tm, qw), row),
                   pl.BlockSpec((1, tm, kw), row),
                   pl.BlockSpec((1, ATT_KV_HEADS, ATT_HEAD_DIM + ONES_ROWS, tm), lambda bi, i: (bi, 0, 0, i)),
                   pl.BlockSpec((1, tm, qw), row)],
        out_shape=[jax.ShapeDtypeStruct((b, n, qw), _BF16),
                   jax.ShapeDtypeStruct((b, n, kw), _BF16),
                   jax.ShapeDtypeStruct((b, ATT_KV_HEADS, ATT_HEAD_DIM + ONES_ROWS, n), _BF16),
                   jax.ShapeDtypeStruct((b, n, qw), _BF16)],
        compiler_params=_params(2),
        name="attn_inproj",
    )(x, shift, scale, w, q_scale, k_scale, cos, sin)


def _ret_project(h, w_ref, cos, sin, qk_ref, v_ref, sg_ref, rows):
    dk = RET_QK_DIM
    half = dk // 2
    qkw = 2 * RET_HEADS * dk
    vw = RET_HEADS * RET_V_DIM
    for hd in range(2 * RET_HEADS):
        p = jnp.dot(h, w_ref[:, hd * dk:(hd + 1) * dk], preferred_element_type=_F32)
        x1, x2 = p[:, :half], p[:, half:]
        o1 = x1 * cos - x2 * sin
        o2 = x1 * sin + x2 * cos
        if hd >= RET_HEADS:
            o1 = o1 * dk ** -0.5
            o2 = o2 * dk ** -0.5
        qk_ref[0, rows, hd * dk:hd * dk + half] = o1.astype(_BF16)
        qk_ref[0, rows, hd * dk + half:(hd + 1) * dk] = o2.astype(_BF16)
    for j in range(vw // MXU_COLS):
        v_ref[0, rows, j * MXU_COLS:(j + 1) * MXU_COLS] = _slab_dot(h, w_ref, qkw + j * MXU_COLS).astype(_BF16)
    for j in range(vw // MXU_COLS):
        g = _slab_dot(h, w_ref, qkw + vw + j * MXU_COLS)
        sg_ref[0, rows, j * MXU_COLS:(j + 1) * MXU_COLS] = _silu(g).astype(_BF16)


def _flash_kernel(*refs, seg_chunks, tq, n_sub):
    q_ref, sg_ref = refs[0], refs[1]
    kv_refs = refs[2:-1]
    t_ref = refs[-1]
    dh = ATT_HEAD_DIM
    grp = ATT_GROUP
    chunks = []
    for si, tk in enumerate(seg_chunks):
        k_ref, vt_ref = kv_refs[2 * si], kv_refs[2 * si + 1]
        chunks += [(k_ref, vt_ref, c * tk, tk) for c in range(k_ref.shape[1] // tk)]
    units = [(sb, ci, i) for ci in range(len(chunks)) for sb in range(n_sub) for i in range(grp)]

    def scores(sb, ci, i):
        k_ref, _, start, tk = chunks[ci]
        q = q_ref[0, sb * tq:(sb + 1) * tq, i * dh:(i + 1) * dh]
        return lax.dot_general(k_ref[0, start:start + tk, :], q, (((1,), (1,)), ((), ())),
                               preferred_element_type=_F32)

    m, acc = {}, {}
    pending = [scores(*u) for u in units[:FLASH_PREFETCH]]
    for n, (sb, ci, i) in enumerate(units):
        _, vt_ref, start, tk = chunks[ci]
        s = pending.pop(0)
        if n + FLASH_PREFETCH < len(units):
            pending.append(scores(*units[n + FLASH_PREFETCH]))
        if ci == 0:
            m[sb, i] = jnp.full((1, tq), -jnp.inf, _F32)
            acc[sb, i] = jnp.zeros((dh + ONES_ROWS, tq), _F32)
        m_new = jnp.maximum(m[sb, i], jnp.max(s, axis=0, keepdims=True))
        alpha = jnp.exp2(m[sb, i] - m_new)
        p = jnp.exp2(s - m_new).astype(_BF16)
        acc[sb, i] = alpha * acc[sb, i] + jnp.dot(vt_ref[0, 0, :, start:start + tk], p,
                                                  preferred_element_type=_F32)
        m[sb, i] = m_new
        if ci == len(chunks) - 1:
            a = acc.pop((sb, i))
            o_t = a[:dh] * (1.0 / a[dh:dh + 1])
            gate = sg_ref[0, sb * tq:(sb + 1) * tq, i * dh:(i + 1) * dh].astype(_F32)
            t_ref[0, sb * tq:(sb + 1) * tq, i * dh:(i + 1) * dh] = (o_t.T * gate).astype(_BF16)


def _flash(q, sg, kv_segments, tq, n_sub=1):
    b, nq, qw = q.shape
    dh = ATT_HEAD_DIM
    gw = ATT_GROUP * dh
    qmap = lambda bi, hi, i: (bi, i, hi)
    kmap = lambda bi, hi, i: (bi, 0, hi)
    vmap = lambda bi, hi, i: (bi, hi, 0, 0)
    tstep = tq * n_sub
    in_specs = [pl.BlockSpec((1, tstep, gw), qmap), pl.BlockSpec((1, tstep, gw), qmap)]
    args = [q, sg]
    for k, vt, _ in kv_segments:
        in_specs += [pl.BlockSpec((1, k.shape[1], dh), kmap),
                     pl.BlockSpec((1, 1, dh + ONES_ROWS, vt.shape[3]), vmap)]
        args += [k, vt]
    return pl.pallas_call(
        functools.partial(_flash_kernel, seg_chunks=tuple(tk for _, _, tk in kv_segments), tq=tq, n_sub=n_sub),
        grid=(b, ATT_KV_HEADS, nq // tstep),
        in_specs=in_specs,
        out_specs=pl.BlockSpec((1, tstep, gw), qmap),
        out_shape=jax.ShapeDtypeStruct((b, nq, qw), _BF16),
        compiler_params=_params(3),
        name="flash_attention",
    )(*args)


def _retention_kernel(lgf_ref, lgb_ref, qc_ref, kc_ref, vc_ref, sgc_ref, ql_ref, kl_ref, vl_ref, sgl_ref,
                      tc_ref, tl_ref, state_f, state_b, fc_ref, fl_ref, *, chunk, unroll):
    hd = pl.program_id(1)
    c = chunk
    n_ctx = qc_ref.shape[1] // c
    n_lat = ql_ref.shape[1] // c
    ii = lax.broadcasted_iota(jnp.int32, (c, c), 0)
    jj = lax.broadcasted_iota(jnp.int32, (c, c), 1)
    row = lax.broadcasted_iota(jnp.int32, (c, 1), 0).astype(_F32)

    def tables(lg, forward):
        diff = (ii - jj) if forward else (jj - ii)
        keep = (diff >= 0) if forward else (diff > 0)
        decay = jnp.where(keep, jnp.exp(lg * jnp.maximum(diff, 0).astype(_F32)), 0.0)
        if forward:
            xi = jnp.exp(lg * (row + 1.0))
            zeta = jnp.exp(lg * (c - 1.0 - row))
        else:
            xi = jnp.exp(lg * (c - row))
            zeta = jnp.exp(lg * row)
        g_chunk = jnp.exp(jnp.full((1, 1), lg * c, _F32))
        return decay, xi, zeta, g_chunk

    def step(q_ref, k_ref, v_ref, sg_ref, f_ref, t_ref, start, tabs, state_ref, final):
        decay, xi, zeta, g_chunk = tabs
        sl = pl.ds(start, c)
        q = q_ref[0, sl, :]
        k = k_ref[0, sl, :]
        v = v_ref[0, sl, :]
        state = state_ref[...]
        s = lax.dot_general(q, k, (((1,), (1,)), ((), ())), preferred_element_type=_F32) * decay
        o = jnp.dot(s.astype(_BF16), v, preferred_element_type=_F32)
        o = o + jnp.dot(q, state.astype(_BF16), preferred_element_type=_F32) * xi
        kz = (k.astype(_F32) * zeta).astype(_BF16)
        upd = lax.dot_general(kz, v, (((0,), (0,)), ((), ())), preferred_element_type=_F32)
        state_ref[...] = state * g_chunk + upd
        if not final:
            f_ref[sl, :] = o
        else:
            o = o + f_ref[sl, :]
            mu = jnp.mean(o, axis=-1, keepdims=True)
            oc = o - mu
            var = jnp.mean(oc * oc, axis=-1, keepdims=True)
            on = (oc * lax.rsqrt(var + GN_EPS)).astype(_BF16)
            t_ref[0, sl, :] = on * sg_ref[0, sl, :]

    tabs_f = tables(lgf_ref[hd], True)
    tabs_b = tables(lgb_ref[hd], False)
    state_f[...] = jnp.zeros_like(state_f)
    state_b[...] = jnp.zeros_like(state_b)
    for i in range(n_ctx):
        step(qc_ref, kc_ref, vc_ref, sgc_ref, fc_ref, tc_ref, i * c, tabs_f, state_f, False)
    for i in reversed(range(n_ctx)):
        step(qc_ref, kc_ref, vc_ref, sgc_ref, fc_ref, tc_ref, i * c, tabs_b, state_b, True)

    half = n_lat // 2

    def lat_body(final):
        def body(i, carry):
            lo = pl.multiple_of(i * c, c)
            hi = pl.multiple_of((n_lat - 1 - i) * c, c)
            step(ql_ref, kl_ref, vl_ref, sgl_ref, fl_ref, tl_ref, lo, tabs_f, state_f, final)
            step(ql_ref, kl_ref, vl_ref, sgl_ref, fl_ref, tl_ref, hi, tabs_b, state_b, final)
            return carry
        return body

    lax.fori_loop(0, half, lat_body(False), 0, unroll=min(2 * unroll, max(half, 1)))
    lax.fori_loop(half, n_lat, lat_body(True), 0, unroll=unroll)


def _retention(lg_f, lg_b, qk_c, v_c, sg_c, qk_l, v_l, sg_l, chunk):
    b, n_lat, _ = qk_l.shape
    n_ctx = qk_c.shape[1]
    assert (n_lat // chunk) % 2 == 0, "the two scan directions meet in the middle of the latents"
    dk, dv, nh = RET_QK_DIM, RET_V_DIM, RET_HEADS
    qmap = lambda bi, hi: (bi, 0, hi)
    kmap = lambda bi, hi: (bi, 0, nh + hi)
    smem = pl.BlockSpec(memory_space=pltpu.SMEM)
    unroll = math.gcd(n_lat // chunk // 2, 4)
    return pl.pallas_call(
        functools.partial(_retention_kernel, chunk=chunk, unroll=unroll),
        grid=(b, nh),
        in_specs=[smem, smem,
                  pl.BlockSpec((1, n_ctx, dk), qmap), pl.BlockSpec((1, n_ctx, dk), kmap),
                  pl.BlockSpec((1, n_ctx, dv), qmap), pl.BlockSpec((1, n_ctx, dv), qmap),
                  pl.BlockSpec((1, n_lat, dk), qmap), pl.BlockSpec((1, n_lat, dk), kmap),
                  pl.BlockSpec((1, n_lat, dv), qmap), pl.BlockSpec((1, n_lat, dv), qmap)],
        out_specs=[pl.BlockSpec((1, n_ctx, dv), qmap), pl.BlockSpec((1, n_lat, dv), qmap)],
        out_shape=[jax.ShapeDtypeStruct((b, n_ctx, nh * dv), _BF16),
                   jax.ShapeDtypeStruct((b, n_lat, nh * dv), _BF16)],
        scratch_shapes=[pltpu.VMEM((dk, dv), _F32),
                        pltpu.VMEM((dk, dv), _F32),
                        pltpu.VMEM((n_ctx, dv), _F32),
                        pltpu.VMEM((n_lat, dv), _F32)],
        compiler_params=_params(2),
        name="retention",
    )(lg_f, lg_b, qk_c, qk_c, v_c, sg_c, qk_l, qk_l, v_l, sg_l)


def _layer_norm_slab(x_ref, gate_ref, lng_ref, lnb_ref, rows, y, eps):
    z = x_ref[0, rows, :] + gate_ref[0] * y
    mu = jnp.mean(z, axis=-1, keepdims=True)
    zc = z - mu
    var = jnp.mean(zc * zc, axis=-1, keepdims=True)
    return zc * lax.rsqrt(var + eps) * lng_ref[...] + lnb_ref[...]


def _outproj_kernel(t_ref, x_ref, w_ref, gate_ref, lng_ref, lnb_ref, o_ref, *, eps, sub):
    n_sub = t_ref.shape[1] // sub
    proj = lambda r: jnp.dot(t_ref[0, r * sub:(r + 1) * sub, :], w_ref[...], preferred_element_type=_F32)
    y_next = proj(0)
    for r in range(n_sub):
        y = y_next
        if r + 1 < n_sub:
            y_next = proj(r + 1)
        rows = slice(r * sub, (r + 1) * sub)
        o_ref[0, rows, :] = _layer_norm_slab(x_ref, gate_ref, lng_ref, lnb_ref, rows, y, eps)


def _outproj(t, x, w, layer, gate, ln_g, ln_b, alpha, tm):
    b, n, d = x.shape
    row = lambda bi, i: (bi, i, 0)
    const = lambda bi, i: (0, 0)
    return pl.pallas_call(
        functools.partial(_outproj_kernel, eps=LN_EPS / alpha ** 2, sub=math.gcd(tm, OUT_SUB_ROWS)),
        grid=(b, n // tm),
        in_specs=[pl.BlockSpec((1, tm, t.shape[2]), row),
                  pl.BlockSpec((1, tm, d), row),
                  pl.BlockSpec((None,) + w.shape[1:], lambda bi, i: (layer, 0, 0)),
                  pl.BlockSpec((1, 1, d), lambda bi, i: (bi, 0, 0)),
                  pl.BlockSpec((1, d), const),
                  pl.BlockSpec((1, d), const)],
        out_specs=pl.BlockSpec((1, tm, d), row),
        out_shape=jax.ShapeDtypeStruct((b, n, d), _F32),
        compiler_params=_params(2),
        name="outproj",
    )(t, x, w, gate, ln_g, ln_b)


def _outproj_ret_inproj_kernel(t_ref, x_ref, wo_ref, gate_ref, lng_ref, lnb_ref, shift_ref, scale_ref, wi_ref,
                               cos_ref, sin_ref, xo_ref, qk_ref, v_ref, sg_ref, *, eps, sub):
    n_sub = t_ref.shape[1] // sub
    proj = lambda r: jnp.dot(t_ref[0, r * sub:(r + 1) * sub, :], wo_ref[...], preferred_element_type=_F32)
    y_next = proj(0)
    for r in range(n_sub):
        y = y_next
        if r + 1 < n_sub:
            y_next = proj(r + 1)
        rows = slice(r * sub, (r + 1) * sub)
        x_new = _layer_norm_slab(x_ref, gate_ref, lng_ref, lnb_ref, rows, y, eps)
        xo_ref[0, rows, :] = x_new
        h = _modulated(x_new, shift_ref, scale_ref)
        _ret_project(h, wi_ref, cos_ref[rows, :], sin_ref[rows, :], qk_ref, v_ref, sg_ref, rows)


def _outproj_ret_inproj(t, x, wo, lo, gate, ln_g, ln_b, alpha, shift, scale, wi, li, cos, sin, tm):
    b, n, d = x.shape
    qkw = 2 * RET_HEADS * RET_QK_DIM
    vw = RET_HEADS * RET_V_DIM
    half = RET_QK_DIM // 2
    row = lambda bi, i: (bi, i, 0)
    vec = lambda bi, i: (bi, 0, 0)
    const = lambda bi, i: (0, 0)
    return pl.pallas_call(
        functools.partial(_outproj_ret_inproj_kernel, eps=LN_EPS / alpha ** 2, sub=math.gcd(tm, OUT_SUB_ROWS)),
        grid=(b, n // tm),
        in_specs=[pl.BlockSpec((1, tm, t.shape[2]), row),
                  pl.BlockSpec((1, tm, d), row),
                  pl.BlockSpec((None,) + wo.shape[1:], lambda bi, i: (lo, 0, 0), pipeline_mode=pl.Buffered(1)),
                  pl.BlockSpec((1, 1, d), vec),
                  pl.BlockSpec((1, d), const),
                  pl.BlockSpec((1, d), const),
                  pl.BlockSpec((1, 1, d), vec),
                  pl.BlockSpec((1, 1, d), vec),
                  pl.BlockSpec((None,) + wi.shape[1:], lambda bi, i: (li, 0, 0), pipeline_mode=pl.Buffered(1)),
                  pl.BlockSpec((tm, half), lambda bi, i: (i, 0)),
                  pl.BlockSpec((tm, half), lambda bi, i: (i, 0))],
        out_specs=[pl.BlockSpec((1, tm, d), row),
                   pl.BlockSpec((1, tm, qkw), row),
                   pl.BlockSpec((1, tm, vw), row),
                   pl.BlockSpec((1, tm, vw), row)],
        out_shape=[jax.ShapeDtypeStruct((b, n, d), _F32),
                   jax.ShapeDtypeStruct((b, n, qkw), _BF16),
                   jax.ShapeDtypeStruct((b, n, vw), _BF16),
                   jax.ShapeDtypeStruct((b, n, vw), _BF16)],
        compiler_params=_params(2),
        name="outproj_ret_inproj",
    )(t, x, wo, gate, ln_g, ln_b, shift, scale, wi, cos, sin)


def _outproj_attn_inproj_kernel(t_ref, x_ref, wo_ref, gate_ref, lng_ref, lnb_ref, shift_ref, scale_ref, wi_ref,
                                qs_ref, ks_ref, cos_ref, sin_ref, xo_ref, q_ref, k_ref, vt_ref, sg_ref, *, eps, sub):
    n_sub = t_ref.shape[1] // sub
    proj = lambda r: jnp.dot(t_ref[0, r * sub:(r + 1) * sub, :], wo_ref[...], preferred_element_type=_F32)
    y_next = proj(0)
    for r in range(n_sub):
        y = y_next
        if r + 1 < n_sub:
            y_next = proj(r + 1)
        rows = slice(r * sub, (r + 1) * sub)
        x_new = _layer_norm_slab(x_ref, gate_ref, lng_ref, lnb_ref, rows, y, eps)
        xo_ref[0, rows, :] = x_new
        h = _modulated(x_new, shift_ref, scale_ref)
        _attn_project(h, wi_ref, qs_ref[...], ks_ref[...], cos_ref[rows, :], sin_ref[rows, :],
                      q_ref, k_ref, vt_ref, sg_ref, rows, Q_PREMUL)


def _outproj_attn_inproj(t, x, wo, lo, gate, ln_g, ln_b, alpha, shift, scale, wi, li, q_scale, k_scale, cos, sin, tm):
    b, n, d = x.shape
    qw = ATT_HEADS * ATT_HEAD_DIM
    kw = ATT_KV_HEADS * ATT_HEAD_DIM
    row = lambda bi, i: (bi, i, 0)
    vec = lambda bi, i: (bi, 0, 0)
    const = lambda bi, i: (0, 0)
    return pl.pallas_call(
        functools.partial(_outproj_attn_inproj_kernel, eps=LN_EPS / alpha ** 2, sub=math.gcd(tm, OUT_SUB_ROWS)),
        grid=(b, n // tm),
        in_specs=[pl.BlockSpec((1, tm, t.shape[2]), row),
                  pl.BlockSpec((1, tm, d), row),
                  pl.BlockSpec((None,) + wo.shape[1:], lambda bi, i: (lo, 0, 0), pipeline_mode=pl.Buffered(1)),
                  pl.BlockSpec((1, 1, d), vec),
                  pl.BlockSpec((1, d), const),
                  pl.BlockSpec((1, d), const),
                  pl.BlockSpec((1, 1, d), vec),
                  pl.BlockSpec((1, 1, d), vec),
                  pl.BlockSpec((None,) + wi.shape[1:], lambda bi, i: (li, 0, 0), pipeline_mode=pl.Buffered(1)),
                  pl.BlockSpec((1, ATT_HEAD_DIM), const),
                  pl.BlockSpec((1, ATT_HEAD_DIM), const),
                  pl.BlockSpec((tm, ATT_HEAD_DIM), lambda bi, i: (i, 0)),
                  pl.BlockSpec((tm, ATT_HEAD_DIM), lambda bi, i: (i, 0))],
        out_specs=[pl.BlockSpec((1, tm, d), row),
                   pl.BlockSpec((1, tm, qw), row),
                   pl.BlockSpec((1, tm, kw), row),
                   pl.BlockSpec((1, ATT_KV_HEADS, ATT_HEAD_DIM + ONES_ROWS, tm), lambda bi, i: (bi, 0, 0, i)),
                   pl.BlockSpec((1, tm, qw), row)],
        out_shape=[jax.ShapeDtypeStruct((b, n, d), _F32),
                   jax.ShapeDtypeStruct((b, n, qw), _BF16),
                   jax.ShapeDtypeStruct((b, n, kw), _BF16),
                   jax.ShapeDtypeStruct((b, ATT_KV_HEADS, ATT_HEAD_DIM + ONES_ROWS, n), _BF16),
                   jax.ShapeDtypeStruct((b, n, qw), _BF16)],
        compiler_params=_params(2),
        name="outproj_attn_inproj",
    )(t, x, wo, gate, ln_g, ln_b, shift, scale, wi, q_scale, k_scale, cos, sin)


def _axial_perm():
    quarter = ATT_HEAD_DIM // 4
    return np.concatenate([np.arange(quarter) + off * quarter for off in (0, 2, 1, 3)])


def _axial_tables(s):
    quarter = ATT_HEAD_DIM // 4
    t = np.arange(s)
    pos = np.stack([t // GRID_W, t % GRID_W], axis=1).astype(np.float32)
    freqs = (ROPE_THETA ** (-np.arange(quarter, dtype=np.float32) / quarter)).astype(np.float32)
    ang = (pos[:, :, None] * freqs[None, None, :]).reshape(s, 2 * quarter)
    cos = np.concatenate([np.cos(ang), np.cos(ang)], axis=-1)
    sin = np.concatenate([-np.sin(ang), np.sin(ang)], axis=-1)
    return cos.astype(np.float32), sin.astype(np.float32)


def _permute_qk_columns(w_in):
    dh = ATT_HEAD_DIM
    qw = ATT_HEADS * dh
    kw = ATT_KV_HEADS * dh
    perm = _axial_perm()
    q_cols = (jnp.arange(ATT_HEADS)[:, None] * dh + perm[None, :]).reshape(-1)
    k_cols = 2 * qw + (jnp.arange(ATT_KV_HEADS)[:, None] * dh + perm[None, :]).reshape(-1)
    cols = jnp.concatenate([q_cols, jnp.arange(qw, 2 * qw), k_cols, jnp.arange(2 * qw + kw, 2 * qw + 2 * kw)])
    return w_in[..., cols]


def _rope_tables(pos):
    half = RET_QK_DIM // 2
    freqs = (ROPE_THETA ** (-np.arange(half, dtype=np.float32) / half)).astype(np.float32)
    ang = pos.astype(np.float32)[:, None] * freqs[None, :]
    return np.cos(ang).astype(np.float32), np.sin(ang).astype(np.float32)


def _row_tile(n, want):
    return want if n % want == 0 else n


def kernel(x, c, ctx, c_ctx, mod_w, mod_b, ln_g, ln_b, attn_w_in, attn_w_out, attn_q_scale, attn_k_scale,
           ret_w_in, ret_w_out, ret_gn_g, ret_log_decay_fwd, ret_log_decay_bwd):
    b, s, d = x.shape
    l = ctx.shape[1]
    depth = mod_w.shape[0]
    alpha = (2.0 * depth) ** 0.25

    rows = 8 * ((b + 1 + 7) // 8)
    cvec = jnp.zeros((rows, d), _F32).at[:b].set(c).at[b].set(c_ctx)
    mods = _modulation(cvec, mod_w, mod_b, alpha)

    cos_ax, sin_ax = _axial_tables(s)
    cos_id, sin_id = np.ones((l, ATT_HEAD_DIM), np.float32), np.zeros((l, ATT_HEAD_DIM), np.float32)
    cos_c, sin_c = _rope_tables(np.arange(l))
    cos_l, sin_l = _rope_tables(l + np.arange(s))
    attn_w_out_b = attn_w_out.astype(_BF16)
    ret_w_in_b = ret_w_in.astype(_BF16)
    ret_w_out_b = (ret_gn_g[:, :, None] * ret_w_out).astype(_BF16)

    tm_l = _row_tile(s, 512)
    tm_c = _row_tile(l, 256)
    tm_out = _row_tile(s, 1024)
    tq = _row_tile(s, 256)
    tk = _row_tile(s, 512)
    chunk = RET_CHUNK if (l % RET_CHUNK == 0 and s % RET_CHUNK == 0) else 128

    def mod_vectors(i):
        lat = [mods[i, :b, None, k * d:(k + 1) * d] for k in range(3)]
        cx = [jnp.broadcast_to(mods[i, b, None, None, k * d:(k + 1) * d], (b, 1, d)) for k in range(3)]
        return lat, cx

    def attn_weights(j):
        w_in = _permute_qk_columns(attn_w_in[j]).astype(_BF16)[None]
        return w_in, attn_q_scale[j][_axial_perm()][None, :], attn_k_scale[j][_axial_perm()][None, :]

    projected = None
    for i in range(depth):
        need_ctx = i < depth - 1
        j = i // 2
        (shift_l, scale_l, gate_l), (shift_c, scale_c, gate_c) = mod_vectors(i)
        lng = ln_g[i][None, :]
        lnb = ln_b[i][None, :]
        if i % 2 == 0:
            w_out = attn_w_out_b
            if projected is None:
                w_in, qs, ks = attn_weights(j)
                projected = (_attn_inproj(x, shift_l, scale_l, w_in, 0, qs, ks, cos_ax, sin_ax, tm_l),
                             _attn_inproj(ctx, shift_c, scale_c, w_in, 0, qs, ks, cos_id, sin_id, tm_c))
            (q_l, k_l, vt_l, sg_l), (q_c, k_c, vt_c, sg_c) = projected
            t_l = _flash(q_l, sg_l, [(k_l, vt_l, tk), (k_c, vt_c, l)], tq,
                         FLASH_SUB_BLOCKS if s % (tq * FLASH_SUB_BLOCKS) == 0 else 1)
            if need_ctx:
                t_c = _flash(q_c, sg_c, [(k_c, vt_c, l)], _row_tile(l, 128))
        else:
            w_out = ret_w_out_b
            (qk_l, v_l, sg_l), (qk_c, v_c, sg_c) = projected
            t_c, t_l = _retention(ret_log_decay_fwd[j], ret_log_decay_bwd[j],
                                  qk_c, v_c, sg_c, qk_l, v_l, sg_l, chunk)
        if i + 1 == depth:
            x = _outproj(t_l, x, w_out, j, gate_l, lng, lnb, alpha, tm_out)
            continue
        (shift_n, scale_n, _), (shift_nc, scale_nc, _) = mod_vectors(i + 1)
        jn = (i + 1) // 2
        head = (w_out, j)
        if (i + 1) % 2 == 1:
            tail = (ret_w_in_b, jn)
            x, *proj_l = _outproj_ret_inproj(t_l, x, *head, gate_l, lng, lnb, alpha, shift_n, scale_n, *tail,
                                             cos_l, sin_l, tm_l)
            ctx, *proj_c = _outproj_ret_inproj(t_c, ctx, *head, gate_c, lng, lnb, alpha, shift_nc, scale_nc, *tail,
                                               cos_c, sin_c, tm_c)
        else:
            w_in, qs, ks = attn_weights(jn)
            tail = (w_in, 0, qs, ks)
            x, *proj_l = _outproj_attn_inproj(t_l, x, *head, gate_l, lng, lnb, alpha, shift_n, scale_n, *tail,
                                              cos_ax, sin_ax, tm_l)
            ctx, *proj_c = _outproj_attn_inproj(t_c, ctx, *head, gate_c, lng, lnb, alpha, shift_nc, scale_nc, *tail,
                                                cos_id, sin_id, tm_c)
        projected = (proj_l, proj_c)
    return x
```

```python
import functools
import math

import jax
import jax.numpy as jnp
import numpy as np
from jax import lax
from jax.experimental import pallas as pl
from jax.experimental.pallas import tpu as pltpu

GRID_W = 64
ROPE_THETA = 10000.0

ATT_HEADS = 8
ATT_KV_HEADS = 2
ATT_GROUP = ATT_HEADS // ATT_KV_HEADS
ATT_HEAD_DIM = 128
ONES_ROWS = 16

RET_HEADS = 4
RET_QK_DIM = 256
RET_V_DIM = 512
RET_CHUNK = 256

LN_EPS = 1e-5
QK_EPS = 1e-6
GN_EPS = 1e-5

MXU_COLS = 256
FLASH_PREFETCH = 4
FLASH_SUB_BLOCKS = 2
OUT_SUB_ROWS = 256
Q_PREMUL = ATT_HEAD_DIM ** -0.5 * math.log2(math.e)
VMEM_LIMIT = 56 * 1024 * 1024

_BF16 = jnp.bfloat16
_F32 = jnp.float32


def _params(n_grid):
    return pltpu.CompilerParams(dimension_semantics=("arbitrary",) * n_grid,
                                vmem_limit_bytes=VMEM_LIMIT)


def _silu(g):
    return g * jax.nn.sigmoid(g)


def _mod_kernel(c_ref, w_ref, b_ref, o_ref, *, gate_mul):
    sc = _silu(c_ref[...])
    out = jnp.dot(sc, w_ref[0], preferred_element_type=_F32) + b_ref[0]
    o_ref[0] = out * jnp.where(pl.program_id(1) == 2, gate_mul, 1.0)


def _modulation(cvec, mod_w, mod_b, alpha):
    depth, d, d3 = mod_w.shape
    r = cvec.shape[0]
    return pl.pallas_call(
        functools.partial(_mod_kernel, gate_mul=1.0 / alpha),
        grid=(depth, d3 // d),
        in_specs=[pl.BlockSpec((r, d), lambda i, j: (0, 0)),
                  pl.BlockSpec((1, d, d), lambda i, j: (i, 0, j)),
                  pl.BlockSpec((1, 1, d), lambda i, j: (i, 0, j))],
        out_specs=pl.BlockSpec((1, r, d), lambda i, j: (i, 0, j)),
        out_shape=jax.ShapeDtypeStruct((depth, r, d3), _F32),
        compiler_params=_params(2),
        name="modulation",
    )(cvec, mod_w, mod_b.reshape(depth, 1, d3))


def _modulated(x, shift_ref, scale_ref):
    return (x * (1.0 + scale_ref[0]) + shift_ref[0]).astype(_BF16)


def _slab_dot(h, w_ref, start):
    return jnp.dot(h, w_ref[:, start:start + MXU_COLS], preferred_element_type=_F32)


def _attn_project(h, w_ref, q_scale, k_scale, cos, sin, q_ref, k_ref, vt_ref, sg_ref, rows, q_premul):
    dh = ATT_HEAD_DIM
    qw = ATT_HEADS * dh
    kw = ATT_KV_HEADS * dh

    def norm_rope_heads(o_ref, scale_row, mul, p):
        ts = [p[:, u * dh:(u + 1) * dh] for u in range(p.shape[1] // dh)]
        ms = [jnp.mean(t * t, axis=-1, keepdims=True) for t in ts]
        rs = [lax.rsqrt(m + QK_EPS) for m in ms]
        ts = [t * r * scale_row for t, r in zip(ts, rs)]
        rolled = [pltpu.roll(t, dh // 2, axis=1) for t in ts]
        ts = [t * cos + r * sin for t, r in zip(ts, rolled)]
        for hd, t in enumerate(ts):
            o_ref[0, rows, hd * dh:(hd + 1) * dh] = (t if mul == 1.0 else t * mul).astype(_BF16)

    dot = lambda start, width: jnp.dot(h, w_ref[:, start:start + width], preferred_element_type=_F32)
    p = dot(2 * qw, 2 * kw)
    norm_rope_heads(k_ref, k_scale, 1.0, p[:, :kw])
    for hd in range(ATT_KV_HEADS):
        vt_ref[0, hd, :dh, rows] = p[:, kw + hd * dh:kw + (hd + 1) * dh].T.astype(_BF16)
        vt_ref[0, hd, dh:, rows] = jnp.ones((ONES_ROWS, p.shape[0]), _BF16)
    norm_rope_heads(q_ref, q_scale, q_premul, dot(0, qw))
    sg_ref[0, rows, :] = _silu(dot(qw, qw)).astype(_BF16)


def _attn_inproj_kernel(x_ref, shift_ref, scale_ref, w_ref, qs_ref, ks_ref, cos_ref, sin_ref,
                        q_ref, k_ref, vt_ref, sg_ref, *, q_premul, sub):
    for r in range(x_ref.shape[1] // sub):
        rows = slice(r * sub, (r + 1) * sub)
        h = _modulated(x_ref[0, rows, :], shift_ref, scale_ref)
        _attn_project(h, w_ref, qs_ref[...], ks_ref[...], cos_ref[rows, :], sin_ref[rows, :],
                      q_ref, k_ref, vt_ref, sg_ref, rows, q_premul)


def _attn_inproj(x, shift, scale, w, layer, q_scale, k_scale, cos, sin, tm):
    b, n, d = x.shape
    qw = ATT_HEADS * ATT_HEAD_DIM
    kw = ATT_KV_HEADS * ATT_HEAD_DIM
    row = lambda bi, i: (bi, i, 0)
    vec = lambda bi, i: (bi, 0, 0)
    const = lambda bi, i: (0, 0)
    return pl.pallas_call(
        functools.partial(_attn_inproj_kernel, q_premul=Q_PREMUL, sub=math.gcd(tm, OUT_SUB_ROWS)),
        grid=(b, n // tm),
        in_specs=[pl.BlockSpec((1, tm, d), row),
                  pl.BlockSpec((1, 1, d), vec),
                  pl.BlockSpec((1, 1, d), vec),
                  pl.BlockSpec((None,) + w.shape[1:], lambda bi, i: (layer, 0, 0)),
                  pl.BlockSpec((1, ATT_HEAD_DIM), const),
                  pl.BlockSpec((1, ATT_HEAD_DIM), const),
                  pl.BlockSpec((tm, ATT_HEAD_DIM), lambda bi, i: (i, 0)),
                  pl.BlockSpec((tm, ATT_HEAD_DIM), lambda bi, i: (i, 0))],
        out_specs=[pl.BlockSpec((1, tm, qw), row),
                   pl.BlockSpec((1, tm, kw), row),
                   pl.BlockSpec((1, ATT_KV_HEADS, ATT_HEAD_DIM + ONES_ROWS, tm), lambda bi, i: (bi, 0, 0, i)),
                   pl.BlockSpec((1, tm, qw), row)],
        out_shape=[jax.ShapeDtypeStruct((b, n, qw), _BF16),
                   jax.ShapeDtypeStruct((b, n, kw), _BF16),
                   jax.ShapeDtypeStruct((b, ATT_KV_HEADS, ATT_HEAD_DIM + ONES_ROWS, n), _BF16),
                   jax.ShapeDtypeStruct((b, n, qw), _BF16)],
        compiler_params=_params(2),
        name="attn_inproj",
    )(x, shift, scale, w, q_scale, k_scale, cos, sin)


def _ret_project(h, w_ref, cos, sin, qk_ref, v_ref, sg_ref, rows):
    dk = RET_QK_DIM
    half = dk // 2
    qkw = 2 * RET_HEADS * dk
    vw = RET_HEADS * RET_V_DIM
    for hd in range(2 * RET_HEADS):
        p = jnp.dot(h, w_ref[:, hd * dk:(hd + 1) * dk], preferred_element_type=_F32)
        x1, x2 = p[:, :half], p[:, half:]
        o1 = x1 * cos - x2 * sin
        o2 = x1 * sin + x2 * cos
        if hd >= RET_HEADS:
            o1 = o1 * dk ** -0.5
            o2 = o2 * dk ** -0.5
        qk_ref[0, rows, hd * dk:hd * dk + half] = o1.astype(_BF16)
        qk_ref[0, rows, hd * dk + half:(hd + 1) * dk] = o2.astype(_BF16)
    for j in range(vw // MXU_COLS):
        v_ref[0, rows, j * MXU_COLS:(j + 1) * MXU_COLS] = _slab_dot(h, w_ref, qkw + j * MXU_COLS).astype(_BF16)
    for j in range(vw // MXU_COLS):
        g = _slab_dot(h, w_ref, qkw + vw + j * MXU_COLS)
        sg_ref[0, rows, j * MXU_COLS:(j + 1) * MXU_COLS] = _silu(g).astype(_BF16)


def _flash_kernel(*refs, seg_chunks, tq, n_sub):
    q_ref, sg_ref = refs[0], refs[1]
    kv_refs = refs[2:-1]
    t_ref = refs[-1]
    dh = ATT_HEAD_DIM
    grp = ATT_GROUP
    chunks = []
    for si, tk in enumerate(seg_chunks):
        k_ref, vt_ref = kv_refs[2 * si], kv_refs[2 * si + 1]
        chunks += [(k_ref, vt_ref, c * tk, tk) for c in range(k_ref.shape[1] // tk)]
    units = [(sb, ci, i) for ci in range(len(chunks)) for sb in range(n_sub) for i in range(grp)]

    def scores(sb, ci, i):
        k_ref, _, start, tk = chunks[ci]
        q = q_ref[0, sb * tq:(sb + 1) * tq, i * dh:(i + 1) * dh]
        return lax.dot_general(k_ref[0, start:start + tk, :], q, (((1,), (1,)), ((), ())),
                               preferred_element_type=_F32)

    m, acc = {}, {}
    pending = [scores(*u) for u in units[:FLASH_PREFETCH]]
    for n, (sb, ci, i) in enumerate(units):
        _, vt_ref, start, tk = chunks[ci]
        s = pending.pop(0)
        if n + FLASH_PREFETCH < len(units):
            pending.append(scores(*units[n + FLASH_PREFETCH]))
        if ci == 0:
            m[sb, i] = jnp.full((1, tq), -jnp.inf, _F32)
            acc[sb, i] = jnp.zeros((dh + ONES_ROWS, tq), _F32)
        m_new = jnp.maximum(m[sb, i], jnp.max(s, axis=0, keepdims=True))
        alpha = jnp.exp2(m[sb, i] - m_new)
        p = jnp.exp2(s - m_new).astype(_BF16)
        acc[sb, i] = alpha * acc[sb, i] + jnp.dot(vt_ref[0, 0, :, start:start + tk], p,
                                                  preferred_element_type=_F32)
        m[sb, i] = m_new
        if ci == len(chunks) - 1:
            a = acc.pop((sb, i))
            o_t = a[:dh] * (1.0 / a[dh:dh + 1])
            gate = sg_ref[0, sb * tq:(sb + 1) * tq, i * dh:(i + 1) * dh].astype(_F32)
            t_ref[0, sb * tq:(sb + 1) * tq, i * dh:(i + 1) * dh] = (o_t.T * gate).astype(_BF16)


def _flash(q, sg, kv_segments, tq, n_sub=1):
    b, nq, qw = q.shape
    dh = ATT_HEAD_DIM
    gw = ATT_GROUP * dh
    qmap = lambda bi, hi, i: (bi, i, hi)
    kmap = lambda bi, hi, i: (bi, 0, hi)
    vmap = lambda bi, hi, i: (bi, hi, 0, 0)
    tstep = tq * n_sub
    in_specs = [pl.BlockSpec((1, tstep, gw), qmap), pl.BlockSpec((1, tstep, gw), qmap)]
    args = [q, sg]
    for k, vt, _ in kv_segments:
        in_specs += [pl.BlockSpec((1, k.shape[1], dh), kmap),
                     pl.BlockSpec((1, 1, dh + ONES_ROWS, vt.shape[3]), vmap)]
        args += [k, vt]
    return pl.pallas_call(
        functools.partial(_flash_kernel, seg_chunks=tuple(tk for _, _, tk in kv_segments), tq=tq, n_sub=n_sub),
        grid=(b, ATT_KV_HEADS, nq // tstep),
        in_specs=in_specs,
        out_specs=pl.BlockSpec((1, tstep, gw), qmap),
        out_shape=jax.ShapeDtypeStruct((b, nq, qw), _BF16),
        compiler_params=_params(3),
        name="flash_attention",
    )(*args)


def _retention_kernel(lgf_ref, lgb_ref, qc_ref, kc_ref, vc_ref, sgc_ref, ql_ref, kl_ref, vl_ref, sgl_ref,
                      tc_ref, tl_ref, state_f, state_b, fc_ref, fl_ref, *, chunk, unroll):
    hd = pl.program_id(1)
    c = chunk
    n_ctx = qc_ref.shape[1] // c
    n_lat = ql_ref.shape[1] // c
    ii = lax.broadcasted_iota(jnp.int32, (c, c), 0)
    jj = lax.broadcasted_iota(jnp.int32, (c, c), 1)
    row = lax.broadcasted_iota(jnp.int32, (c, 1), 0).astype(_F32)

    def tables(lg, forward):
        diff = (ii - jj) if forward else (jj - ii)
        keep = (diff >= 0) if forward else (diff > 0)
        decay = jnp.where(keep, jnp.exp(lg * jnp.maximum(diff, 0).astype(_F32)), 0.0)
        if forward:
            xi = jnp.exp(lg * (row + 1.0))
            zeta = jnp.exp(lg * (c - 1.0 - row))
        else:
            xi = jnp.exp(lg * (c - row))
            zeta = jnp.exp(lg * row)
        g_chunk = jnp.exp(jnp.full((1, 1), lg * c, _F32))
        return decay, xi, zeta, g_chunk

    def step(q_ref, k_ref, v_ref, sg_ref, f_ref, t_ref, start, tabs, state_ref, final):
        decay, xi, zeta, g_chunk = tabs
        sl = pl.ds(start, c)
        q = q_ref[0, sl, :]
        k = k_ref[0, sl, :]
        v = v_ref[0, sl, :]
        state = state_ref[...]
        s = lax.dot_general(q, k, (((1,), (1,)), ((), ())), preferred_element_type=_F32) * decay
        o = jnp.dot(s.astype(_BF16), v, preferred_element_type=_F32)
        o = o + jnp.dot(q, state.astype(_BF16), preferred_element_type=_F32) * xi
        kz = (k.astype(_F32) * zeta).astype(_BF16)
        upd = lax.dot_general(kz, v, (((0,), (0,)), ((), ())), preferred_element_type=_F32)
        state_ref[...] = state * g_chunk + upd
        if not final:
            f_ref[sl, :] = o
        else:
            o = o + f_ref[sl, :]
            mu = jnp.mean(o, axis=-1, keepdims=True)
            oc = o - mu
            var = jnp.mean(oc * oc, axis=-1, keepdims=True)
            on = (oc * lax.rsqrt(var + GN_EPS)).astype(_BF16)
            t_ref[0, sl, :] = on * sg_ref[0, sl, :]

    tabs_f = tables(lgf_ref[hd], True)
    tabs_b = tables(lgb_ref[hd], False)
    state_f[...] = jnp.zeros_like(state_f)
    state_b[...] = jnp.zeros_like(state_b)
    for i in range(n_ctx):
        step(qc_ref, kc_ref, vc_ref, sgc_ref, fc_ref, tc_ref, i * c, tabs_f, state_f, False)
    for i in reversed(range(n_ctx)):
        step(qc_ref, kc_ref, vc_ref, sgc_ref, fc_ref, tc_ref, i * c, tabs_b, state_b, True)

    half = n_lat // 2

    def lat_body(final):
        def body(i, carry):
            lo = pl.multiple_of(i * c, c)
            hi = pl.multiple_of((n_lat - 1 - i) * c, c)
            step(ql_ref, kl_ref, vl_ref, sgl_ref, fl_ref, tl_ref, lo, tabs_f, state_f, final)
            step(ql_ref, kl_ref, vl_ref, sgl_ref, fl_ref, tl_ref, hi, tabs_b, state_b, final)
            return carry
        return body

    lax.fori_loop(0, half, lat_body(False), 0, unroll=min(2 * unroll, max(half, 1)))
    lax.fori_loop(half, n_lat, lat_body(True), 0, unroll=unroll)


def _retention(lg_f, lg_b, qk_c, v_c, sg_c, qk_l, v_l, sg_l, chunk):
    b, n_lat, _ = qk_l.shape
    n_ctx = qk_c.shape[1]
    assert (n_lat // chunk) % 2 == 0, "the two scan directions meet in the middle of the latents"
    dk, dv, nh = RET_QK_DIM, RET_V_DIM, RET_HEADS
    qmap = lambda bi, hi: (bi, 0, hi)
    kmap = lambda bi, hi: (bi, 0, nh + hi)
    smem = pl.BlockSpec(memory_space=pltpu.SMEM)
    unroll = math.gcd(n_lat // chunk // 2, 4)
    return pl.pallas_call(
        functools.partial(_retention_kernel, chunk=chunk, unroll=unroll),
        grid=(b, nh),
        in_specs=[smem, smem,
                  pl.BlockSpec((1, n_ctx, dk), qmap), pl.BlockSpec((1, n_ctx, dk), kmap),
                  pl.BlockSpec((1, n_ctx, dv), qmap), pl.BlockSpec((1, n_ctx, dv), qmap),
                  pl.BlockSpec((1, n_lat, dk), qmap), pl.BlockSpec((1, n_lat, dk), kmap),
                  pl.BlockSpec((1, n_lat, dv), qmap), pl.BlockSpec((1, n_lat, dv), qmap)],
        out_specs=[pl.BlockSpec((1, n_ctx, dv), qmap), pl.BlockSpec((1, n_lat, dv), qmap)],
        out_shape=[jax.ShapeDtypeStruct((b, n_ctx, nh * dv), _BF16),
                   jax.ShapeDtypeStruct((b, n_lat, nh * dv), _BF16)],
        scratch_shapes=[pltpu.VMEM((dk, dv), _F32),
                        pltpu.VMEM((dk, dv), _F32),
                        pltpu.VMEM((n_ctx, dv), _F32),
                        pltpu.VMEM((n_lat, dv), _F32)],
        compiler_params=_params(2),
        name="retention",
    )(lg_f, lg_b, qk_c, qk_c, v_c, sg_c, qk_l, qk_l, v_l, sg_l)


def _layer_norm_slab(x_ref, gate_ref, lng_ref, lnb_ref, rows, y, eps):
    z = x_ref[0, rows, :] + gate_ref[0] * y
    mu = jnp.mean(z, axis=-1, keepdims=True)
    zc = z - mu
    var = jnp.mean(zc * zc, axis=-1, keepdims=True)
    return zc * lax.rsqrt(var + eps) * lng_ref[...] + lnb_ref[...]


def _outproj_kernel(t_ref, x_ref, w_ref, gate_ref, lng_ref, lnb_ref, o_ref, *, eps, sub):
    n_sub = t_ref.shape[1] // sub
    proj = lambda r: jnp.dot(t_ref[0, r * sub:(r + 1) * sub, :], w_ref[...], preferred_element_type=_F32)
    y_next = proj(0)
    for r in range(n_sub):
        y = y_next
        if r + 1 < n_sub:
            y_next = proj(r + 1)
        rows = slice(r * sub, (r + 1) * sub)
        o_ref[0, rows, :] = _layer_norm_slab(x_ref, gate_ref, lng_ref, lnb_ref, rows, y, eps)


def _outproj(t, x, w, layer, gate, ln_g, ln_b, alpha, tm):
    b, n, d = x.shape
    row = lambda bi, i: (bi, i, 0)
    const = lambda bi, i: (0, 0)
    return pl.pallas_call(
        functools.partial(_outproj_kernel, eps=LN_EPS / alpha ** 2, sub=math.gcd(tm, OUT_SUB_ROWS)),
        grid=(b, n // tm),
        in_specs=[pl.BlockSpec((1, tm, t.shape[2]), row),
                  pl.BlockSpec((1, tm, d), row),
                  pl.BlockSpec((None,) + w.shape[1:], lambda bi, i: (layer, 0, 0)),
                  pl.BlockSpec((1, 1, d), lambda bi, i: (bi, 0, 0)),
                  pl.BlockSpec((1, d), const),
                  pl.BlockSpec((1, d), const)],
        out_specs=pl.BlockSpec((1, tm, d), row),
        out_shape=jax.ShapeDtypeStruct((b, n, d), _F32),
        compiler_params=_params(2),
        name="outproj",
    )(t, x, w, gate, ln_g, ln_b)


def _outproj_ret_inproj_kernel(t_ref, x_ref, wo_ref, gate_ref, lng_ref, lnb_ref, shift_ref, scale_ref, wi_ref,
                               cos_ref, sin_ref, xo_ref, qk_ref, v_ref, sg_ref, *, eps, sub):
    n_sub = t_ref.shape[1] // sub
    proj = lambda r: jnp.dot(t_ref[0, r * sub:(r + 1) * sub, :], wo_ref[...], preferred_element_type=_F32)
    y_next = proj(0)
    for r in range(n_sub):
        y = y_next
        if r + 1 < n_sub:
            y_next = proj(r + 1)
        rows = slice(r * sub, (r + 1) * sub)
        x_new = _layer_norm_slab(x_ref, gate_ref, lng_ref, lnb_ref, rows, y, eps)
        xo_ref[0, rows, :] = x_new
        h = _modulated(x_new, shift_ref, scale_ref)
        _ret_project(h, wi_ref, cos_ref[rows, :], sin_ref[rows, :], qk_ref, v_ref, sg_ref, rows)


def _outproj_ret_inproj(t, x, wo, lo, gate, ln_g, ln_b, alpha, shift, scale, wi, li, cos, sin, tm):
    b, n, d = x.shape
    qkw = 2 * RET_HEADS * RET_QK_DIM
    vw = RET_HEADS * RET_V_DIM
    half = RET_QK_DIM // 2
    row = lambda bi, i: (bi, i, 0)
    vec = lambda bi, i: (bi, 0, 0)
    const = lambda bi, i: (0, 0)
    return pl.pallas_call(
        functools.partial(_outproj_ret_inproj_kernel, eps=LN_EPS / alpha ** 2, sub=math.gcd(tm, OUT_SUB_ROWS)),
        grid=(b, n // tm),
        in_specs=[pl.BlockSpec((1, tm, t.shape[2]), row),
                  pl.BlockSpec((1, tm, d), row),
                  pl.BlockSpec((None,) + wo.shape[1:], lambda bi, i: (lo, 0, 0), pipeline_mode=pl.Buffered(1)),
                  pl.BlockSpec((1, 1, d), vec),
                  pl.BlockSpec((1, d), const),
                  pl.BlockSpec((1, d), const),
                  pl.BlockSpec((1, 1, d), vec),
                  pl.BlockSpec((1, 1, d), vec),
                  pl.BlockSpec((None,) + wi.shape[1:], lambda bi, i: (li, 0, 0), pipeline_mode=pl.Buffered(1)),
                  pl.BlockSpec((tm, half), lambda bi, i: (i, 0)),
                  pl.BlockSpec((tm, half), lambda bi, i: (i, 0))],
        out_specs=[pl.BlockSpec((1, tm, d), row),
                   pl.BlockSpec((1, tm, qkw), row),
                   pl.BlockSpec((1, tm, vw), row),
                   pl.BlockSpec((1, tm, vw), row)],
        out_shape=[jax.ShapeDtypeStruct((b, n, d), _F32),
                   jax.ShapeDtypeStruct((b, n, qkw), _BF16),
                   jax.ShapeDtypeStruct((b, n, vw), _BF16),
                   jax.ShapeDtypeStruct((b, n, vw), _BF16)],
        compiler_params=_params(2),
        name="outproj_ret_inproj",
    )(t, x, wo, gate, ln_g, ln_b, shift, scale, wi, cos, sin)


def _outproj_attn_inproj_kernel(t_ref, x_ref, wo_ref, gate_ref, lng_ref, lnb_ref, shift_ref, scale_ref, wi_ref,
                                qs_ref, ks_ref, cos_ref, sin_ref, xo_ref, q_ref, k_ref, vt_ref, sg_ref, *, eps, sub):
    n_sub = t_ref.shape[1] // sub
    proj = lambda r: jnp.dot(t_ref[0, r * sub:(r + 1) * sub, :], wo_ref[...], preferred_element_type=_F32)
    y_next = proj(0)
    for r in range(n_sub):
        y = y_next
        if r + 1 < n_sub:
            y_next = proj(r + 1)
        rows = slice(r * sub, (r + 1) * sub)
        x_new = _layer_norm_slab(x_ref, gate_ref, lng_ref, lnb_ref, rows, y, eps)
        xo_ref[0, rows, :] = x_new
        h = _modulated(x_new, shift_ref, scale_ref)
        _attn_project(h, wi_ref, qs_ref[...], ks_ref[...], cos_ref[rows, :], sin_ref[rows, :],
                      q_ref, k_ref, vt_ref, sg_ref, rows, Q_PREMUL)


def _outproj_attn_inproj(t, x, wo, lo, gate, ln_g, ln_b, alpha, shift, scale, wi, li, q_scale, k_scale, cos, sin, tm):
    b, n, d = x.shape
    qw = ATT_HEADS * ATT_HEAD_DIM
    kw = ATT_KV_HEADS * ATT_HEAD_DIM
    row = lambda bi, i: (bi, i, 0)
    vec = lambda bi, i: (bi, 0, 0)
    const = lambda bi, i: (0, 0)
    return pl.pallas_call(
        functools.partial(_outproj_attn_inproj_kernel, eps=LN_EPS / alpha ** 2, sub=math.gcd(tm, OUT_SUB_ROWS)),
        grid=(b, n // tm),
        in_specs=[pl.BlockSpec((1, tm, t.shape[2]), row),
                  pl.BlockSpec((1, tm, d), row),
                  pl.BlockSpec((None,) + wo.shape[1:], lambda bi, i: (lo, 0, 0), pipeline_mode=pl.Buffered(1)),
                  pl.BlockSpec((1, 1, d), vec),
                  pl.BlockSpec((1, d), const),
                  pl.BlockSpec((1, d), const),
                  pl.BlockSpec((1, 1, d), vec),
                  pl.BlockSpec((1, 1, d), vec),
                  pl.BlockSpec((None,) + wi.shape[1:], lambda bi, i: (li, 0, 0), pipeline_mode=pl.Buffered(1)),
                  pl.BlockSpec((1, ATT_HEAD_DIM), const),
                  pl.BlockSpec((1, ATT_HEAD_DIM), const),
                  pl.BlockSpec((tm, ATT_HEAD_DIM), lambda bi, i: (i, 0)),
                  pl.BlockSpec((tm, ATT_HEAD_DIM), lambda bi, i: (i, 0))],
        out_specs=[pl.BlockSpec((1, tm, d), row),
                   pl.BlockSpec((1, tm, qw), row),
                   pl.BlockSpec((1, tm, kw), row),
                   pl.BlockSpec((1, ATT_KV_HEADS, ATT_HEAD_DIM + ONES_ROWS, tm), lambda bi, i: (bi, 0, 0, i)),
                   pl.BlockSpec((1, tm, qw), row)],
        out_shape=[jax.ShapeDtypeStruct((b, n, d), _F32),
                   jax.ShapeDtypeStruct((b, n, qw), _BF16),
                   jax.ShapeDtypeStruct((b, n, kw), _BF16),
                   jax.ShapeDtypeStruct((b, ATT_KV_HEADS, ATT_HEAD_DIM + ONES_ROWS, n), _BF16),
                   jax.ShapeDtypeStruct((b, n, qw), _BF16)],
        compiler_params=_params(2),
        name="outproj_attn_inproj",
    )(t, x, wo, gate, ln_g, ln_b, shift, scale, wi, q_scale, k_scale, cos, sin)


def _axial_perm():
    quarter = ATT_HEAD_DIM // 4
    return np.concatenate([np.arange(quarter) + off * quarter for off in (0, 2, 1, 3)])


def _axial_tables(s):
    quarter = ATT_HEAD_DIM // 4
    t = np.arange(s)
    pos = np.stack([t // GRID_W, t % GRID_W], axis=1).astype(np.float32)
    freqs = (ROPE_THETA ** (-np.arange(quarter, dtype=np.float32) / quarter)).astype(np.float32)
    ang = (pos[:, :, None] * freqs[None, None, :]).reshape(s, 2 * quarter)
    cos = np.concatenate([np.cos(ang), np.cos(ang)], axis=-1)
    sin = np.concatenate([-np.sin(ang), np.sin(ang)], axis=-1)
    return cos.astype(np.float32), sin.astype(np.float32)


def _permute_qk_columns(w_in):
    dh = ATT_HEAD_DIM
    qw = ATT_HEADS * dh
    kw = ATT_KV_HEADS * dh
    perm = _axial_perm()
    q_cols = (jnp.arange(ATT_HEADS)[:, None] * dh + perm[None, :]).reshape(-1)
    k_cols = 2 * qw + (jnp.arange(ATT_KV_HEADS)[:, None] * dh + perm[None, :]).reshape(-1)
    cols = jnp.concatenate([q_cols, jnp.arange(qw, 2 * qw), k_cols, jnp.arange(2 * qw + kw, 2 * qw + 2 * kw)])
    return w_in[..., cols]


def _rope_tables(pos):
    half = RET_QK_DIM // 2
    freqs = (ROPE_THETA ** (-np.arange(half, dtype=np.float32) / half)).astype(np.float32)
    ang = pos.astype(np.float32)[:, None] * freqs[None, :]
    return np.cos(ang).astype(np.float32), np.sin(ang).astype(np.float32)


def _row_tile(n, want):
    return want if n % want == 0 else n


def kernel(x, c, ctx, c_ctx, mod_w, mod_b, ln_g, ln_b, attn_w_in, attn_w_out, attn_q_scale, attn_k_scale,
           ret_w_in, ret_w_out, ret_gn_g, ret_log_decay_fwd, ret_log_decay_bwd):
    b, s, d = x.shape
    l = ctx.shape[1]
    depth = mod_w.shape[0]
    alpha = (2.0 * depth) ** 0.25

    rows = 8 * ((b + 1 + 7) // 8)
    cvec = jnp.zeros((rows, d), _F32).at[:b].set(c).at[b].set(c_ctx)
    mods = _modulation(cvec, mod_w, mod_b, alpha)

    cos_ax, sin_ax = _axial_tables(s)
    nc = b * l
    ctx = ctx.reshape(1, nc, d)
    cos_id, sin_id = np.ones((nc, ATT_HEAD_DIM), np.float32), np.zeros((nc, ATT_HEAD_DIM), np.float32)
    cos_c, sin_c = (np.tile(t, (b, 1)) for t in _rope_tables(np.arange(l)))
    cos_l, sin_l = _rope_tables(l + np.arange(s))
    attn_w_out_b = attn_w_out.astype(_BF16)
    ret_w_in_b = ret_w_in.astype(_BF16)
    ret_w_out_b = (ret_gn_g[:, :, None] * ret_w_out).astype(_BF16)

    tm_l = _row_tile(s, 512)
    tm_c = _row_tile(nc, 512)
    tm_out = _row_tile(s, 1024)
    tq = _row_tile(s, 256)
    tk = _row_tile(s, 512)
    chunk = RET_CHUNK if (l % RET_CHUNK == 0 and s % RET_CHUNK == 0) else 128

    def mod_vectors(i):
        lat = [mods[i, :b, None, k * d:(k + 1) * d] for k in range(3)]
        cx = [mods[i, b, None, None, k * d:(k + 1) * d] for k in range(3)]
        return lat, cx

    def per_batch(a):
        return a.reshape(b, l, a.shape[-1])

    def per_batch_vt(vt):
        return vt.reshape(vt.shape[1], vt.shape[2], b, l).transpose(2, 0, 1, 3)

    def attn_weights(j):
        w_in = _permute_qk_columns(attn_w_in[j]).astype(_BF16)[None]
        return w_in, attn_q_scale[j][_axial_perm()][None, :], attn_k_scale[j][_axial_perm()][None, :]

    projected = None
    for i in range(depth):
        need_ctx = i < depth - 1
        j = i // 2
        (shift_l, scale_l, gate_l), (shift_c, scale_c, gate_c) = mod_vectors(i)
        lng = ln_g[i][None, :]
        lnb = ln_b[i][None, :]
        if i % 2 == 0:
            w_out = attn_w_out_b
            if projected is None:
                w_in, qs, ks = attn_weights(j)
                projected = (_attn_inproj(x, shift_l, scale_l, w_in, 0, qs, ks, cos_ax, sin_ax, tm_l),
                             _attn_inproj(ctx, shift_c, scale_c, w_in, 0, qs, ks, cos_id, sin_id, tm_c))
            (q_l, k_l, vt_l, sg_l), (q_c, k_c, vt_c, sg_c) = projected
            q_c, k_c, vt_c, sg_c = per_batch(q_c), per_batch(k_c), per_batch_vt(vt_c), per_batch(sg_c)
            t_l = _flash(q_l, sg_l, [(k_l, vt_l, tk), (k_c, vt_c, l)], tq,
                         FLASH_SUB_BLOCKS if s % (tq * FLASH_SUB_BLOCKS) == 0 else 1)
            if need_ctx:
                t_c = _flash(q_c, sg_c, [(k_c, vt_c, l)], _row_tile(l, 128))
        else:
            w_out = ret_w_out_b
            (qk_l, v_l, sg_l), (qk_c, v_c, sg_c) = projected
            t_c, t_l = _retention(ret_log_decay_fwd[j], ret_log_decay_bwd[j],
                                  per_batch(qk_c), per_batch(v_c), per_batch(sg_c), qk_l, v_l, sg_l, chunk)
        if i + 1 == depth:
            x = _outproj(t_l, x, w_out, j, gate_l, lng, lnb, alpha, tm_out)
            continue
        (shift_n, scale_n, _), (shift_nc, scale_nc, _) = mod_vectors(i + 1)
        t_c = t_c.reshape(1, nc, t_c.shape[-1])
        jn = (i + 1) // 2
        head = (w_out, j)
        if (i + 1) % 2 == 1:
            tail = (ret_w_in_b, jn)
            x, *proj_l = _outproj_ret_inproj(t_l, x, *head, gate_l, lng, lnb, alpha, shift_n, scale_n, *tail,
                                             cos_l, sin_l, tm_l)
            ctx, *proj_c = _outproj_ret_inproj(t_c, ctx, *head, gate_c, lng, lnb, alpha, shift_nc, scale_nc, *tail,
                                               cos_c, sin_c, tm_c)
        else:
            w_in, qs, ks = attn_weights(jn)
            tail = (w_in, 0, qs, ks)
            x, *proj_l = _outproj_attn_inproj(t_l, x, *head, gate_l, lng, lnb, alpha, shift_n, scale_n, *tail,
                                              cos_ax, sin_ax, tm_l)
            ctx, *proj_c = _outproj_attn_inproj(t_c, ctx, *head, gate_c, lng, lnb, alpha, shift_nc, scale_nc, *tail,
                                                cos_id, sin_id, tm_c)
        projected = (proj_l, proj_c)
    return x
```

```python
import functools
import math

import jax
import jax.numpy as jnp
import numpy as np
from jax import lax
from jax.experimental import pallas as pl
from jax.experimental.pallas import tpu as pltpu

GRID_W = 64
ROPE_THETA = 10000.0

ATT_HEADS = 8
ATT_KV_HEADS = 2
ATT_GROUP = ATT_HEADS // ATT_KV_HEADS
ATT_HEAD_DIM = 128
ONES_ROWS = 16

RET_HEADS = 4
RET_QK_DIM = 256
RET_V_DIM = 512
RET_CHUNK = 256

LN_EPS = 1e-5
QK_EPS = 1e-6
GN_EPS = 1e-5

MXU_COLS = 256
FLASH_PREFETCH = 4
FLASH_SUB_BLOCKS = 2
OUT_SUB_ROWS = 256
Q_PREMUL = ATT_HEAD_DIM ** -0.5 * math.log2(math.e)
VMEM_LIMIT = 56 * 1024 * 1024

_BF16 = jnp.bfloat16
_F32 = jnp.float32


def _params(n_grid):
    return pltpu.CompilerParams(dimension_semantics=("arbitrary",) * n_grid,
                                vmem_limit_bytes=VMEM_LIMIT)


def _silu(g):
    return g * jax.nn.sigmoid(g)


def _mod_kernel(c_ref, w_ref, b_ref, o_ref, *, gate_mul):
    sc = _silu(c_ref[...])
    out = jnp.dot(sc, w_ref[0], preferred_element_type=_F32) + b_ref[0]
    o_ref[0] = out * jnp.where(pl.program_id(1) == 2, gate_mul, 1.0)


def _modulation(cvec, mod_w, mod_b, alpha):
    depth, d, d3 = mod_w.shape
    r = cvec.shape[0]
    return pl.pallas_call(
        functools.partial(_mod_kernel, gate_mul=1.0 / alpha),
        grid=(depth, d3 // d),
        in_specs=[pl.BlockSpec((r, d), lambda i, j: (0, 0)),
                  pl.BlockSpec((1, d, d), lambda i, j: (i, 0, j)),
                  pl.BlockSpec((1, 1, d), lambda i, j: (i, 0, j))],
        out_specs=pl.BlockSpec((1, r, d), lambda i, j: (i, 0, j)),
        out_shape=jax.ShapeDtypeStruct((depth, r, d3), _F32),
        compiler_params=_params(2),
        name="modulation",
    )(cvec, mod_w, mod_b.reshape(depth, 1, d3))


def _modulated(x, shift_ref, scale_ref):
    return (x * (1.0 + scale_ref[0]) + shift_ref[0]).astype(_BF16)


def _slab_dot(h, w_ref, start):
    return jnp.dot(h, w_ref[:, start:start + MXU_COLS], preferred_element_type=_F32)


def _attn_project(h, w_ref, q_scale, k_scale, cos, sin, q_ref, k_ref, vt_ref, sg_ref, rows, q_premul):
    dh = ATT_HEAD_DIM
    qw = ATT_HEADS * dh
    kw = ATT_KV_HEADS * dh

    def norm_rope_heads(o_ref, scale_row, mul, p):
        ts = [p[:, u * dh:(u + 1) * dh] for u in range(p.shape[1] // dh)]
        ms = [jnp.mean(t * t, axis=-1, keepdims=True) for t in ts]
        rs = [lax.rsqrt(m + QK_EPS) for m in ms]
        ts = [t * r * scale_row for t, r in zip(ts, rs)]
        rolled = [pltpu.roll(t, dh // 2, axis=1) for t in ts]
        ts = [t * cos + r * sin for t, r in zip(ts, rolled)]
        for hd, t in enumerate(ts):
            o_ref[0, rows, hd * dh:(hd + 1) * dh] = (t if mul == 1.0 else t * mul).astype(_BF16)

    dot = lambda start, width: jnp.dot(h, w_ref[:, start:start + width], preferred_element_type=_F32)
    p = dot(2 * qw, 2 * kw)
    norm_rope_heads(k_ref, k_scale, 1.0, p[:, :kw])
    for hd in range(ATT_KV_HEADS):
        vt_ref[0, hd, :dh, rows] = p[:, kw + hd * dh:kw + (hd + 1) * dh].T.astype(_BF16)
        vt_ref[0, hd, dh:, rows] = jnp.ones((ONES_ROWS, p.shape[0]), _BF16)
    norm_rope_heads(q_ref, q_scale, q_premul, dot(0, qw))
    sg_ref[0, rows, :] = _silu(dot(qw, qw)).astype(_BF16)


def _attn_inproj_kernel(x_ref, shift_ref, scale_ref, w_ref, qs_ref, ks_ref, cos_ref, sin_ref,
                        q_ref, k_ref, vt_ref, sg_ref, *, q_premul, sub):
    for r in range(x_ref.shape[1] // sub):
        rows = slice(r * sub, (r + 1) * sub)
        h = _modulated(x_ref[0, rows, :], shift_ref, scale_ref)
        _attn_project(h, w_ref, qs_ref[...], ks_ref[...], cos_ref[rows, :], sin_ref[rows, :],
                      q_ref, k_ref, vt_ref, sg_ref, rows, q_premul)


def _attn_inproj(x, shift, scale, w, layer, q_scale, k_scale, cos, sin, tm):
    b, n, d = x.shape
    qw = ATT_HEADS * ATT_HEAD_DIM
    kw = ATT_KV_HEADS * ATT_HEAD_DIM
    row = lambda bi, i: (bi, i, 0)
    vec = lambda bi, i: (bi, 0, 0)
    const = lambda bi, i: (0, 0)
    return pl.pallas_call(
        functools.partial(_attn_inproj_kernel, q_premul=Q_PREMUL, sub=math.gcd(tm, OUT_SUB_ROWS)),
        grid=(b, n // tm),
        in_specs=[pl.BlockSpec((1, tm, d), row),
                  pl.BlockSpec((1, 1, d), vec),
                  pl.BlockSpec((1, 1, d), vec),
                  pl.BlockSpec((None,) + w.shape[1:], lambda bi, i: (layer, 0, 0)),
                  pl.BlockSpec((1, ATT_HEAD_DIM), const),
                  pl.BlockSpec((1, ATT_HEAD_DIM), const),
                  pl.BlockSpec((tm, ATT_HEAD_DIM), lambda bi, i: (i, 0)),
                  pl.BlockSpec((tm, ATT_HEAD_DIM), lambda bi, i: (i, 0))],
        out_specs=[pl.BlockSpec((1, tm, qw), row),
                   pl.BlockSpec((1, tm, kw), row),
                   pl.BlockSpec((1, ATT_KV_HEADS, ATT_HEAD_DIM + ONES_ROWS, tm), lambda bi, i: (bi, 0, 0, i)),
                   pl.BlockSpec((1, tm, qw), row)],
        out_shape=[jax.ShapeDtypeStruct((b, n, qw), _BF16),
                   jax.ShapeDtypeStruct((b, n, kw), _BF16),
                   jax.ShapeDtypeStruct((b, ATT_KV_HEADS, ATT_HEAD_DIM + ONES_ROWS, n), _BF16),
                   jax.ShapeDtypeStruct((b, n, qw), _BF16)],
        compiler_params=_params(2),
        name="attn_inproj",
    )(x, shift, scale, w, q_scale, k_scale, cos, sin)


def _ret_project(h, w_ref, cos, sin, qk_ref, v_ref, sg_ref, rows):
    dk = RET_QK_DIM
    half = dk // 2
    qkw = 2 * RET_HEADS * dk
    vw = RET_HEADS * RET_V_DIM
    for hd in range(2 * RET_HEADS):
        p = jnp.dot(h, w_ref[:, hd * dk:(hd + 1) * dk], preferred_element_type=_F32)
        x1, x2 = p[:, :half], p[:, half:]
        o1 = x1 * cos - x2 * sin
        o2 = x1 * sin + x2 * cos
        if hd >= RET_HEADS:
            o1 = o1 * dk ** -0.5
            o2 = o2 * dk ** -0.5
        qk_ref[0, rows, hd * dk:hd * dk + half] = o1.astype(_BF16)
        qk_ref[0, rows, hd * dk + half:(hd + 1) * dk] = o2.astype(_BF16)
    for j in range(vw // MXU_COLS):
        v_ref[0, rows, j * MXU_COLS:(j + 1) * MXU_COLS] = _slab_dot(h, w_ref, qkw + j * MXU_COLS).astype(_BF16)
    for j in range(vw // MXU_COLS):
        g = _slab_dot(h, w_ref, qkw + vw + j * MXU_COLS)
        sg_ref[0, rows, j * MXU_COLS:(j + 1) * MXU_COLS] = _silu(g).astype(_BF16)


def _flash_kernel(*refs, seg_chunks, tq, n_sub):
    q_ref, sg_ref = refs[0], refs[1]
    kv_refs = refs[2:-1]
    t_ref = refs[-1]
    dh = ATT_HEAD_DIM
    grp = ATT_GROUP
    chunks = []
    for si, tk in enumerate(seg_chunks):
        k_ref, vt_ref = kv_refs[2 * si], kv_refs[2 * si + 1]
        chunks += [(k_ref, vt_ref, c * tk, tk) for c in range(k_ref.shape[1] // tk)]
    units = [(sb, ci, i) for ci in range(len(chunks)) for sb in range(n_sub) for i in range(grp)]

    def scores(sb, ci, i):
        k_ref, _, start, tk = chunks[ci]
        q = q_ref[0, sb * tq:(sb + 1) * tq, i * dh:(i + 1) * dh]
        return lax.dot_general(k_ref[0, start:start + tk, :], q, (((1,), (1,)), ((), ())),
                               preferred_element_type=_F32)

    m, acc = {}, {}
    pending = [scores(*u) for u in units[:FLASH_PREFETCH]]
    for n, (sb, ci, i) in enumerate(units):
        _, vt_ref, start, tk = chunks[ci]
        s = pending.pop(0)
        if n + FLASH_PREFETCH < len(units):
            pending.append(scores(*units[n + FLASH_PREFETCH]))
        if ci == 0:
            m[sb, i] = jnp.full((1, tq), -jnp.inf, _F32)
            acc[sb, i] = jnp.zeros((dh + ONES_ROWS, tq), _F32)
        m_new = jnp.maximum(m[sb, i], jnp.max(s, axis=0, keepdims=True))
        alpha = jnp.exp2(m[sb, i] - m_new)
        p = jnp.exp2(s - m_new).astype(_BF16)
        acc[sb, i] = alpha * acc[sb, i] + jnp.dot(vt_ref[0, 0, :, start:start + tk], p,
                                                  preferred_element_type=_F32)
        m[sb, i] = m_new
        if ci == len(chunks) - 1:
            a = acc.pop((sb, i))
            o_t = a[:dh] * (1.0 / a[dh:dh + 1])
            gate = sg_ref[0, sb * tq:(sb + 1) * tq, i * dh:(i + 1) * dh].astype(_F32)
            t_ref[0, sb * tq:(sb + 1) * tq, i * dh:(i + 1) * dh] = (o_t.T * gate).astype(_BF16)


def _flash(q, sg, kv_segments, tq, n_sub=1):
    b, nq, qw = q.shape
    dh = ATT_HEAD_DIM
    gw = ATT_GROUP * dh
    qmap = lambda bi, hi, i: (bi, i, hi)
    kmap = lambda bi, hi, i: (bi, 0, hi)
    vmap = lambda bi, hi, i: (bi, hi, 0, 0)
    tstep = tq * n_sub
    in_specs = [pl.BlockSpec((1, tstep, gw), qmap), pl.BlockSpec((1, tstep, gw), qmap)]
    args = [q, sg]
    for k, vt, _ in kv_segments:
        in_specs += [pl.BlockSpec((1, k.shape[1], dh), kmap),
                     pl.BlockSpec((1, 1, dh + ONES_ROWS, vt.shape[3]), vmap)]
        args += [k, vt]
    return pl.pallas_call(
        functools.partial(_flash_kernel, seg_chunks=tuple(tk for _, _, tk in kv_segments), tq=tq, n_sub=n_sub),
        grid=(b, ATT_KV_HEADS, nq // tstep),
        in_specs=in_specs,
        out_specs=pl.BlockSpec((1, tstep, gw), qmap),
        out_shape=jax.ShapeDtypeStruct((b, nq, qw), _BF16),
        compiler_params=_params(3),
        name="flash_attention",
    )(*args)


def _retention_kernel(lgf_ref, lgb_ref, qc_ref, kc_ref, vc_ref, sgc_ref, ql_ref, kl_ref, vl_ref, sgl_ref,
                      tc_ref, tl_ref, state_f, state_b, fc_ref, fl_ref, *, chunk, unroll):
    hd = pl.program_id(1)
    c = chunk
    n_ctx = qc_ref.shape[1] // c
    n_lat = ql_ref.shape[1] // c
    ii = lax.broadcasted_iota(jnp.int32, (c, c), 0)
    jj = lax.broadcasted_iota(jnp.int32, (c, c), 1)
    row = lax.broadcasted_iota(jnp.int32, (c, 1), 0).astype(_F32)

    def tables(lg, forward):
        diff = (ii - jj) if forward else (jj - ii)
        keep = (diff >= 0) if forward else (diff > 0)
        decay = jnp.where(keep, jnp.exp(lg * jnp.maximum(diff, 0).astype(_F32)), 0.0)
        if forward:
            xi = jnp.exp(lg * (row + 1.0))
            zeta = jnp.exp(lg * (c - 1.0 - row))
        else:
            xi = jnp.exp(lg * (c - row))
            zeta = jnp.exp(lg * row)
        g_chunk = jnp.exp(jnp.full((1, 1), lg * c, _F32))
        return decay, xi, zeta, g_chunk

    def step(q_ref, k_ref, v_ref, sg_ref, f_ref, t_ref, start, tabs, state_ref, final):
        decay, xi, zeta, g_chunk = tabs
        sl = pl.ds(start, c)
        q = q_ref[0, sl, :]
        k = k_ref[0, sl, :]
        v = v_ref[0, sl, :]
        state = state_ref[...]
        s = lax.dot_general(q, k, (((1,), (1,)), ((), ())), preferred_element_type=_F32) * decay
        o = jnp.dot(s.astype(_BF16), v, preferred_element_type=_F32)
        o = o + jnp.dot(q, state.astype(_BF16), preferred_element_type=_F32) * xi
        kz = (k.astype(_F32) * zeta).astype(_BF16)
        upd = lax.dot_general(kz, v, (((0,), (0,)), ((), ())), preferred_element_type=_F32)
        state_ref[...] = state * g_chunk + upd
        if not final:
            f_ref[sl, :] = o
        else:
            o = o + f_ref[sl, :]
            mu = jnp.mean(o, axis=-1, keepdims=True)
            oc = o - mu
            var = jnp.mean(oc * oc, axis=-1, keepdims=True)
            on = (oc * lax.rsqrt(var + GN_EPS)).astype(_BF16)
            t_ref[0, sl, :] = on * sg_ref[0, sl, :]

    tabs_f = tables(lgf_ref[hd], True)
    tabs_b = tables(lgb_ref[hd], False)
    state_f[...] = jnp.zeros_like(state_f)
    state_b[...] = jnp.zeros_like(state_b)
    for i in range(n_ctx):
        step(qc_ref, kc_ref, vc_ref, sgc_ref, fc_ref, tc_ref, i * c, tabs_f, state_f, False)
    for i in reversed(range(n_ctx)):
        step(qc_ref, kc_ref, vc_ref, sgc_ref, fc_ref, tc_ref, i * c, tabs_b, state_b, True)

    half = n_lat // 2

    def lat_body(final):
        def body(i, carry):
            lo = pl.multiple_of(i * c, c)
            hi = pl.multiple_of((n_lat - 1 - i) * c, c)
            step(ql_ref, kl_ref, vl_ref, sgl_ref, fl_ref, tl_ref, lo, tabs_f, state_f, final)
            step(ql_ref, kl_ref, vl_ref, sgl_ref, fl_ref, tl_ref, hi, tabs_b, state_b, final)
            return carry
        return body

    lax.fori_loop(0, half, lat_body(False), 0, unroll=min(2 * unroll, max(half, 1)))
    lax.fori_loop(half, n_lat, lat_body(True), 0, unroll=unroll)


def _retention(lg_f, lg_b, qk_c, v_c, sg_c, qk_l, v_l, sg_l, chunk):
    b, n_lat, _ = qk_l.shape
    n_ctx = qk_c.shape[1]
    assert (n_lat // chunk) % 2 == 0, "the two scan directions meet in the middle of the latents"
    dk, dv, nh = RET_QK_DIM, RET_V_DIM, RET_HEADS
    qmap = lambda bi, hi: (bi, 0, hi)
    kmap = lambda bi, hi: (bi, 0, nh + hi)
    smem = pl.BlockSpec(memory_space=pltpu.SMEM)
    unroll = math.gcd(n_lat // chunk // 2, 4)
    return pl.pallas_call(
        functools.partial(_retention_kernel, chunk=chunk, unroll=unroll),
        grid=(b, nh),
        in_specs=[smem, smem,
                  pl.BlockSpec((1, n_ctx, dk), qmap), pl.BlockSpec((1, n_ctx, dk), kmap),
                  pl.BlockSpec((1, n_ctx, dv), qmap), pl.BlockSpec((1, n_ctx, dv), qmap),
                  pl.BlockSpec((1, n_lat, dk), qmap), pl.BlockSpec((1, n_lat, dk), kmap),
                  pl.BlockSpec((1, n_lat, dv), qmap), pl.BlockSpec((1, n_lat, dv), qmap)],
        out_specs=[pl.BlockSpec((1, n_ctx, dv), qmap), pl.BlockSpec((1, n_lat, dv), qmap)],
        out_shape=[jax.ShapeDtypeStruct((b, n_ctx, nh * dv), _BF16),
                   jax.ShapeDtypeStruct((b, n_lat, nh * dv), _BF16)],
        scratch_shapes=[pltpu.VMEM((dk, dv), _F32),
                        pltpu.VMEM((dk, dv), _F32),
                        pltpu.VMEM((n_ctx, dv), _F32),
                        pltpu.VMEM((n_lat, dv), _F32)],
        compiler_params=_params(2),
        name="retention",
    )(lg_f, lg_b, qk_c, qk_c, v_c, sg_c, qk_l, qk_l, v_l, sg_l)


def _layer_norm_slab(x_ref, gate_ref, lng_ref, lnb_ref, rows, y, eps):
    z = x_ref[0, rows, :] + gate_ref[0] * y
    mu = jnp.mean(z, axis=-1, keepdims=True)
    zc = z - mu
    var = jnp.mean(zc * zc, axis=-1, keepdims=True)
    return zc * lax.rsqrt(var + eps) * lng_ref[...] + lnb_ref[...]


def _outproj_kernel(t_ref, x_ref, w_ref, gate_ref, lng_ref, lnb_ref, o_ref, *, eps, sub):
    n_sub = t_ref.shape[1] // sub
    proj = lambda r: jnp.dot(t_ref[0, r * sub:(r + 1) * sub, :], w_ref[...], preferred_element_type=_F32)
    y_next = proj(0)
    for r in range(n_sub):
        y = y_next
        if r + 1 < n_sub:
            y_next = proj(r + 1)
        rows = slice(r * sub, (r + 1) * sub)
        o_ref[0, rows, :] = _layer_norm_slab(x_ref, gate_ref, lng_ref, lnb_ref, rows, y, eps)


def _outproj(t, x, w, layer, gate, ln_g, ln_b, alpha, tm):
    b, n, d = x.shape
    row = lambda bi, i: (bi, i, 0)
    const = lambda bi, i: (0, 0)
    return pl.pallas_call(
        functools.partial(_outproj_kernel, eps=LN_EPS / alpha ** 2, sub=math.gcd(tm, OUT_SUB_ROWS)),
        grid=(b, n // tm),
        in_specs=[pl.BlockSpec((1, tm, t.shape[2]), row),
                  pl.BlockSpec((1, tm, d), row),
                  pl.BlockSpec((None,) + w.shape[1:], lambda bi, i: (layer, 0, 0)),
                  pl.BlockSpec((1, 1, d), lambda bi, i: (bi, 0, 0)),
                  pl.BlockSpec((1, d), const),
                  pl.BlockSpec((1, d), const)],
        out_specs=pl.BlockSpec((1, tm, d), row),
        out_shape=jax.ShapeDtypeStruct((b, n, d), _F32),
        compiler_params=_params(2),
        name="outproj",
    )(t, x, w, gate, ln_g, ln_b)


def _outproj_ret_inproj_kernel(t_ref, x_ref, wo_ref, gate_ref, lng_ref, lnb_ref, shift_ref, scale_ref, wi_ref,
                               cos_ref, sin_ref, xo_ref, qk_ref, v_ref, sg_ref, *, eps, sub):
    n_sub = t_ref.shape[1] // sub
    proj = lambda r: jnp.dot(t_ref[0, r * sub:(r + 1) * sub, :], wo_ref[...], preferred_element_type=_F32)
    y_next = proj(0)
    for r in range(n_sub):
        y = y_next
        if r + 1 < n_sub:
            y_next = proj(r + 1)
        rows = slice(r * sub, (r + 1) * sub)
        x_new = _layer_norm_slab(x_ref, gate_ref, lng_ref, lnb_ref, rows, y, eps)
        xo_ref[0, rows, :] = x_new
        h = _modulated(x_new, shift_ref, scale_ref)
        _ret_project(h, wi_ref, cos_ref[rows, :], sin_ref[rows, :], qk_ref, v_ref, sg_ref, rows)


def _outproj_ret_inproj(t, x, wo, lo, gate, ln_g, ln_b, alpha, shift, scale, wi, li, cos, sin, tm):
    b, n, d = x.shape
    qkw = 2 * RET_HEADS * RET_QK_DIM
    vw = RET_HEADS * RET_V_DIM
    half = RET_QK_DIM // 2
    row = lambda bi, i: (bi, i, 0)
    vec = lambda bi, i: (bi, 0, 0)
    const = lambda bi, i: (0, 0)
    return pl.pallas_call(
        functools.partial(_outproj_ret_inproj_kernel, eps=LN_EPS / alpha ** 2, sub=math.gcd(tm, OUT_SUB_ROWS)),
        grid=(b, n // tm),
        in_specs=[pl.BlockSpec((1, tm, t.shape[2]), row),
                  pl.BlockSpec((1, tm, d), row),
                  pl.BlockSpec((None,) + wo.shape[1:], lambda bi, i: (lo, 0, 0), pipeline_mode=pl.Buffered(1)),
                  pl.BlockSpec((1, 1, d), vec),
                  pl.BlockSpec((1, d), const),
                  pl.BlockSpec((1, d), const),
                  pl.BlockSpec((1, 1, d), vec),
                  pl.BlockSpec((1, 1, d), vec),
                  pl.BlockSpec((None,) + wi.shape[1:], lambda bi, i: (li, 0, 0), pipeline_mode=pl.Buffered(1)),
                  pl.BlockSpec((tm, half), lambda bi, i: (i, 0)),
                  pl.BlockSpec((tm, half), lambda bi, i: (i, 0))],
        out_specs=[pl.BlockSpec((1, tm, d), row),
                   pl.BlockSpec((1, tm, qkw), row),
                   pl.BlockSpec((1, tm, vw), row),
                   pl.BlockSpec((1, tm, vw), row)],
        out_shape=[jax.ShapeDtypeStruct((b, n, d), _F32),
                   jax.ShapeDtypeStruct((b, n, qkw), _BF16),
                   jax.ShapeDtypeStruct((b, n, vw), _BF16),
                   jax.ShapeDtypeStruct((b, n, vw), _BF16)],
        compiler_params=_params(2),
        name="outproj_ret_inproj",
    )(t, x, wo, gate, ln_g, ln_b, shift, scale, wi, cos, sin)


def _outproj_attn_inproj_kernel(t_ref, x_ref, wo_ref, gate_ref, lng_ref, lnb_ref, shift_ref, scale_ref, wi_ref,
                                qs_ref, ks_ref, cos_ref, sin_ref, xo_ref, q_ref, k_ref, vt_ref, sg_ref, *, eps, sub):
    n_sub = t_ref.shape[1] // sub
    proj = lambda r: jnp.dot(t_ref[0, r * sub:(r + 1) * sub, :], wo_ref[...], preferred_element_type=_F32)
    y_next = proj(0)
    for r in range(n_sub):
        y = y_next
        if r + 1 < n_sub:
            y_next = proj(r + 1)
        rows = slice(r * sub, (r + 1) * sub)
        x_new = _layer_norm_slab(x_ref, gate_ref, lng_ref, lnb_ref, rows, y, eps)
        xo_ref[0, rows, :] = x_new
        h = _modulated(x_new, shift_ref, scale_ref)
        _attn_project(h, wi_ref, qs_ref[...], ks_ref[...], cos_ref[rows, :], sin_ref[rows, :],
                      q_ref, k_ref, vt_ref, sg_ref, rows, Q_PREMUL)


def _outproj_attn_inproj(t, x, wo, lo, gate, ln_g, ln_b, alpha, shift, scale, wi, li, q_scale, k_scale, cos, sin, tm):
    b, n, d = x.shape
    qw = ATT_HEADS * ATT_HEAD_DIM
    kw = ATT_KV_HEADS * ATT_HEAD_DIM
    row = lambda bi, i: (bi, i, 0)
    vec = lambda bi, i: (bi, 0, 0)
    const = lambda bi, i: (0, 0)
    return pl.pallas_call(
        functools.partial(_outproj_attn_inproj_kernel, eps=LN_EPS / alpha ** 2, sub=math.gcd(tm, OUT_SUB_ROWS)),
        grid=(b, n // tm),
        in_specs=[pl.BlockSpec((1, tm, t.shape[2]), row),
                  pl.BlockSpec((1, tm, d), row),
                  pl.BlockSpec((None,) + wo.shape[1:], lambda bi, i: (lo, 0, 0), pipeline_mode=pl.Buffered(1)),
                  pl.BlockSpec((1, 1, d), vec),
                  pl.BlockSpec((1, d), const),
                  pl.BlockSpec((1, d), const),
                  pl.BlockSpec((1, 1, d), vec),
                  pl.BlockSpec((1, 1, d), vec),
                  pl.BlockSpec((None,) + wi.shape[1:], lambda bi, i: (li, 0, 0), pipeline_mode=pl.Buffered(1)),
                  pl.BlockSpec((1, ATT_HEAD_DIM), const),
                  pl.BlockSpec((1, ATT_HEAD_DIM), const),
                  pl.BlockSpec((tm, ATT_HEAD_DIM), lambda bi, i: (i, 0)),
                  pl.BlockSpec((tm, ATT_HEAD_DIM), lambda bi, i: (i, 0))],
        out_specs=[pl.BlockSpec((1, tm, d), row),
                   pl.BlockSpec((1, tm, qw), row),
                   pl.BlockSpec((1, tm, kw), row),
                   pl.BlockSpec((1, ATT_KV_HEADS, ATT_HEAD_DIM + ONES_ROWS, tm), lambda bi, i: (bi, 0, 0, i)),
                   pl.BlockSpec((1, tm, qw), row)],
        out_shape=[jax.ShapeDtypeStruct((b, n, d), _F32),
                   jax.ShapeDtypeStruct((b, n, qw), _BF16),
                   jax.ShapeDtypeStruct((b, n, kw), _BF16),
                   jax.ShapeDtypeStruct((b, ATT_KV_HEADS, ATT_HEAD_DIM + ONES_ROWS, n), _BF16),
                   jax.ShapeDtypeStruct((b, n, qw), _BF16)],
        compiler_params=_params(2),
        name="outproj_attn_inproj",
    )(t, x, wo, gate, ln_g, ln_b, shift, scale, wi, q_scale, k_scale, cos, sin)


def _axial_perm():
    quarter = ATT_HEAD_DIM // 4
    return np.concatenate([np.arange(quarter) + off * quarter for off in (0, 2, 1, 3)])


def _axial_tables(s):
    quarter = ATT_HEAD_DIM // 4
    t = np.arange(s)
    pos = np.stack([t // GRID_W, t % GRID_W], axis=1).astype(np.float32)
    freqs = (ROPE_THETA ** (-np.arange(quarter, dtype=np.float32) / quarter)).astype(np.float32)
    ang = (pos[:, :, None] * freqs[None, None, :]).reshape(s, 2 * quarter)
    cos = np.concatenate([np.cos(ang), np.cos(ang)], axis=-1)
    sin = np.concatenate([-np.sin(ang), np.sin(ang)], axis=-1)
    return cos.astype(np.float32), sin.astype(np.float32)


def _permute_qk_columns(w_in):
    dh = ATT_HEAD_DIM
    qw = ATT_HEADS * dh
    kw = ATT_KV_HEADS * dh
    perm = _axial_perm()
    q_cols = (jnp.arange(ATT_HEADS)[:, None] * dh + perm[None, :]).reshape(-1)
    k_cols = 2 * qw + (jnp.arange(ATT_KV_HEADS)[:, None] * dh + perm[None, :]).reshape(-1)
    cols = jnp.concatenate([q_cols, jnp.arange(qw, 2 * qw), k_cols, jnp.arange(2 * qw + kw, 2 * qw + 2 * kw)])
    return w_in[..., cols]


def _rope_tables(pos):
    half = RET_QK_DIM // 2
    freqs = (ROPE_THETA ** (-np.arange(half, dtype=np.float32) / half)).astype(np.float32)
    ang = pos.astype(np.float32)[:, None] * freqs[None, :]
    return np.cos(ang).astype(np.float32), np.sin(ang).astype(np.float32)


def _row_tile(n, want):
    return want if n % want == 0 else n


def kernel(x, c, ctx, c_ctx, mod_w, mod_b, ln_g, ln_b, attn_w_in, attn_w_out, attn_q_scale, attn_k_scale,
           ret_w_in, ret_w_out, ret_gn_g, ret_log_decay_fwd, ret_log_decay_bwd):
    b, s, d = x.shape
    l = ctx.shape[1]
    depth = mod_w.shape[0]
    alpha = (2.0 * depth) ** 0.25

    rows = 8 * ((b + 1 + 7) // 8)
    cvec = jnp.zeros((rows, d), _F32).at[:b].set(c).at[b].set(c_ctx)
    mods = _modulation(cvec, mod_w, mod_b, alpha)

    cos_ax, sin_ax = _axial_tables(s)
    cos_id, sin_id = np.ones((l, ATT_HEAD_DIM), np.float32), np.zeros((l, ATT_HEAD_DIM), np.float32)
    cos_c, sin_c = _rope_tables(np.arange(l))
    cos_l, sin_l = _rope_tables(l + np.arange(s))
    attn_w_out_b = attn_w_out.astype(_BF16)
    ret_w_in_b = ret_w_in.astype(_BF16)
    ret_w_out_b = (ret_gn_g[:, :, None] * ret_w_out).astype(_BF16)

    tm_l = _row_tile(s, 512)
    tm_c = _row_tile(l, 256)
    tm_out = _row_tile(s, 1024)
    tq = _row_tile(s, 256)
    tk = _row_tile(s, 512)
    chunk = RET_CHUNK if (l % RET_CHUNK == 0 and s % RET_CHUNK == 0) else 128

    def mod_vectors(i):
        lat = [mods[i, :b, None, k * d:(k + 1) * d] for k in range(3)]
        cx = [jnp.broadcast_to(mods[i, b, None, None, k * d:(k + 1) * d], (b, 1, d)) for k in range(3)]
        return lat, cx

    def attn_weights(j):
        w_in = _permute_qk_columns(attn_w_in[j]).astype(_BF16)[None]
        return w_in, attn_q_scale[j][_axial_perm()][None, :], attn_k_scale[j][_axial_perm()][None, :]

    projected = None
    for i in range(depth):
        need_ctx = i < depth - 1
        j = i // 2
        (shift_l, scale_l, gate_l), (shift_c, scale_c, gate_c) = mod_vectors(i)
        lng = ln_g[i][None, :]
        lnb = ln_b[i][None, :]
        if i % 2 == 0:
            w_out = attn_w_out_b
            if projected is None:
                w_in, qs, ks = attn_weights(j)
                projected = (_attn_inproj(x, shift_l, scale_l, w_in, 0, qs, ks, cos_ax, sin_ax, tm_l),
                             _attn_inproj(ctx, shift_c, scale_c, w_in, 0, qs, ks, cos_id, sin_id, tm_c))
            (q_l, k_l, vt_l, sg_l), (q_c, k_c, vt_c, sg_c) = projected
            t_l = _flash(q_l, sg_l, [(k_l, vt_l, tk), (k_c, vt_c, l)], tq,
                         FLASH_SUB_BLOCKS if s % (tq * FLASH_SUB_BLOCKS) == 0 else 1)
            if need_ctx:
                t_c = _flash(q_c, sg_c, [(k_c, vt_c, l)], _row_tile(l, 256))
        else:
            w_out = ret_w_out_b
            (qk_l, v_l, sg_l), (qk_c, v_c, sg_c) = projected
            t_c, t_l = _retention(ret_log_decay_fwd[j], ret_log_decay_bwd[j],
                                  qk_c, v_c, sg_c, qk_l, v_l, sg_l, chunk)
        if i + 1 == depth:
            x = _outproj(t_l, x, w_out, j, gate_l, lng, lnb, alpha, tm_out)
            continue
        (shift_n, scale_n, _), (shift_nc, scale_nc, _) = mod_vectors(i + 1)
        jn = (i + 1) // 2
        head = (w_out, j)
        if (i + 1) % 2 == 1:
            tail = (ret_w_in_b, jn)
            x, *proj_l = _outproj_ret_inproj(t_l, x, *head, gate_l, lng, lnb, alpha, shift_n, scale_n, *tail,
                                             cos_l, sin_l, tm_l)
            ctx, *proj_c = _outproj_ret_inproj(t_c, ctx, *head, gate_c, lng, lnb, alpha, shift_nc, scale_nc, *tail,
                                               cos_c, sin_c, tm_c)
        else:
            w_in, qs, ks = attn_weights(jn)
            tail = (w_in, 0, qs, ks)
            x, *proj_l = _outproj_attn_inproj(t_l, x, *head, gate_l, lng, lnb, alpha, shift_n, scale_n, *tail,
                                              cos_ax, sin_ax, tm_l)
            ctx, *proj_c = _outproj_attn_inproj(t_c, ctx, *head, gate_c, lng, lnb, alpha, shift_nc, scale_nc, *tail,
                                                cos_id, sin_id, tm_c)
        projected = (proj_l, proj_c)
    return x
```

```python
import functools
import math

import jax
import jax.numpy as jnp
import numpy as np
from jax import lax
from jax.experimental import pallas as pl
from jax.experimental.pallas import tpu as pltpu

GRID_W = 64
ROPE_THETA = 10000.0

ATT_HEADS = 8
ATT_KV_HEADS = 2
ATT_GROUP = ATT_HEADS // ATT_KV_HEADS
ATT_HEAD_DIM = 128
ONES_ROWS = 16

RET_HEADS = 4
RET_QK_DIM = 256
RET_V_DIM = 512
RET_CHUNK = 256

LN_EPS = 1e-5
QK_EPS = 1e-6
GN_EPS = 1e-5

MXU_COLS = 256
FLASH_PREFETCH = 4
FLASH_SUB_BLOCKS = 2
OUT_SUB_ROWS = 256
Q_PREMUL = ATT_HEAD_DIM ** -0.5 * math.log2(math.e)
VMEM_LIMIT = 56 * 1024 * 1024

_BF16 = jnp.bfloat16
_F32 = jnp.float32


def _params(n_grid):
    return pltpu.CompilerParams(dimension_semantics=("arbitrary",) * n_grid,
                                vmem_limit_bytes=VMEM_LIMIT)


def _silu(g):
    return g * jax.nn.sigmoid(g)


def _mod_kernel(c_ref, w_ref, b_ref, o_ref, *, gate_mul):
    sc = _silu(c_ref[...])
    out = jnp.dot(sc, w_ref[0], preferred_element_type=_F32) + b_ref[0]
    o_ref[0] = out * jnp.where(pl.program_id(1) == 2, gate_mul, 1.0)


def _modulation(cvec, mod_w, mod_b, alpha):
    depth, d, d3 = mod_w.shape
    r = cvec.shape[0]
    return pl.pallas_call(
        functools.partial(_mod_kernel, gate_mul=1.0 / alpha),
        grid=(depth, d3 // d),
        in_specs=[pl.BlockSpec((r, d), lambda i, j: (0, 0)),
                  pl.BlockSpec((1, d, d), lambda i, j: (i, 0, j)),
                  pl.BlockSpec((1, 1, d), lambda i, j: (i, 0, j))],
        out_specs=pl.BlockSpec((1, r, d), lambda i, j: (i, 0, j)),
        out_shape=jax.ShapeDtypeStruct((depth, r, d3), _F32),
        compiler_params=_params(2),
        name="modulation",
    )(cvec, mod_w, mod_b.reshape(depth, 1, d3))


def _modulated(x, shift_ref, scale_ref):
    return (x * (1.0 + scale_ref[0]) + shift_ref[0]).astype(_BF16)


def _slab_dot(h, w_ref, start):
    return jnp.dot(h, w_ref[:, start:start + MXU_COLS], preferred_element_type=_F32)


def _attn_project(h, w_ref, q_scale, k_scale, cos, sin, q_ref, k_ref, vt_ref, sg_ref, rows, q_premul):
    dh = ATT_HEAD_DIM
    qw = ATT_HEADS * dh
    kw = ATT_KV_HEADS * dh

    def norm_rope_heads(o_ref, scale_row, mul, p):
        ts = [p[:, u * dh:(u + 1) * dh] for u in range(p.shape[1] // dh)]
        ms = [jnp.mean(t * t, axis=-1, keepdims=True) for t in ts]
        rs = [lax.rsqrt(m + QK_EPS) for m in ms]
        ts = [t * r * scale_row for t, r in zip(ts, rs)]
        rolled = [pltpu.roll(t, dh // 2, axis=1) for t in ts]
        ts = [t * cos + r * sin for t, r in zip(ts, rolled)]
        for hd, t in enumerate(ts):
            o_ref[0, rows, hd * dh:(hd + 1) * dh] = (t if mul == 1.0 else t * mul).astype(_BF16)

    dot = lambda start, width: jnp.dot(h, w_ref[:, start:start + width], preferred_element_type=_F32)
    p = dot(2 * qw, 2 * kw)
    norm_rope_heads(k_ref, k_scale, 1.0, p[:, :kw])
    for hd in range(ATT_KV_HEADS):
        vt_ref[0, hd, :dh, rows] = p[:, kw + hd * dh:kw + (hd + 1) * dh].T.astype(_BF16)
        vt_ref[0, hd, dh:, rows] = jnp.ones((ONES_ROWS, p.shape[0]), _BF16)
    norm_rope_heads(q_ref, q_scale, q_premul, dot(0, qw))
    sg_ref[0, rows, :] = _silu(dot(qw, qw)).astype(_BF16)


def _attn_inproj_kernel(x_ref, shift_ref, scale_ref, w_ref, qs_ref, ks_ref, cos_ref, sin_ref,
                        q_ref, k_ref, vt_ref, sg_ref, *, q_premul, sub):
    for r in range(x_ref.shape[1] // sub):
        rows = slice(r * sub, (r + 1) * sub)
        h = _modulated(x_ref[0, rows, :], shift_ref, scale_ref)
        _attn_project(h, w_ref, qs_ref[...], ks_ref[...], cos_ref[rows, :], sin_ref[rows, :],
                      q_ref, k_ref, vt_ref, sg_ref, rows, q_premul)


def _attn_inproj(x, shift, scale, w, layer, q_scale, k_scale, cos, sin, tm):
    b, n, d = x.shape
    qw = ATT_HEADS * ATT_HEAD_DIM
    kw = ATT_KV_HEADS * ATT_HEAD_DIM
    row = lambda bi, i: (bi, i, 0)
    vec = lambda bi, i: (bi, 0, 0)
    const = lambda bi, i: (0, 0)
    return pl.pallas_call(
        functools.partial(_attn_inproj_kernel, q_premul=Q_PREMUL, sub=math.gcd(tm, OUT_SUB_ROWS)),
        grid=(b, n // tm),
        in_specs=[pl.BlockSpec((1, tm, d), row),
                  pl.BlockSpec((1, 1, d), vec),
                  pl.BlockSpec((1, 1, d), vec),
                  pl.BlockSpec((None,) + w.shape[1:], lambda bi, i: (layer, 0, 0)),
                  pl.BlockSpec((1, ATT_HEAD_DIM), const),
                  pl.BlockSpec((1, ATT_HEAD_DIM), const),
                  pl.BlockSpec((tm, ATT_HEAD_DIM), lambda bi, i: (i, 0)),
                  pl.BlockSpec((tm, ATT_HEAD_DIM), lambda bi, i: (i, 0))],
        out_specs=[pl.BlockSpec((1, tm, qw), row),
                   pl.BlockSpec((1, tm, kw), row),
                   pl.BlockSpec((1, ATT_KV_HEADS, ATT_HEAD_DIM + ONES_ROWS, tm), lambda bi, i: (bi, 0, 0, i)),
                   pl.BlockSpec((1, tm, qw), row)],
        out_shape=[jax.ShapeDtypeStruct((b, n, qw), _BF16),
                   jax.ShapeDtypeStruct((b, n, kw), _BF16),
                   jax.ShapeDtypeStruct((b, ATT_KV_HEADS, ATT_HEAD_DIM + ONES_ROWS, n), _BF16),
                   jax.ShapeDtypeStruct((b, n, qw), _BF16)],
        compiler_params=_params(2),
        name="attn_inproj",
    )(x, shift, scale, w, q_scale, k_scale, cos, sin)


def _ret_project(h, w_ref, cos, sin, qk_ref, v_ref, sg_ref, rows):
    dk = RET_QK_DIM
    half = dk // 2
    qkw = 2 * RET_HEADS * dk
    vw = RET_HEADS * RET_V_DIM
    for hd in range(2 * RET_HEADS):
        p = jnp.dot(h, w_ref[:, hd * dk:(hd + 1) * dk], preferred_element_type=_F32)
        x1, x2 = p[:, :half], p[:, half:]
        o1 = x1 * cos - x2 * sin
        o2 = x1 * sin + x2 * cos
        if hd >= RET_HEADS:
            o1 = o1 * dk ** -0.5
            o2 = o2 * dk ** -0.5
        qk_ref[0, rows, hd * dk:hd * dk + half] = o1.astype(_BF16)
        qk_ref[0, rows, hd * dk + half:(hd + 1) * dk] = o2.astype(_BF16)
    for j in range(vw // MXU_COLS):
        v_ref[0, rows, j * MXU_COLS:(j + 1) * MXU_COLS] = _slab_dot(h, w_ref, qkw + j * MXU_COLS).astype(_BF16)
    for j in range(vw // MXU_COLS):
        g = _slab_dot(h, w_ref, qkw + vw + j * MXU_COLS)
        sg_ref[0, rows, j * MXU_COLS:(j + 1) * MXU_COLS] = _silu(g).astype(_BF16)


def _flash_kernel(*refs, seg_chunks, tq, n_sub):
    q_ref, sg_ref = refs[0], refs[1]
    kv_refs = refs[2:-1]
    t_ref = refs[-1]
    dh = ATT_HEAD_DIM
    grp = ATT_GROUP
    chunks = []
    for si, tk in enumerate(seg_chunks):
        k_ref, vt_ref = kv_refs[2 * si], kv_refs[2 * si + 1]
        chunks += [(k_ref, vt_ref, c * tk, tk) for c in range(k_ref.shape[1] // tk)]
    units = [(sb, ci, i) for ci in range(len(chunks)) for sb in range(n_sub) for i in range(grp)]

    def scores(sb, ci, i):
        k_ref, _, start, tk = chunks[ci]
        q = q_ref[0, sb * tq:(sb + 1) * tq, i * dh:(i + 1) * dh]
        return lax.dot_general(k_ref[0, start:start + tk, :], q, (((1,), (1,)), ((), ())),
                               preferred_element_type=_F32)

    m, acc = {}, {}
    pending = [scores(*u) for u in units[:FLASH_PREFETCH]]
    for n, (sb, ci, i) in enumerate(units):
        _, vt_ref, start, tk = chunks[ci]
        s = pending.pop(0)
        if n + FLASH_PREFETCH < len(units):
            pending.append(scores(*units[n + FLASH_PREFETCH]))
        if ci == 0:
            m[sb, i] = jnp.full((1, tq), -jnp.inf, _F32)
            acc[sb, i] = jnp.zeros((dh + ONES_ROWS, tq), _F32)
        m_new = jnp.maximum(m[sb, i], jnp.max(s, axis=0, keepdims=True))
        alpha = jnp.exp2(m[sb, i] - m_new)
        p = jnp.exp2(s - m_new).astype(_BF16)
        acc[sb, i] = alpha * acc[sb, i] + jnp.dot(vt_ref[0, 0, :, start:start + tk], p,
                                                  preferred_element_type=_F32)
        m[sb, i] = m_new
        if ci == len(chunks) - 1:
            a = acc.pop((sb, i))
            o_t = a[:dh] * (1.0 / a[dh:dh + 1])
            gate = sg_ref[0, sb * tq:(sb + 1) * tq, i * dh:(i + 1) * dh].astype(_F32)
            t_ref[0, sb * tq:(sb + 1) * tq, i * dh:(i + 1) * dh] = (o_t.T * gate).astype(_BF16)


def _flash(q, sg, kv_segments, tq, n_sub=1):
    b, nq, qw = q.shape
    dh = ATT_HEAD_DIM
    gw = ATT_GROUP * dh
    qmap = lambda bi, hi, i: (bi, i, hi)
    kmap = lambda bi, hi, i: (bi, 0, hi)
    vmap = lambda bi, hi, i: (bi, hi, 0, 0)
    tstep = tq * n_sub
    in_specs = [pl.BlockSpec((1, tstep, gw), qmap), pl.BlockSpec((1, tstep, gw), qmap)]
    args = [q, sg]
    for k, vt, _ in kv_segments:
        in_specs += [pl.BlockSpec((1, k.shape[1], dh), kmap),
                     pl.BlockSpec((1, 1, dh + ONES_ROWS, vt.shape[3]), vmap)]
        args += [k, vt]
    return pl.pallas_call(
        functools.partial(_flash_kernel, seg_chunks=tuple(tk for _, _, tk in kv_segments), tq=tq, n_sub=n_sub),
        grid=(b, ATT_KV_HEADS, nq // tstep),
        in_specs=in_specs,
        out_specs=pl.BlockSpec((1, tstep, gw), qmap),
        out_shape=jax.ShapeDtypeStruct((b, nq, qw), _BF16),
        compiler_params=_params(3),
        name="flash_attention",
    )(*args)


def _retention_kernel(lgf_ref, lgb_ref, qc_ref, kc_ref, vc_ref, sgc_ref, ql_ref, kl_ref, vl_ref, sgl_ref,
                      tc_ref, tl_ref, state_f, state_b, fc_ref, fl_ref, *, chunk, unroll):
    hd = pl.program_id(1)
    c = chunk
    n_ctx = qc_ref.shape[1] // c
    n_lat = ql_ref.shape[1] // c
    ii = lax.broadcasted_iota(jnp.int32, (c, c), 0)
    jj = lax.broadcasted_iota(jnp.int32, (c, c), 1)
    row = lax.broadcasted_iota(jnp.int32, (c, 1), 0).astype(_F32)

    def tables(lg, forward):
        diff = (ii - jj) if forward else (jj - ii)
        keep = (diff >= 0) if forward else (diff > 0)
        decay = jnp.where(keep, jnp.exp(lg * jnp.maximum(diff, 0).astype(_F32)), 0.0)
        if forward:
            xi = jnp.exp(lg * (row + 1.0))
            zeta = jnp.exp(lg * (c - 1.0 - row))
        else:
            xi = jnp.exp(lg * (c - row))
            zeta = jnp.exp(lg * row)
        g_chunk = jnp.exp(jnp.full((1, 1), lg * c, _F32))
        return decay, xi, zeta, g_chunk

    def step(q_ref, k_ref, v_ref, sg_ref, f_ref, t_ref, start, tabs, state_ref, final):
        decay, xi, zeta, g_chunk = tabs
        sl = pl.ds(start, c)
        q = q_ref[0, sl, :]
        k = k_ref[0, sl, :]
        v = v_ref[0, sl, :]
        state = state_ref[...]
        s = lax.dot_general(q, k, (((1,), (1,)), ((), ())), preferred_element_type=_F32) * decay
        o = jnp.dot(s.astype(_BF16), v, preferred_element_type=_F32)
        o = o + jnp.dot(q, state.astype(_BF16), preferred_element_type=_F32) * xi
        kz = (k.astype(_F32) * zeta).astype(_BF16)
        upd = lax.dot_general(kz, v, (((0,), (0,)), ((), ())), preferred_element_type=_F32)
        state_ref[...] = state * g_chunk + upd
        if not final:
            f_ref[sl, :] = o
        else:
            o = o + f_ref[sl, :]
            mu = jnp.mean(o, axis=-1, keepdims=True)
            oc = o - mu
            var = jnp.mean(oc * oc, axis=-1, keepdims=True)
            on = (oc * lax.rsqrt(var + GN_EPS)).astype(_BF16)
            t_ref[0, sl, :] = on * sg_ref[0, sl, :]

    tabs_f = tables(lgf_ref[hd], True)
    tabs_b = tables(lgb_ref[hd], False)
    state_f[...] = jnp.zeros_like(state_f)
    state_b[...] = jnp.zeros_like(state_b)
    for i in range(n_ctx):
        step(qc_ref, kc_ref, vc_ref, sgc_ref, fc_ref, tc_ref, i * c, tabs_f, state_f, False)
    for i in reversed(range(n_ctx)):
        step(qc_ref, kc_ref, vc_ref, sgc_ref, fc_ref, tc_ref, i * c, tabs_b, state_b, True)

    half = n_lat // 2

    def lat_body(final):
        def body(i, carry):
            lo = pl.multiple_of(i * c, c)
            hi = pl.multiple_of((n_lat - 1 - i) * c, c)
            step(ql_ref, kl_ref, vl_ref, sgl_ref, fl_ref, tl_ref, lo, tabs_f, state_f, final)
            step(ql_ref, kl_ref, vl_ref, sgl_ref, fl_ref, tl_ref, hi, tabs_b, state_b, final)
            return carry
        return body

    lax.fori_loop(0, half, lat_body(False), 0, unroll=min(2 * unroll, max(half, 1)))
    lax.fori_loop(half, n_lat, lat_body(True), 0, unroll=unroll)


def _retention(lg_f, lg_b, qk_c, v_c, sg_c, qk_l, v_l, sg_l, chunk):
    b, n_lat, _ = qk_l.shape
    n_ctx = qk_c.shape[1]
    assert (n_lat // chunk) % 2 == 0, "the two scan directions meet in the middle of the latents"
    dk, dv, nh = RET_QK_DIM, RET_V_DIM, RET_HEADS
    qmap = lambda bi, hi: (bi, 0, hi)
    kmap = lambda bi, hi: (bi, 0, nh + hi)
    smem = pl.BlockSpec(memory_space=pltpu.SMEM)
    unroll = math.gcd(n_lat // chunk // 2, 4)
    return pl.pallas_call(
        functools.partial(_retention_kernel, chunk=chunk, unroll=unroll),
        grid=(b, nh),
        in_specs=[smem, smem,
                  pl.BlockSpec((1, n_ctx, dk), qmap), pl.BlockSpec((1, n_ctx, dk), kmap),
                  pl.BlockSpec((1, n_ctx, dv), qmap), pl.BlockSpec((1, n_ctx, dv), qmap),
                  pl.BlockSpec((1, n_lat, dk), qmap), pl.BlockSpec((1, n_lat, dk), kmap),
                  pl.BlockSpec((1, n_lat, dv), qmap), pl.BlockSpec((1, n_lat, dv), qmap)],
        out_specs=[pl.BlockSpec((1, n_ctx, dv), qmap), pl.BlockSpec((1, n_lat, dv), qmap)],
        out_shape=[jax.ShapeDtypeStruct((b, n_ctx, nh * dv), _BF16),
                   jax.ShapeDtypeStruct((b, n_lat, nh * dv), _BF16)],
        scratch_shapes=[pltpu.VMEM((dk, dv), _F32),
                        pltpu.VMEM((dk, dv), _F32),
                        pltpu.VMEM((n_ctx, dv), _F32),
                        pltpu.VMEM((n_lat, dv), _F32)],
        compiler_params=_params(2),
        name="retention",
    )(lg_f, lg_b, qk_c, qk_c, v_c, sg_c, qk_l, qk_l, v_l, sg_l)


def _layer_norm_slab(x_ref, gate_ref, lng_ref, lnb_ref, rows, y, eps):
    z = x_ref[0, rows, :] + gate_ref[0] * y
    mu = jnp.mean(z, axis=-1, keepdims=True)
    zc = z - mu
    var = jnp.mean(zc * zc, axis=-1, keepdims=True)
    return zc * lax.rsqrt(var + eps) * lng_ref[...] + lnb_ref[...]


def _outproj_kernel(t_ref, x_ref, w_ref, gate_ref, lng_ref, lnb_ref, o_ref, *, eps, sub):
    n_sub = t_ref.shape[1] // sub
    proj = lambda r: jnp.dot(t_ref[0, r * sub:(r + 1) * sub, :], w_ref[...], preferred_element_type=_F32)
    y_next = proj(0)
    for r in range(n_sub):
        y = y_next
        if r + 1 < n_sub:
            y_next = proj(r + 1)
        rows = slice(r * sub, (r + 1) * sub)
        o_ref[0, rows, :] = _layer_norm_slab(x_ref, gate_ref, lng_ref, lnb_ref, rows, y, eps)


def _outproj(t, x, w, layer, gate, ln_g, ln_b, alpha, tm):
    b, n, d = x.shape
    row = lambda bi, i: (bi, i, 0)
    const = lambda bi, i: (0, 0)
    return pl.pallas_call(
        functools.partial(_outproj_kernel, eps=LN_EPS / alpha ** 2, sub=math.gcd(tm, OUT_SUB_ROWS)),
        grid=(b, n // tm),
        in_specs=[pl.BlockSpec((1, tm, t.shape[2]), row),
                  pl.BlockSpec((1, tm, d), row),
                  pl.BlockSpec((None,) + w.shape[1:], lambda bi, i: (layer, 0, 0)),
                  pl.BlockSpec((1, 1, d), lambda bi, i: (bi, 0, 0)),
                  pl.BlockSpec((1, d), const),
                  pl.BlockSpec((1, d), const)],
        out_specs=pl.BlockSpec((1, tm, d), row),
        out_shape=jax.ShapeDtypeStruct((b, n, d), _F32),
        compiler_params=_params(2),
        name="outproj",
    )(t, x, w, gate, ln_g, ln_b)


def _outproj_ret_inproj_kernel(t_ref, x_ref, wo_ref, gate_ref, lng_ref, lnb_ref, shift_ref, scale_ref, wi_ref,
                               cos_ref, sin_ref, xo_ref, qk_ref, v_ref, sg_ref, *, eps, sub):
    n_sub = t_ref.shape[1] // sub
    proj = lambda r: jnp.dot(t_ref[0, r * sub:(r + 1) * sub, :], wo_ref[...], preferred_element_type=_F32)
    y_next = proj(0)
    for r in range(n_sub):
        y = y_next
        if r + 1 < n_sub:
            y_next = proj(r + 1)
        rows = slice(r * sub, (r + 1) * sub)
        x_new = _layer_norm_slab(x_ref, gate_ref, lng_ref, lnb_ref, rows, y, eps)
        xo_ref[0, rows, :] = x_new
        h = _modulated(x_new, shift_ref, scale_ref)
        _ret_project(h, wi_ref, cos_ref[rows, :], sin_ref[rows, :], qk_ref, v_ref, sg_ref, rows)


def _outproj_ret_inproj(t, x, wo, lo, gate, ln_g, ln_b, alpha, shift, scale, wi, li, cos, sin, tm):
    b, n, d = x.shape
    qkw = 2 * RET_HEADS * RET_QK_DIM
    vw = RET_HEADS * RET_V_DIM
    half = RET_QK_DIM // 2
    row = lambda bi, i: (bi, i, 0)
    vec = lambda bi, i: (bi, 0, 0)
    const = lambda bi, i: (0, 0)
    return pl.pallas_call(
        functools.partial(_outproj_ret_inproj_kernel, eps=LN_EPS / alpha ** 2, sub=math.gcd(tm, OUT_SUB_ROWS)),
        grid=(b, n // tm),
        in_specs=[pl.BlockSpec((1, tm, t.shape[2]), row),
                  pl.BlockSpec((1, tm, d), row),
                  pl.BlockSpec((None,) + wo.shape[1:], lambda bi, i: (lo, 0, 0), pipeline_mode=pl.Buffered(1)),
                  pl.BlockSpec((1, 1, d), vec),
                  pl.BlockSpec((1, d), const),
                  pl.BlockSpec((1, d), const),
                  pl.BlockSpec((1, 1, d), vec),
                  pl.BlockSpec((1, 1, d), vec),
                  pl.BlockSpec((None,) + wi.shape[1:], lambda bi, i: (li, 0, 0), pipeline_mode=pl.Buffered(1)),
                  pl.BlockSpec((tm, half), lambda bi, i: (i, 0)),
                  pl.BlockSpec((tm, half), lambda bi, i: (i, 0))],
        out_specs=[pl.BlockSpec((1, tm, d), row),
                   pl.BlockSpec((1, tm, qkw), row),
                   pl.BlockSpec((1, tm, vw), row),
                   pl.BlockSpec((1, tm, vw), row)],
        out_shape=[jax.ShapeDtypeStruct((b, n, d), _F32),
                   jax.ShapeDtypeStruct((b, n, qkw), _BF16),
                   jax.ShapeDtypeStruct((b, n, vw), _BF16),
                   jax.ShapeDtypeStruct((b, n, vw), _BF16)],
        compiler_params=_params(2),
        name="outproj_ret_inproj",
    )(t, x, wo, gate, ln_g, ln_b, shift, scale, wi, cos, sin)


def _outproj_attn_inproj_kernel(t_ref, x_ref, wo_ref, gate_ref, lng_ref, lnb_ref, shift_ref, scale_ref, wi_ref,
                                qs_ref, ks_ref, cos_ref, sin_ref, xo_ref, q_ref, k_ref, vt_ref, sg_ref, *, eps, sub):
    n_sub = t_ref.shape[1] // sub
    proj = lambda r: jnp.dot(t_ref[0, r * sub:(r + 1) * sub, :], wo_ref[...], preferred_element_type=_F32)
    y_next = proj(0)
    for r in range(n_sub):
        y = y_next
        if r + 1 < n_sub:
            y_next = proj(r + 1)
        rows = slice(r * sub, (r + 1) * sub)
        x_new = _layer_norm_slab(x_ref, gate_ref, lng_ref, lnb_ref, rows, y, eps)
        xo_ref[0, rows, :] = x_new
        h = _modulated(x_new, shift_ref, scale_ref)
        _attn_project(h, wi_ref, qs_ref[...], ks_ref[...], cos_ref[rows, :], sin_ref[rows, :],
                      q_ref, k_ref, vt_ref, sg_ref, rows, Q_PREMUL)


def _outproj_attn_inproj(t, x, wo, lo, gate, ln_g, ln_b, alpha, shift, scale, wi, li, q_scale, k_scale, cos, sin, tm):
    b, n, d = x.shape
    qw = ATT_HEADS * ATT_HEAD_DIM
    kw = ATT_KV_HEADS * ATT_HEAD_DIM
    row = lambda bi, i: (bi, i, 0)
    vec = lambda bi, i: (bi, 0, 0)
    const = lambda bi, i: (0, 0)
    return pl.pallas_call(
        functools.partial(_outproj_attn_inproj_kernel, eps=LN_EPS / alpha ** 2, sub=math.gcd(tm, OUT_SUB_ROWS)),
        grid=(b, n // tm),
        in_specs=[pl.BlockSpec((1, tm, t.shape[2]), row),
                  pl.BlockSpec((1, tm, d), row),
                  pl.BlockSpec((None,) + wo.shape[1:], lambda bi, i: (lo, 0, 0), pipeline_mode=pl.Buffered(1)),
                  pl.BlockSpec((1, 1, d), vec),
                  pl.BlockSpec((1, d), const),
                  pl.BlockSpec((1, d), const),
                  pl.BlockSpec((1, 1, d), vec),
                  pl.BlockSpec((1, 1, d), vec),
                  pl.BlockSpec((None,) + wi.shape[1:], lambda bi, i: (li, 0, 0), pipeline_mode=pl.Buffered(1)),
                  pl.BlockSpec((1, ATT_HEAD_DIM), const),
                  pl.BlockSpec((1, ATT_HEAD_DIM), const),
                  pl.BlockSpec((tm, ATT_HEAD_DIM), lambda bi, i: (i, 0)),
                  pl.BlockSpec((tm, ATT_HEAD_DIM), lambda bi, i: (i, 0))],
        out_specs=[pl.BlockSpec((1, tm, d), row),
                   pl.BlockSpec((1, tm, qw), row),
                   pl.BlockSpec((1, tm, kw), row),
                   pl.BlockSpec((1, ATT_KV_HEADS, ATT_HEAD_DIM + ONES_ROWS, tm), lambda bi, i: (bi, 0, 0, i)),
                   pl.BlockSpec((1, tm, qw), row)],
        out_shape=[jax.ShapeDtypeStruct((b, n, d), _F32),
                   jax.ShapeDtypeStruct((b, n, qw), _BF16),
                   jax.ShapeDtypeStruct((b, n, kw), _BF16),
                   jax.ShapeDtypeStruct((b, ATT_KV_HEADS, ATT_HEAD_DIM + ONES_ROWS, n), _BF16),
                   jax.ShapeDtypeStruct((b, n, qw), _BF16)],
        compiler_params=_params(2),
        name="outproj_attn_inproj",
    )(t, x, wo, gate, ln_g, ln_b, shift, scale, wi, q_scale, k_scale, cos, sin)


def _axial_perm():
    quarter = ATT_HEAD_DIM // 4
    return np.concatenate([np.arange(quarter) + off * quarter for off in (0, 2, 1, 3)])


def _axial_tables(s):
    quarter = ATT_HEAD_DIM // 4
    t = np.arange(s)
    pos = np.stack([t // GRID_W, t % GRID_W], axis=1).astype(np.float32)
    freqs = (ROPE_THETA ** (-np.arange(quarter, dtype=np.float32) / quarter)).astype(np.float32)
    ang = (pos[:, :, None] * freqs[None, None, :]).reshape(s, 2 * quarter)
    cos = np.concatenate([np.cos(ang), np.cos(ang)], axis=-1)
    sin = np.concatenate([-np.sin(ang), np.sin(ang)], axis=-1)
    return cos.astype(np.float32), sin.astype(np.float32)


def _permute_qk_columns(w_in):
    dh = ATT_HEAD_DIM
    qw = ATT_HEADS * dh
    kw = ATT_KV_HEADS * dh
    perm = _axial_perm()
    q_cols = (jnp.arange(ATT_HEADS)[:, None] * dh + perm[None, :]).reshape(-1)
    k_cols = 2 * qw + (jnp.arange(ATT_KV_HEADS)[:, None] * dh + perm[None, :]).reshape(-1)
    cols = jnp.concatenate([q_cols, jnp.arange(qw, 2 * qw), k_cols, jnp.arange(2 * qw + kw, 2 * qw + 2 * kw)])
    return w_in[..., cols]


def _rope_tables(pos):
    half = RET_QK_DIM // 2
    freqs = (ROPE_THETA ** (-np.arange(half, dtype=np.float32) / half)).astype(np.float32)
    ang = pos.astype(np.float32)[:, None] * freqs[None, :]
    return np.cos(ang).astype(np.float32), np.sin(ang).astype(np.float32)


def _row_tile(n, want):
    return want if n % want == 0 else n


def kernel(x, c, ctx, c_ctx, mod_w, mod_b, ln_g, ln_b, attn_w_in, attn_w_out, attn_q_scale, attn_k_scale,
           ret_w_in, ret_w_out, ret_gn_g, ret_log_decay_fwd, ret_log_decay_bwd):
    b, s, d = x.shape
    l = ctx.shape[1]
    depth = mod_w.shape[0]
    alpha = (2.0 * depth) ** 0.25

    rows = 8 * ((b + 1 + 7) // 8)
    cvec = jnp.zeros((rows, d), _F32).at[:b].set(c).at[b].set(c_ctx)
    mods = _modulation(cvec, mod_w, mod_b, alpha)

    cos_ax, sin_ax = _axial_tables(s)
    cos_id, sin_id = np.ones((l, ATT_HEAD_DIM), np.float32), np.zeros((l, ATT_HEAD_DIM), np.float32)
    cos_c, sin_c = _rope_tables(np.arange(l))
    cos_l, sin_l = _rope_tables(l + np.arange(s))
    attn_w_out_b = attn_w_out.astype(_BF16)
    ret_w_in_b = ret_w_in.astype(_BF16)
    ret_w_out_b = (ret_gn_g[:, :, None] * ret_w_out).astype(_BF16)

    tm_l = _row_tile(s, 512)
    tm_attn = _row_tile(s, 1024)
    tm_c = _row_tile(l, 256)
    tm_out = _row_tile(s, 1024)
    tq = _row_tile(s, 256)
    tk = _row_tile(s, 512)
    chunk = RET_CHUNK if (l % RET_CHUNK == 0 and s % RET_CHUNK == 0) else 128

    def mod_vectors(i):
        lat = [mods[i, :b, None, k * d:(k + 1) * d] for k in range(3)]
        cx = [jnp.broadcast_to(mods[i, b, None, None, k * d:(k + 1) * d], (b, 1, d)) for k in range(3)]
        return lat, cx

    def attn_weights(j):
        w_in = _permute_qk_columns(attn_w_in[j]).astype(_BF16)[None]
        return w_in, attn_q_scale[j][_axial_perm()][None, :], attn_k_scale[j][_axial_perm()][None, :]

    projected = None
    for i in range(depth):
        need_ctx = i < depth - 1
        j = i // 2
        (shift_l, scale_l, gate_l), (shift_c, scale_c, gate_c) = mod_vectors(i)
        lng = ln_g[i][None, :]
        lnb = ln_b[i][None, :]
        if i % 2 == 0:
            w_out = attn_w_out_b
            if projected is None:
                w_in, qs, ks = attn_weights(j)
                projected = (_attn_inproj(x, shift_l, scale_l, w_in, 0, qs, ks, cos_ax, sin_ax, tm_attn),
                             _attn_inproj(ctx, shift_c, scale_c, w_in, 0, qs, ks, cos_id, sin_id, tm_c))
            (q_l, k_l, vt_l, sg_l), (q_c, k_c, vt_c, sg_c) = projected
            t_l = _flash(q_l, sg_l, [(k_l, vt_l, tk), (k_c, vt_c, l)], tq,
                         FLASH_SUB_BLOCKS if s % (tq * FLASH_SUB_BLOCKS) == 0 else 1)
            if need_ctx:
                t_c = _flash(q_c, sg_c, [(k_c, vt_c, l)], _row_tile(l, 256))
        else:
            w_out = ret_w_out_b
            (qk_l, v_l, sg_l), (qk_c, v_c, sg_c) = projected
            t_c, t_l = _retention(ret_log_decay_fwd[j], ret_log_decay_bwd[j],
                                  qk_c, v_c, sg_c, qk_l, v_l, sg_l, chunk)
        if i + 1 == depth:
            x = _outproj(t_l, x, w_out, j, gate_l, lng, lnb, alpha, tm_out)
            continue
        (shift_n, scale_n, _), (shift_nc, scale_nc, _) = mod_vectors(i + 1)
        jn = (i + 1) // 2
        head = (w_out, j)
        if (i + 1) % 2 == 1:
            tail = (ret_w_in_b, jn)
            x, *proj_l = _outproj_ret_inproj(t_l, x, *head, gate_l, lng, lnb, alpha, shift_n, scale_n, *tail,
                                             cos_l, sin_l, tm_l)
            ctx, *proj_c = _outproj_ret_inproj(t_c, ctx, *head, gate_c, lng, lnb, alpha, shift_nc, scale_nc, *tail,
                                               cos_c, sin_c, tm_c)
        else:
            w_in, qs, ks = attn_weights(jn)
            tail = (w_in, 0, qs, ks)
            x, *proj_l = _outproj_attn_inproj(t_l, x, *head, gate_l, lng, lnb, alpha, shift_n, scale_n, *tail,
                                              cos_ax, sin_ax, tm_attn)
            ctx, *proj_c = _outproj_attn_inproj(t_c, ctx, *head, gate_c, lng, lnb, alpha, shift_nc, scale_nc, *tail,
                                                cos_id, sin_id, tm_c)
        projected = (proj_l, proj_c)
    return x
```

```python
import functools
import math

import jax
import jax.numpy as jnp
import numpy as np
from jax import lax
from jax.experimental import pallas as pl
from jax.experimental.pallas import tpu as pltpu

GRID_W = 64
ROPE_THETA = 10000.0

ATT_HEADS = 8
ATT_KV_HEADS = 2
ATT_GROUP = ATT_HEADS // ATT_KV_HEADS
ATT_HEAD_DIM = 128
ONES_ROWS = 16

RET_HEADS = 4
RET_QK_DIM = 256
RET_V_DIM = 512
RET_CHUNK = 256

LN_EPS = 1e-5
QK_EPS = 1e-6
GN_EPS = 1e-5

MXU_COLS = 256
FLASH_PREFETCH = 4
FLASH_SUB_BLOCKS = 2
RING_SLOTS = 3
OUT_SUB_ROWS = 256
Q_PREMUL = ATT_HEAD_DIM ** -0.5 * math.log2(math.e)
VMEM_LIMIT = 56 * 1024 * 1024

_BF16 = jnp.bfloat16
_F32 = jnp.float32


def _params(n_grid):
    return pltpu.CompilerParams(dimension_semantics=("arbitrary",) * n_grid,
                                vmem_limit_bytes=VMEM_LIMIT)


def _silu(g):
    return g * jax.nn.sigmoid(g)


def _mod_kernel(c_ref, w_ref, b_ref, o_ref, *, gate_mul):
    sc = _silu(c_ref[...])
    out = jnp.dot(sc, w_ref[0], preferred_element_type=_F32) + b_ref[0]
    o_ref[0] = out * jnp.where(pl.program_id(1) == 2, gate_mul, 1.0)


def _modulation(cvec, mod_w, mod_b, alpha):
    depth, d, d3 = mod_w.shape
    r = cvec.shape[0]
    return pl.pallas_call(
        functools.partial(_mod_kernel, gate_mul=1.0 / alpha),
        grid=(depth, d3 // d),
        in_specs=[pl.BlockSpec((r, d), lambda i, j: (0, 0)),
                  pl.BlockSpec((1, d, d), lambda i, j: (i, 0, j)),
                  pl.BlockSpec((1, 1, d), lambda i, j: (i, 0, j))],
        out_specs=pl.BlockSpec((1, r, d), lambda i, j: (i, 0, j)),
        out_shape=jax.ShapeDtypeStruct((depth, r, d3), _F32),
        compiler_params=_params(2),
        name="modulation",
    )(cvec, mod_w, mod_b.reshape(depth, 1, d3))


def _modulated(x, shift_ref, scale_ref):
    return (x * (1.0 + scale_ref[0]) + shift_ref[0]).astype(_BF16)


def _slab_dot(h, w_ref, start):
    return jnp.dot(h, w_ref[:, start:start + MXU_COLS], preferred_element_type=_F32)


def _attn_project(h, w_ref, q_scale, k_scale, cos, sin, q_ref, k_ref, vt_ref, sg_ref, rows, q_premul):
    dh = ATT_HEAD_DIM
    qw = ATT_HEADS * dh
    kw = ATT_KV_HEADS * dh

    def norm_rope_heads(o_ref, scale_row, mul, p):
        ts = [p[:, u * dh:(u + 1) * dh] for u in range(p.shape[1] // dh)]
        ms = [jnp.mean(t * t, axis=-1, keepdims=True) for t in ts]
        rs = [lax.rsqrt(m + QK_EPS) for m in ms]
        ts = [t * r * scale_row for t, r in zip(ts, rs)]
        rolled = [pltpu.roll(t, dh // 2, axis=1) for t in ts]
        ts = [t * cos + r * sin for t, r in zip(ts, rolled)]
        for hd, t in enumerate(ts):
            o_ref[0, rows, hd * dh:(hd + 1) * dh] = (t if mul == 1.0 else t * mul).astype(_BF16)

    dot = lambda start, width: jnp.dot(h, w_ref[:, start:start + width], preferred_element_type=_F32)
    p = dot(2 * qw, 2 * kw)
    norm_rope_heads(k_ref, k_scale, 1.0, p[:, :kw])
    for hd in range(ATT_KV_HEADS):
        vt_ref[0, hd, :dh, rows] = p[:, kw + hd * dh:kw + (hd + 1) * dh].T.astype(_BF16)
        vt_ref[0, hd, dh:, rows] = jnp.ones((ONES_ROWS, p.shape[0]), _BF16)
    norm_rope_heads(q_ref, q_scale, q_premul, dot(0, qw))
    sg_ref[0, rows, :] = _silu(dot(qw, qw)).astype(_BF16)


def _attn_inproj_kernel(x_ref, shift_ref, scale_ref, w_ref, qs_ref, ks_ref, cos_ref, sin_ref,
                        q_ref, k_ref, vt_ref, sg_ref, *, q_premul, sub):
    for r in range(x_ref.shape[1] // sub):
        rows = slice(r * sub, (r + 1) * sub)
        h = _modulated(x_ref[0, rows, :], shift_ref, scale_ref)
        _attn_project(h, w_ref, qs_ref[...], ks_ref[...], cos_ref[rows, :], sin_ref[rows, :],
                      q_ref, k_ref, vt_ref, sg_ref, rows, q_premul)


def _attn_inproj(x, shift, scale, w, layer, q_scale, k_scale, cos, sin, tm):
    b, n, d = x.shape
    qw = ATT_HEADS * ATT_HEAD_DIM
    kw = ATT_KV_HEADS * ATT_HEAD_DIM
    row = lambda bi, i: (bi, i, 0)
    vec = lambda bi, i: (bi, 0, 0)
    const = lambda bi, i: (0, 0)
    return pl.pallas_call(
        functools.partial(_attn_inproj_kernel, q_premul=Q_PREMUL, sub=math.gcd(tm, OUT_SUB_ROWS)),
        grid=(b, n // tm),
        in_specs=[pl.BlockSpec((1, tm, d), row),
                  pl.BlockSpec((1, 1, d), vec),
                  pl.BlockSpec((1, 1, d), vec),
                  pl.BlockSpec((None,) + w.shape[1:], lambda bi, i: (layer, 0, 0)),
                  pl.BlockSpec((1, ATT_HEAD_DIM), const),
                  pl.BlockSpec((1, ATT_HEAD_DIM), const),
                  pl.BlockSpec((tm, ATT_HEAD_DIM), lambda bi, i: (i, 0)),
                  pl.BlockSpec((tm, ATT_HEAD_DIM), lambda bi, i: (i, 0))],
        out_specs=[pl.BlockSpec((1, tm, qw), row),
                   pl.BlockSpec((1, tm, kw), row),
                   pl.BlockSpec((1, ATT_KV_HEADS, ATT_HEAD_DIM + ONES_ROWS, tm), lambda bi, i: (bi, 0, 0, i)),
                   pl.BlockSpec((1, tm, qw), row)],
        out_shape=[jax.ShapeDtypeStruct((b, n, qw), _BF16),
                   jax.ShapeDtypeStruct((b, n, kw), _BF16),
                   jax.ShapeDtypeStruct((b, ATT_KV_HEADS, ATT_HEAD_DIM + ONES_ROWS, n), _BF16),
                   jax.ShapeDtypeStruct((b, n, qw), _BF16)],
        compiler_params=_params(2),
        name="attn_inproj",
    )(x, shift, scale, w, q_scale, k_scale, cos, sin)


def _ret_project(h, w_ref, cos, sin, qk_ref, v_ref, sg_ref, rows):
    dk = RET_QK_DIM
    half = dk // 2
    qkw = 2 * RET_HEADS * dk
    vw = RET_HEADS * RET_V_DIM
    for hd in range(2 * RET_HEADS):
        p = jnp.dot(h, w_ref[:, hd * dk:(hd + 1) * dk], preferred_element_type=_F32)
        x1, x2 = p[:, :half], p[:, half:]
        o1 = x1 * cos - x2 * sin
        o2 = x1 * sin + x2 * cos
        if hd >= RET_HEADS:
            o1 = o1 * dk ** -0.5
            o2 = o2 * dk ** -0.5
        qk_ref[0, rows, hd * dk:hd * dk + half] = o1.astype(_BF16)
        qk_ref[0, rows, hd * dk + half:(hd + 1) * dk] = o2.astype(_BF16)
    for j in range(vw // MXU_COLS):
        v_ref[0, rows, j * MXU_COLS:(j + 1) * MXU_COLS] = _slab_dot(h, w_ref, qkw + j * MXU_COLS).astype(_BF16)
    for j in range(vw // MXU_COLS):
        g = _slab_dot(h, w_ref, qkw + vw + j * MXU_COLS)
        sg_ref[0, rows, j * MXU_COLS:(j + 1) * MXU_COLS] = _silu(g).astype(_BF16)


def _flash_kernel(*refs, seg_chunks, tq, n_sub):
    q_ref, sg_ref = refs[0], refs[1]
    kv_refs = refs[2:-1]
    t_ref = refs[-1]
    dh = ATT_HEAD_DIM
    grp = ATT_GROUP
    chunks = []
    for si, tk in enumerate(seg_chunks):
        k_ref, vt_ref = kv_refs[2 * si], kv_refs[2 * si + 1]
        chunks += [(k_ref, vt_ref, c * tk, tk) for c in range(k_ref.shape[1] // tk)]
    units = [(sb, ci, i) for ci in range(len(chunks)) for sb in range(n_sub) for i in range(grp)]

    def scores(sb, ci, i):
        k_ref, _, start, tk = chunks[ci]
        q = q_ref[0, sb * tq:(sb + 1) * tq, i * dh:(i + 1) * dh]
        return lax.dot_general(k_ref[0, start:start + tk, :], q, (((1,), (1,)), ((), ())),
                               preferred_element_type=_F32)

    m, acc = {}, {}
    pending = [scores(*u) for u in units[:FLASH_PREFETCH]]
    for n, (sb, ci, i) in enumerate(units):
        _, vt_ref, start, tk = chunks[ci]
        s = pending.pop(0)
        if n + FLASH_PREFETCH < len(units):
            pending.append(scores(*units[n + FLASH_PREFETCH]))
        if ci == 0:
            m[sb, i] = jnp.full((1, tq), -jnp.inf, _F32)
            acc[sb, i] = jnp.zeros((dh + ONES_ROWS, tq), _F32)
        m_new = jnp.maximum(m[sb, i], jnp.max(s, axis=0, keepdims=True))
        alpha = jnp.exp2(m[sb, i] - m_new)
        p = jnp.exp2(s - m_new).astype(_BF16)
        acc[sb, i] = alpha * acc[sb, i] + jnp.dot(vt_ref[0, 0, :, start:start + tk], p,
                                                  preferred_element_type=_F32)
        m[sb, i] = m_new
        if ci == len(chunks) - 1:
            a = acc.pop((sb, i))
            o_t = a[:dh] * (1.0 / a[dh:dh + 1])
            gate = sg_ref[0, sb * tq:(sb + 1) * tq, i * dh:(i + 1) * dh].astype(_F32)
            t_ref[0, sb * tq:(sb + 1) * tq, i * dh:(i + 1) * dh] = (o_t.T * gate).astype(_BF16)


def _flash(q, sg, kv_segments, tq, n_sub=1):
    b, nq, qw = q.shape
    dh = ATT_HEAD_DIM
    gw = ATT_GROUP * dh
    qmap = lambda bi, hi, i: (bi, i, hi)
    kmap = lambda bi, hi, i: (bi, 0, hi)
    vmap = lambda bi, hi, i: (bi, hi, 0, 0)
    tstep = tq * n_sub
    in_specs = [pl.BlockSpec((1, tstep, gw), qmap), pl.BlockSpec((1, tstep, gw), qmap)]
    args = [q, sg]
    for k, vt, _ in kv_segments:
        in_specs += [pl.BlockSpec((1, k.shape[1], dh), kmap),
                     pl.BlockSpec((1, 1, dh + ONES_ROWS, vt.shape[3]), vmap)]
        args += [k, vt]
    return pl.pallas_call(
        functools.partial(_flash_kernel, seg_chunks=tuple(tk for _, _, tk in kv_segments), tq=tq, n_sub=n_sub),
        grid=(b, ATT_KV_HEADS, nq // tstep),
        in_specs=in_specs,
        out_specs=pl.BlockSpec((1, tstep, gw), qmap),
        out_shape=jax.ShapeDtypeStruct((b, nq, qw), _BF16),
        compiler_params=_params(3),
        name="flash_attention",
    )(*args)


def _retention_kernel(lgf_ref, lgb_ref, qc_ref, kc_ref, vc_ref, sgc_ref, ql_ref, kl_ref, vl_ref, sgl_ref,
                      tc_ref, tl_ref, state_f, state_b, fc_ref, fl_ref, *, chunk, unroll):
    hd = pl.program_id(1)
    c = chunk
    n_ctx = qc_ref.shape[1] // c
    n_lat = ql_ref.shape[1] // c
    ii = lax.broadcasted_iota(jnp.int32, (c, c), 0)
    jj = lax.broadcasted_iota(jnp.int32, (c, c), 1)
    row = lax.broadcasted_iota(jnp.int32, (c, 1), 0).astype(_F32)

    def tables(lg, forward):
        diff = (ii - jj) if forward else (jj - ii)
        keep = (diff >= 0) if forward else (diff > 0)
        decay = jnp.where(keep, jnp.exp(lg * jnp.maximum(diff, 0).astype(_F32)), 0.0)
        if forward:
            xi = jnp.exp(lg * (row + 1.0))
            zeta = jnp.exp(lg * (c - 1.0 - row))
        else:
            xi = jnp.exp(lg * (c - row))
            zeta = jnp.exp(lg * row)
        g_chunk = jnp.exp(jnp.full((1, 1), lg * c, _F32))
        return decay, xi, zeta, g_chunk

    def step(q_ref, k_ref, v_ref, sg_ref, f_ref, t_ref, start, tabs, state_ref, final):
        decay, xi, zeta, g_chunk = tabs
        sl = pl.ds(start, c)
        q = q_ref[0, sl, :]
        k = k_ref[0, sl, :]
        v = v_ref[0, sl, :]
        state = state_ref[...]
        s = lax.dot_general(q, k, (((1,), (1,)), ((), ())), preferred_element_type=_F32) * decay
        o = jnp.dot(s.astype(_BF16), v, preferred_element_type=_F32)
        o = o + jnp.dot(q, state.astype(_BF16), preferred_element_type=_F32) * xi
        kz = (k.astype(_F32) * zeta).astype(_BF16)
        upd = lax.dot_general(kz, v, (((0,), (0,)), ((), ())), preferred_element_type=_F32)
        state_ref[...] = state * g_chunk + upd
        if not final:
            f_ref[sl, :] = o
        else:
            o = o + f_ref[sl, :]
            mu = jnp.mean(o, axis=-1, keepdims=True)
            oc = o - mu
            var = jnp.mean(oc * oc, axis=-1, keepdims=True)
            on = (oc * lax.rsqrt(var + GN_EPS)).astype(_BF16)
            t_ref[0, sl, :] = on * sg_ref[0, sl, :]

    tabs_f = tables(lgf_ref[hd], True)
    tabs_b = tables(lgb_ref[hd], False)
    state_f[...] = jnp.zeros_like(state_f)
    state_b[...] = jnp.zeros_like(state_b)
    for i in range(n_ctx):
        step(qc_ref, kc_ref, vc_ref, sgc_ref, fc_ref, tc_ref, i * c, tabs_f, state_f, False)
    for i in reversed(range(n_ctx)):
        step(qc_ref, kc_ref, vc_ref, sgc_ref, fc_ref, tc_ref, i * c, tabs_b, state_b, True)

    half = n_lat // 2

    def lat_body(final):
        def body(i, carry):
            lo = pl.multiple_of(i * c, c)
            hi = pl.multiple_of((n_lat - 1 - i) * c, c)
            step(ql_ref, kl_ref, vl_ref, sgl_ref, fl_ref, tl_ref, lo, tabs_f, state_f, final)
            step(ql_ref, kl_ref, vl_ref, sgl_ref, fl_ref, tl_ref, hi, tabs_b, state_b, final)
            return carry
        return body

    lax.fori_loop(0, half, lat_body(False), 0, unroll=min(2 * unroll, max(half, 1)))
    lax.fori_loop(half, n_lat, lat_body(True), 0, unroll=unroll)


def _retention(lg_f, lg_b, qk_c, v_c, sg_c, qk_l, v_l, sg_l, chunk):
    b, n_lat, _ = qk_l.shape
    n_ctx = qk_c.shape[1]
    assert (n_lat // chunk) % 2 == 0, "the two scan directions meet in the middle of the latents"
    dk, dv, nh = RET_QK_DIM, RET_V_DIM, RET_HEADS
    qmap = lambda bi, hi: (bi, 0, hi)
    kmap = lambda bi, hi: (bi, 0, nh + hi)
    smem = pl.BlockSpec(memory_space=pltpu.SMEM)
    unroll = math.gcd(n_lat // chunk // 2, 4)
    return pl.pallas_call(
        functools.partial(_retention_kernel, chunk=chunk, unroll=unroll),
        grid=(b, nh),
        in_specs=[smem, smem,
                  pl.BlockSpec((1, n_ctx, dk), qmap), pl.BlockSpec((1, n_ctx, dk), kmap),
                  pl.BlockSpec((1, n_ctx, dv), qmap), pl.BlockSpec((1, n_ctx, dv), qmap),
                  pl.BlockSpec((1, n_lat, dk), qmap), pl.BlockSpec((1, n_lat, dk), kmap),
                  pl.BlockSpec((1, n_lat, dv), qmap), pl.BlockSpec((1, n_lat, dv), qmap)],
        out_specs=[pl.BlockSpec((1, n_ctx, dv), qmap), pl.BlockSpec((1, n_lat, dv), qmap)],
        out_shape=[jax.ShapeDtypeStruct((b, n_ctx, nh * dv), _BF16),
                   jax.ShapeDtypeStruct((b, n_lat, nh * dv), _BF16)],
        scratch_shapes=[pltpu.VMEM((dk, dv), _F32),
                        pltpu.VMEM((dk, dv), _F32),
                        pltpu.VMEM((n_ctx, dv), _F32),
                        pltpu.VMEM((n_lat, dv), _F32)],
        compiler_params=_params(2),
        name="retention",
    )(lg_f, lg_b, qk_c, qk_c, v_c, sg_c, qk_l, qk_l, v_l, sg_l)


def _layer_norm_slab(x_ref, gate_ref, lng_ref, lnb_ref, rows, y, eps):
    z = x_ref[0, rows, :] + gate_ref[0] * y
    mu = jnp.mean(z, axis=-1, keepdims=True)
    zc = z - mu
    var = jnp.mean(zc * zc, axis=-1, keepdims=True)
    return zc * lax.rsqrt(var + eps) * lng_ref[...] + lnb_ref[...]


def _outproj_kernel(t_ref, x_ref, w_ref, gate_ref, lng_ref, lnb_ref, o_ref, *, eps, sub):
    n_sub = t_ref.shape[1] // sub
    proj = lambda r: jnp.dot(t_ref[0, r * sub:(r + 1) * sub, :], w_ref[...], preferred_element_type=_F32)
    y_next = proj(0)
    for r in range(n_sub):
        y = y_next
        if r + 1 < n_sub:
            y_next = proj(r + 1)
        rows = slice(r * sub, (r + 1) * sub)
        o_ref[0, rows, :] = _layer_norm_slab(x_ref, gate_ref, lng_ref, lnb_ref, rows, y, eps)


def _outproj(t, x, w, layer, gate, ln_g, ln_b, alpha, tm):
    b, n, d = x.shape
    row = lambda bi, i: (bi, i, 0)
    const = lambda bi, i: (0, 0)
    return pl.pallas_call(
        functools.partial(_outproj_kernel, eps=LN_EPS / alpha ** 2, sub=math.gcd(tm, OUT_SUB_ROWS)),
        grid=(b, n // tm),
        in_specs=[pl.BlockSpec((1, tm, t.shape[2]), row),
                  pl.BlockSpec((1, tm, d), row),
                  pl.BlockSpec((None,) + w.shape[1:], lambda bi, i: (layer, 0, 0)),
                  pl.BlockSpec((1, 1, d), lambda bi, i: (bi, 0, 0)),
                  pl.BlockSpec((1, d), const),
                  pl.BlockSpec((1, d), const)],
        out_specs=pl.BlockSpec((1, tm, d), row),
        out_shape=jax.ShapeDtypeStruct((b, n, d), _F32),
        compiler_params=_params(2),
        name="outproj",
    )(t, x, w, gate, ln_g, ln_b)


def _outproj_ring_kernel(t_hbm, x_hbm, w_ref, gate_ref, lng_ref, lnb_ref, o_ref, tbuf, xbuf, sem,
                         *, eps, sub, n_steps, total, tm):
    lin = pl.program_id(0) * n_steps + pl.program_id(1)

    def copies(step, slot):
        bi = step // n_steps
        r0 = pl.multiple_of((step % n_steps) * tm, tm)
        return (pltpu.make_async_copy(t_hbm.at[bi, pl.ds(r0, tm), :], tbuf.at[slot], sem.at[0, slot]),
                pltpu.make_async_copy(x_hbm.at[bi, pl.ds(r0, tm), :], xbuf.at[slot], sem.at[1, slot]))

    def start(step):
        for c in copies(step, step % RING_SLOTS):
            c.start()

    @pl.when(lin == 0)
    def _():
        for step in range(min(RING_SLOTS - 1, total)):
            start(step)

    @pl.when(lin + RING_SLOTS - 1 < total)
    def _():
        start(lin + RING_SLOTS - 1)

    slot = lin % RING_SLOTS
    for c in copies(lin, slot):
        c.wait()
    _outproj_kernel(tbuf.at[pl.ds(slot, 1)], xbuf.at[pl.ds(slot, 1)], w_ref, gate_ref, lng_ref, lnb_ref, o_ref,
                    eps=eps, sub=sub)


def _outproj_ring(t, x, w, layer, gate, ln_g, ln_b, alpha, tm):
    b, n, d = x.shape
    const = lambda bi, i: (0, 0)
    n_steps = n // tm
    return pl.pallas_call(
        functools.partial(_outproj_ring_kernel, eps=LN_EPS / alpha ** 2, sub=math.gcd(tm, OUT_SUB_ROWS),
                          n_steps=n_steps, total=b * n_steps, tm=tm),
        grid=(b, n_steps),
        in_specs=[pl.BlockSpec(memory_space=pl.ANY),
                  pl.BlockSpec(memory_space=pl.ANY),
                  pl.BlockSpec((None,) + w.shape[1:], lambda bi, i: (layer, 0, 0)),
                  pl.BlockSpec((1, 1, d), lambda bi, i: (bi, 0, 0)),
                  pl.BlockSpec((1, d), const),
                  pl.BlockSpec((1, d), const)],
        out_specs=pl.BlockSpec((1, tm, d), lambda bi, i: (bi, i, 0)),
        out_shape=jax.ShapeDtypeStruct((b, n, d), _F32),
        scratch_shapes=[pltpu.VMEM((RING_SLOTS, tm, t.shape[2]), t.dtype),
                        pltpu.VMEM((RING_SLOTS, tm, d), x.dtype),
                        pltpu.SemaphoreType.DMA((2, RING_SLOTS))],
        compiler_params=_params(2),
        name="outproj_ring",
    )(t, x, w, gate, ln_g, ln_b)


def _outproj_ret_inproj_kernel(t_ref, x_ref, wo_ref, gate_ref, lng_ref, lnb_ref, shift_ref, scale_ref, wi_ref,
                               cos_ref, sin_ref, xo_ref, qk_ref, v_ref, sg_ref, *, eps, sub):
    n_sub = t_ref.shape[1] // sub
    proj = lambda r: jnp.dot(t_ref[0, r * sub:(r + 1) * sub, :], wo_ref[...], preferred_element_type=_F32)
    y_next = proj(0)
    for r in range(n_sub):
        y = y_next
        if r + 1 < n_sub:
            y_next = proj(r + 1)
        rows = slice(r * sub, (r + 1) * sub)
        x_new = _layer_norm_slab(x_ref, gate_ref, lng_ref, lnb_ref, rows, y, eps)
        xo_ref[0, rows, :] = x_new
        h = _modulated(x_new, shift_ref, scale_ref)
        _ret_project(h, wi_ref, cos_ref[rows, :], sin_ref[rows, :], qk_ref, v_ref, sg_ref, rows)


def _outproj_ret_inproj(t, x, wo, lo, gate, ln_g, ln_b, alpha, shift, scale, wi, li, cos, sin, tm):
    b, n, d = x.shape
    qkw = 2 * RET_HEADS * RET_QK_DIM
    vw = RET_HEADS * RET_V_DIM
    half = RET_QK_DIM // 2
    row = lambda bi, i: (bi, i, 0)
    vec = lambda bi, i: (bi, 0, 0)
    const = lambda bi, i: (0, 0)
    return pl.pallas_call(
        functools.partial(_outproj_ret_inproj_kernel, eps=LN_EPS / alpha ** 2, sub=math.gcd(tm, OUT_SUB_ROWS)),
        grid=(b, n // tm),
        in_specs=[pl.BlockSpec((1, tm, t.shape[2]), row),
                  pl.BlockSpec((1, tm, d), row),
                  pl.BlockSpec((None,) + wo.shape[1:], lambda bi, i: (lo, 0, 0), pipeline_mode=pl.Buffered(1)),
                  pl.BlockSpec((1, 1, d), vec),
                  pl.BlockSpec((1, d), const),
                  pl.BlockSpec((1, d), const),
                  pl.BlockSpec((1, 1, d), vec),
                  pl.BlockSpec((1, 1, d), vec),
                  pl.BlockSpec((None,) + wi.shape[1:], lambda bi, i: (li, 0, 0), pipeline_mode=pl.Buffered(1)),
                  pl.BlockSpec((tm, half), lambda bi, i: (i, 0)),
                  pl.BlockSpec((tm, half), lambda bi, i: (i, 0))],
        out_specs=[pl.BlockSpec((1, tm, d), row),
                   pl.BlockSpec((1, tm, qkw), row),
                   pl.BlockSpec((1, tm, vw), row),
                   pl.BlockSpec((1, tm, vw), row)],
        out_shape=[jax.ShapeDtypeStruct((b, n, d), _F32),
                   jax.ShapeDtypeStruct((b, n, qkw), _BF16),
                   jax.ShapeDtypeStruct((b, n, vw), _BF16),
                   jax.ShapeDtypeStruct((b, n, vw), _BF16)],
        compiler_params=_params(2),
        name="outproj_ret_inproj",
    )(t, x, wo, gate, ln_g, ln_b, shift, scale, wi, cos, sin)


def _outproj_attn_inproj_kernel(t_ref, x_ref, wo_ref, gate_ref, lng_ref, lnb_ref, shift_ref, scale_ref, wi_ref,
                                qs_ref, ks_ref, cos_ref, sin_ref, xo_ref, q_ref, k_ref, vt_ref, sg_ref, *, eps, sub):
    n_sub = t_ref.shape[1] // sub
    proj = lambda r: jnp.dot(t_ref[0, r * sub:(r + 1) * sub, :], wo_ref[...], preferred_element_type=_F32)
    y_next = proj(0)
    for r in range(n_sub):
        y = y_next
        if r + 1 < n_sub:
            y_next = proj(r + 1)
        rows = slice(r * sub, (r + 1) * sub)
        x_new = _layer_norm_slab(x_ref, gate_ref, lng_ref, lnb_ref, rows, y, eps)
        xo_ref[0, rows, :] = x_new
        h = _modulated(x_new, shift_ref, scale_ref)
        _attn_project(h, wi_ref, qs_ref[...], ks_ref[...], cos_ref[rows, :], sin_ref[rows, :],
                      q_ref, k_ref, vt_ref, sg_ref, rows, Q_PREMUL)


def _outproj_attn_inproj(t, x, wo, lo, gate, ln_g, ln_b, alpha, shift, scale, wi, li, q_scale, k_scale, cos, sin, tm):
    b, n, d = x.shape
    qw = ATT_HEADS * ATT_HEAD_DIM
    kw = ATT_KV_HEADS * ATT_HEAD_DIM
    row = lambda bi, i: (bi, i, 0)
    vec = lambda bi, i: (bi, 0, 0)
    const = lambda bi, i: (0, 0)
    return pl.pallas_call(
        functools.partial(_outproj_attn_inproj_kernel, eps=LN_EPS / alpha ** 2, sub=math.gcd(tm, OUT_SUB_ROWS)),
        grid=(b, n // tm),
        in_specs=[pl.BlockSpec((1, tm, t.shape[2]), row),
                  pl.BlockSpec((1, tm, d), row),
                  pl.BlockSpec((None,) + wo.shape[1:], lambda bi, i: (lo, 0, 0), pipeline_mode=pl.Buffered(1)),
                  pl.BlockSpec((1, 1, d), vec),
                  pl.BlockSpec((1, d), const),
                  pl.BlockSpec((1, d), const),
                  pl.BlockSpec((1, 1, d), vec),
                  pl.BlockSpec((1, 1, d), vec),
                  pl.BlockSpec((None,) + wi.shape[1:], lambda bi, i: (li, 0, 0), pipeline_mode=pl.Buffered(1)),
                  pl.BlockSpec((1, ATT_HEAD_DIM), const),
                  pl.BlockSpec((1, ATT_HEAD_DIM), const),
                  pl.BlockSpec((tm, ATT_HEAD_DIM), lambda bi, i: (i, 0)),
                  pl.BlockSpec((tm, ATT_HEAD_DIM), lambda bi, i: (i, 0))],
        out_specs=[pl.BlockSpec((1, tm, d), row),
                   pl.BlockSpec((1, tm, qw), row),
                   pl.BlockSpec((1, tm, kw), row),
                   pl.BlockSpec((1, ATT_KV_HEADS, ATT_HEAD_DIM + ONES_ROWS, tm), lambda bi, i: (bi, 0, 0, i)),
                   pl.BlockSpec((1, tm, qw), row)],
        out_shape=[jax.ShapeDtypeStruct((b, n, d), _F32),
                   jax.ShapeDtypeStruct((b, n, qw), _BF16),
                   jax.ShapeDtypeStruct((b, n, kw), _BF16),
                   jax.ShapeDtypeStruct((b, ATT_KV_HEADS, ATT_HEAD_DIM + ONES_ROWS, n), _BF16),
                   jax.ShapeDtypeStruct((b, n, qw), _BF16)],
        compiler_params=_params(2),
        name="outproj_attn_inproj",
    )(t, x, wo, gate, ln_g, ln_b, shift, scale, wi, q_scale, k_scale, cos, sin)


def _axial_perm():
    quarter = ATT_HEAD_DIM // 4
    return np.concatenate([np.arange(quarter) + off * quarter for off in (0, 2, 1, 3)])


def _axial_tables(s):
    quarter = ATT_HEAD_DIM // 4
    t = np.arange(s)
    pos = np.stack([t // GRID_W, t % GRID_W], axis=1).astype(np.float32)
    freqs = (ROPE_THETA ** (-np.arange(quarter, dtype=np.float32) / quarter)).astype(np.float32)
    ang = (pos[:, :, None] * freqs[None, None, :]).reshape(s, 2 * quarter)
    cos = np.concatenate([np.cos(ang), np.cos(ang)], axis=-1)
    sin = np.concatenate([-np.sin(ang), np.sin(ang)], axis=-1)
    return cos.astype(np.float32), sin.astype(np.float32)


def _permute_qk_columns(w_in):
    dh = ATT_HEAD_DIM
    qw = ATT_HEADS * dh
    kw = ATT_KV_HEADS * dh
    perm = _axial_perm()
    q_cols = (jnp.arange(ATT_HEADS)[:, None] * dh + perm[None, :]).reshape(-1)
    k_cols = 2 * qw + (jnp.arange(ATT_KV_HEADS)[:, None] * dh + perm[None, :]).reshape(-1)
    cols = jnp.concatenate([q_cols, jnp.arange(qw, 2 * qw), k_cols, jnp.arange(2 * qw + kw, 2 * qw + 2 * kw)])
    return w_in[..., cols]


def _rope_tables(pos):
    half = RET_QK_DIM // 2
    freqs = (ROPE_THETA ** (-np.arange(half, dtype=np.float32) / half)).astype(np.float32)
    ang = pos.astype(np.float32)[:, None] * freqs[None, :]
    return np.cos(ang).astype(np.float32), np.sin(ang).astype(np.float32)


def _row_tile(n, want):
    return want if n % want == 0 else n


def kernel(x, c, ctx, c_ctx, mod_w, mod_b, ln_g, ln_b, attn_w_in, attn_w_out, attn_q_scale, attn_k_scale,
           ret_w_in, ret_w_out, ret_gn_g, ret_log_decay_fwd, ret_log_decay_bwd):
    b, s, d = x.shape
    l = ctx.shape[1]
    depth = mod_w.shape[0]
    alpha = (2.0 * depth) ** 0.25

    rows = 8 * ((b + 1 + 7) // 8)
    cvec = jnp.zeros((rows, d), _F32).at[:b].set(c).at[b].set(c_ctx)
    mods = _modulation(cvec, mod_w, mod_b, alpha)

    cos_ax, sin_ax = _axial_tables(s)
    cos_id, sin_id = np.ones((l, ATT_HEAD_DIM), np.float32), np.zeros((l, ATT_HEAD_DIM), np.float32)
    cos_c, sin_c = _rope_tables(np.arange(l))
    cos_l, sin_l = _rope_tables(l + np.arange(s))
    attn_w_out_b = attn_w_out.astype(_BF16)
    ret_w_in_b = ret_w_in.astype(_BF16)
    ret_w_out_b = (ret_gn_g[:, :, None] * ret_w_out).astype(_BF16)

    tm_l = _row_tile(s, 512)
    tm_attn = _row_tile(s, 1024)
    tm_c = _row_tile(l, 256)
    tm_out = _row_tile(s, 1024)
    tq = _row_tile(s, 256)
    tk = _row_tile(s, 512)
    chunk = RET_CHUNK if (l % RET_CHUNK == 0 and s % RET_CHUNK == 0) else 128

    def mod_vectors(i):
        lat = [mods[i, :b, None, k * d:(k + 1) * d] for k in range(3)]
        cx = [jnp.broadcast_to(mods[i, b, None, None, k * d:(k + 1) * d], (b, 1, d)) for k in range(3)]
        return lat, cx

    def attn_weights(j):
        w_in = _permute_qk_columns(attn_w_in[j]).astype(_BF16)[None]
        return w_in, attn_q_scale[j][_axial_perm()][None, :], attn_k_scale[j][_axial_perm()][None, :]

    projected = None
    for i in range(depth):
        need_ctx = i < depth - 1
        j = i // 2
        (shift_l, scale_l, gate_l), (shift_c, scale_c, gate_c) = mod_vectors(i)
        lng = ln_g[i][None, :]
        lnb = ln_b[i][None, :]
        if i % 2 == 0:
            w_out = attn_w_out_b
            if projected is None:
                w_in, qs, ks = attn_weights(j)
                projected = (_attn_inproj(x, shift_l, scale_l, w_in, 0, qs, ks, cos_ax, sin_ax, tm_attn),
                             _attn_inproj(ctx, shift_c, scale_c, w_in, 0, qs, ks, cos_id, sin_id, tm_c))
            (q_l, k_l, vt_l, sg_l), (q_c, k_c, vt_c, sg_c) = projected
            t_l = _flash(q_l, sg_l, [(k_l, vt_l, tk), (k_c, vt_c, l)], tq,
                         FLASH_SUB_BLOCKS if s % (tq * FLASH_SUB_BLOCKS) == 0 else 1)
            if need_ctx:
                t_c = _flash(q_c, sg_c, [(k_c, vt_c, l)], _row_tile(l, 256))
        else:
            w_out = ret_w_out_b
            (qk_l, v_l, sg_l), (qk_c, v_c, sg_c) = projected
            t_c, t_l = _retention(ret_log_decay_fwd[j], ret_log_decay_bwd[j],
                                  qk_c, v_c, sg_c, qk_l, v_l, sg_l, chunk)
        if i + 1 == depth:
            x = _outproj_ring(t_l, x, w_out, j, gate_l, lng, lnb, alpha, tm_out)
            continue
        (shift_n, scale_n, _), (shift_nc, scale_nc, _) = mod_vectors(i + 1)
        jn = (i + 1) // 2
        head = (w_out, j)
        if (i + 1) % 2 == 1:
            tail = (ret_w_in_b, jn)
            x, *proj_l = _outproj_ret_inproj(t_l, x, *head, gate_l, lng, lnb, alpha, shift_n, scale_n, *tail,
                                             cos_l, sin_l, tm_l)
            ctx, *proj_c = _outproj_ret_inproj(t_c, ctx, *head, gate_c, lng, lnb, alpha, shift_nc, scale_nc, *tail,
                                               cos_c, sin_c, tm_c)
        else:
            w_in, qs, ks = attn_weights(jn)
            tail = (w_in, 0, qs, ks)
            x, *proj_l = _outproj_attn_inproj(t_l, x, *head, gate_l, lng, lnb, alpha, shift_n, scale_n, *tail,
                                              cos_ax, sin_ax, tm_attn)
            ctx, *proj_c = _outproj_attn_inproj(t_c, ctx, *head, gate_c, lng, lnb, alpha, shift_nc, scale_nc, *tail,
                                                cos_id, sin_id, tm_c)
        projected = (proj_l, proj_c)
    return x
```
